```python
import math
import jax, jax.numpy as jnp
from jax import lax
import numpy as np

D_MODEL = 1024
BATCH = 2
SEQ = 16384
DEPTH = 2

PLE_DIM = 256
HEAD_DIM = 64
DIFF_HEADS = 8
DIFF_QK = DIFF_HEADS * 2 * HEAD_DIM
DIFF_V = DIFF_HEADS * 2 * HEAD_DIM
SWA_HEADS = 16
SWA_KV_HEADS = 2
SWA_GROUP = SWA_HEADS // SWA_KV_HEADS
SWA_Q = SWA_HEADS * HEAD_DIM
SWA_KV = SWA_KV_HEADS * HEAD_DIM
WINDOW = 128
Q_BLOCK = 128
IN_SPLITS = (DIFF_QK, DIFF_QK, DIFF_V, SWA_Q, SWA_KV, SWA_KV, D_MODEL, D_MODEL)
IN_WIDTH = sum(IN_SPLITS)
N_EXPERTS = 32
TOP_K = 4
D_FF = D_MODEL
SWIGLU_LIMIT = 7.0
SWIGLU_ALPHA = 1.702
MOE_BLOCK = 512
LN_EPS = 1e-5
RMS_EPS = 1e-5
DN_ALPHA = (2 * DEPTH) ** 0.25
DN_BETA = (8 * DEPTH) ** -0.25

kernel_name = "hybrid_diffattn_swa_sinks_moe_deepnorm"


def layer_norm(x, g, b):
    xf = x.astype(jnp.float32)
    mu = jnp.mean(xf, axis=-1, keepdims=True)
    var = jnp.mean(jnp.square(xf - mu), axis=-1, keepdims=True)
    y = (xf - mu) * lax.rsqrt(var + LN_EPS)
    return (y * g.astype(jnp.float32) + b.astype(jnp.float32)).astype(x.dtype)


def rms_norm(x, w):
    xf = x.astype(jnp.float32)
    y = xf * lax.rsqrt(jnp.mean(jnp.square(xf), axis=-1, keepdims=True) + RMS_EPS)
    return (y * w.astype(jnp.float32)).astype(x.dtype)


def alibi_slopes(n_heads):
    h = jnp.arange(1, n_heads + 1, dtype=jnp.float32)
    return jnp.exp2(-8.0 * h / n_heads)


def split_in(proj):
    idx, acc = [], 0
    for w in IN_SPLITS[:-1]:
        acc += w
        idx.append(acc)
    return jnp.split(proj, idx, axis=-1)


def diff_attention(q, k, v, lam, subln_w, lam_init):
    B, S = q.shape[0], q.shape[1]
    nb = S // Q_BLOCK
    scale = HEAD_DIM ** -0.5
    slopes = alibi_slopes(DIFF_HEADS)
    q_blocks = jnp.moveaxis(q.reshape(B, nb, Q_BLOCK, DIFF_HEADS, 2, HEAD_DIM), 1, 0)
    key_pos = jnp.arange(S)

    def one_block(args):
        qb, n = args
        s = jnp.einsum('bqhcd,bkhcd->bhcqk', qb, k).astype(jnp.float32) * scale
        dist = (n * Q_BLOCK + jnp.arange(Q_BLOCK))[:, None] - key_pos[None, :]
        s = s - slopes[None, :, None, None, None] * jnp.abs(dist).astype(jnp.float32)
        s = jnp.where(dist >= 0, s, -jnp.inf)
        prob = jax.nn.softmax(s, axis=-1)
        a = (prob[:, :, 0] - lam * prob[:, :, 1]).astype(v.dtype)
        return jnp.einsum('bhqk,bkhe->bqhe', a, v)

    o = lax.map(one_block, (q_blocks, jnp.arange(nb)))
    o = jnp.moveaxis(o, 0, 1).reshape(B, S, DIFF_HEADS, 2 * HEAD_DIM)
    o = rms_norm(o, subln_w) * (1.0 - lam_init)
    return o.reshape(B, S, DIFF_V)


def swa_attention(q, k, v, sinks):
    B, S = q.shape[0], q.shape[1]
    nb = S // Q_BLOCK
    scale = HEAD_DIM ** -0.5

    def blocks(t):
        return t.reshape((B, nb, Q_BLOCK) + t.shape[2:])

    def with_prev(t):
        prev = jnp.concatenate([jnp.zeros_like(t[:, :1]), t[:, :-1]], axis=1)
        return jnp.concatenate([prev, t], axis=2)

    qb = blocks(q)
    kb = with_prev(blocks(k))
    vb = with_prev(blocks(v))
    s = jnp.einsum('bnqhgd,bnkhd->bnhgqk', qb, kb).astype(jnp.float32) * scale
    kj = jnp.arange(2 * Q_BLOCK)
    dist = (jnp.arange(Q_BLOCK) + Q_BLOCK)[:, None] - kj[None, :]
    key_pos = jnp.arange(nb)[:, None] * Q_BLOCK - Q_BLOCK + kj[None, :]
    valid = (dist >= 0)[None] & (dist < WINDOW)[None] & (key_pos >= 0)[:, None, :]
    slopes = alibi_slopes(SWA_HEADS).reshape(SWA_KV_HEADS, SWA_GROUP)
    s = s - slopes[:, :, None, None] * jnp.abs(dist).astype(jnp.float32)
    s = jnp.where(valid[None, :, None, None], s, -jnp.inf)
    sink = sinks.astype(jnp.float32).reshape(SWA_KV_HEADS, SWA_GROUP)[:, :, None, None]
    m = jnp.maximum(jnp.max(s, axis=-1, keepdims=True), sink)
    e = jnp.exp(s - m)
    denom = jnp.sum(e, axis=-1, keepdims=True) + jnp.exp(sink - m)
    prob = (e / denom).astype(v.dtype)
    o = jnp.einsum('bnhgqk,bnkhd->bnqhgd', prob, vb)
    return o.reshape(B, S, SWA_Q)


def moe(h, w_router, b_router, w_gu, b_gu, w_down, b_down):
    B, S, D = h.shape
    N = B * S
    hf = h.reshape(N, D)
    logits = (hf @ w_router + b_router).astype(jnp.float32)
    top_logits, top_idx = lax.top_k(logits, TOP_K)
    gates = jax.nn.softmax(top_logits, axis=-1)
    A = N * TOP_K
    n_blocks = A // MOE_BLOCK + N_EXPERTS + 1
    flat_e = top_idx.reshape(A)
    order = jnp.argsort(flat_e)
    sorted_e = flat_e[order]
    counts = jnp.bincount(flat_e, length=N_EXPERTS)
    padded = (counts + MOE_BLOCK - 1) // MOE_BLOCK * MOE_BLOCK
    padded_end = jnp.cumsum(padded)
    rank = jnp.arange(A) - (jnp.cumsum(counts) - counts)[sorted_e]
    dest = (padded_end - padded)[sorted_e] + rank
    n_slots = n_blocks * MOE_BLOCK
    slot_tok = jnp.full((n_slots,), N, jnp.int32).at[dest].set((order // TOP_K).astype(jnp.int32))
    slot_gate = jnp.zeros((n_slots,), jnp.float32).at[dest].set(gates.reshape(A)[order])
    block_e = jnp.minimum(
        jnp.searchsorted(padded_end, jnp.arange(n_blocks) * MOE_BLOCK, side='right'),
        N_EXPERTS - 1)
    hf_pad = jnp.concatenate([hf, jnp.zeros((1, D), hf.dtype)], axis=0)

    def expert_block(args):
        tok, g, e = args
        xb = hf_pad[tok]
        gu = xb @ w_gu[e] + b_gu[e]
        gate, up = jnp.split(gu, 2, axis=-1)
        gate = jnp.minimum(gate, SWIGLU_LIMIT)
        up = jnp.clip(up, -SWIGLU_LIMIT, SWIGLU_LIMIT)
        act = (up + 1.0) * (gate * jax.nn.sigmoid(SWIGLU_ALPHA * gate))
        y = act @ w_down[e] + b_down[e]
        return y * g[:, None].astype(y.dtype)

    y = lax.map(expert_block, (slot_tok.reshape(n_blocks, MOE_BLOCK),
                               slot_gate.reshape(n_blocks, MOE_BLOCK), block_e))
    out = jnp.zeros((N + 1, D), y.dtype).at[slot_tok].add(y.reshape(n_slots, D))
    return out[:N].reshape(B, S, D)


def setup_inputs(seed: int = 0) -> dict:
    key = jax.random.key(seed)
    ks = jax.random.split(key, 26)
    f32 = jnp.float32

    def nrm(k, shape, scale):
        return jax.random.normal(k, shape, f32) * scale

    col_scale = jnp.concatenate([
        jnp.ones((2 * DIFF_QK,), f32), jnp.full((DIFF_V,), DN_BETA, f32),
        jnp.ones((SWA_Q + SWA_KV,), f32), jnp.full((SWA_KV,), DN_BETA, f32),
        jnp.ones((2 * D_MODEL,), f32)])
    return {
        "x": nrm(ks[0], (BATCH, SEQ, D_MODEL), 1.0),
        "p": nrm(ks[1], (DEPTH, BATCH, SEQ, PLE_DIM), 1.0),
        "w_in": nrm(ks[2], (DEPTH, D_MODEL, IN_WIDTH), D_MODEL ** -0.5) * col_scale,
        "b_in": nrm(ks[3], (DEPTH, IN_WIDTH), 0.01),
        "lambda_q1": nrm(ks[4], (DEPTH, HEAD_DIM), 0.1),
        "lambda_k1": nrm(ks[5], (DEPTH, HEAD_DIM), 0.1),
        "lambda_q2": nrm(ks[6], (DEPTH, HEAD_DIM), 0.1),
        "lambda_k2": nrm(ks[7], (DEPTH, HEAD_DIM), 0.1),
        "subln_w": 1.0 + nrm(ks[8], (DEPTH, 2 * HEAD_DIM), 0.01),
        "sinks": nrm(ks[9], (DEPTH, SWA_HEADS), 0.5),
        "w_br_diff": nrm(ks[10], (DEPTH, DIFF_V, D_MODEL), DIFF_V ** -0.5),
        "w_br_swa": nrm(ks[11], (DEPTH, SWA_Q, D_MODEL), SWA_Q ** -0.5),
        "w_out": nrm(ks[12], (DEPTH, D_MODEL, D_MODEL), D_MODEL ** -0.5 * DN_BETA),
        "b_out": nrm(ks[13], (DEPTH, D_MODEL), 0.01),
        "ln1_g": 1.0 + nrm(ks[14], (DEPTH, D_MODEL), 0.01),
        "ln1_b": nrm(ks[15], (DEPTH, D_MODEL), 0.01),
        "w_router": nrm(ks[16], (DEPTH, D_MODEL, N_EXPERTS), D_MODEL ** -0.5),
        "b_router": nrm(ks[17], (DEPTH, N_EXPERTS), 0.01),
        "w_gate_up": nrm(ks[18], (DEPTH, N_EXPERTS, D_MODEL, 2 * D_FF), D_MODEL ** -0.5),
        "b_gate_up": nrm(ks[19], (DEPTH, N_EXPERTS, 2 * D_FF), 0.01),
        "w_down": nrm(ks[20], (DEPTH, N_EXPERTS, D_FF, D_MODEL), D_FF ** -0.5 * DN_BETA),
        "b_down": nrm(ks[21], (DEPTH, N_EXPERTS, D_MODEL), 0.01),
        "w_ple_gate": nrm(ks[22], (DEPTH, D_MODEL, D_MODEL), D_MODEL ** -0.5),
        "w_ple_proj": nrm(ks[23], (DEPTH, PLE_DIM, D_MODEL), PLE_DIM ** -0.5 * DN_BETA),
        "ln2_g": 1.0 + nrm(ks[24], (DEPTH, D_MODEL), 0.01),
        "ln2_b": nrm(ks[25], (DEPTH, D_MODEL), 0.01),
    }


def reference(x, p, w_in, b_in, lambda_q1, lambda_k1, lambda_q2, lambda_k2, subln_w, sinks,
              w_br_diff, w_br_swa, w_out, b_out, ln1_g, ln1_b, w_router, b_router,
              w_gate_up, b_gate_up, w_down, b_down, w_ple_gate, w_ple_proj, ln2_g, ln2_b):
    B, S, _ = x.shape
    for i in range(DEPTH):
        lam_init = 0.8 - 0.6 * math.exp(-0.3 * i)
        proj = x @ w_in[i] + b_in[i]
        dq, dk, dv, sq, sk, sv, ga, gb = split_in(proj)
        lam = (jnp.exp(jnp.sum(lambda_q1[i].astype(jnp.float32) * lambda_k1[i].astype(jnp.float32)))
               - jnp.exp(jnp.sum(lambda_q2[i].astype(jnp.float32) * lambda_k2[i].astype(jnp.float32)))
               + lam_init)
        o_diff = diff_attention(dq.reshape(B, S, DIFF_HEADS, 2, HEAD_DIM),
                                dk.reshape(B, S, DIFF_HEADS, 2, HEAD_DIM),
                                dv.reshape(B, S, DIFF_HEADS, 2 * HEAD_DIM),
                                lam, subln_w[i], lam_init)
        o_swa = swa_attention(sq.reshape(B, S, SWA_KV_HEADS, SWA_GROUP, HEAD_DIM),
                              sk.reshape(B, S, SWA_KV_HEADS, HEAD_DIM),
                              sv.reshape(B, S, SWA_KV_HEADS, HEAD_DIM),
                              sinks[i])
        merged = (jax.nn.sigmoid(ga) * (o_diff @ w_br_diff[i])
                  + jax.nn.sigmoid(gb) * (o_swa @ w_br_swa[i]))
        mix = merged @ w_out[i] + b_out[i]
        x = layer_norm(DN_ALPHA * x + mix, ln1_g[i], ln1_b[i])
        ffn = moe(x, w_router[i], b_router[i], w_gate_up[i], b_gate_up[i], w_down[i], b_down[i])
        ple = jax.nn.sigmoid(x @ w_ple_gate[i]) * (p[i] @ w_ple_proj[i])
        x = layer_norm(DN_ALPHA * x + ffn + ple, ln2_g[i], ln2_b[i])
    return x
```

```python
import functools
import math

import jax
import jax.numpy as jnp
from jax import lax
from jax.experimental import pallas as pl
from jax.experimental.pallas import tpu as pltpu

F32 = jnp.float32
BF16 = jnp.bfloat16

HEAD_DIM = 64
DIFF_HEADS = 8
SWA_HEADS = 16
SWA_KV_HEADS = 2
SWA_GROUP = SWA_HEADS // SWA_KV_HEADS
WINDOW = 128
N_EXPERTS = 32
TOP_K = 4
MOE_BLOCK = 512
SWIGLU_LIMIT = 7.0
SWIGLU_ALPHA = 1.702
LN_EPS = 1e-5
RMS_EPS = 1e-5
NEG_BIG = -1e30

LANES = 128
VMEM_LIMIT = 56 * 1024 * 1024
RANK_BITS = 16

_COL_GA, _COL_GB, _COL_DQ, _COL_DK, _COL_DV, _COL_SQ, _COL_SK, _COL_SV = (
    0, 1024, 2048, 3072, 4096, 5120, 6144, 6272)


def _params(semantics):
    return pltpu.CompilerParams(dimension_semantics=semantics,
                                vmem_limit_bytes=VMEM_LIMIT)


def _linear_kernel(x_ref, w_ref, b_ref, o_ref):
    acc = jnp.dot(x_ref[...], w_ref[...], preferred_element_type=F32)
    o_ref[...] = (acc + b_ref[...]).astype(o_ref.dtype)


def _linear(x, w, b, *, tm, tn):
    n, k = x.shape
    nout = w.shape[1]
    return pl.pallas_call(
        _linear_kernel,
        out_shape=jax.ShapeDtypeStruct((n, nout), BF16),
        grid=(n // tm, nout // tn),
        in_specs=[pl.BlockSpec((tm, k), lambda i, j: (i, 0)),
                  pl.BlockSpec((k, tn), lambda i, j: (0, j)),
                  pl.BlockSpec((1, tn), lambda i, j: (0, j))],
        out_specs=pl.BlockSpec((tm, tn), lambda i, j: (i, j)),
        compiler_params=_params(("parallel", "arbitrary")),
        name="in_proj",
    )(x, w, b)


def _diff_attn_kernel(slopes_ref, lq1_ref, lk1_ref, lq2_ref, lk2_ref,
                      q_ref, k_ref, v_ref, w_ref, o_ref,
                      m1, l1, a1, m2, l2, a2, *, tq, lam_init):
    h = pl.program_id(1)
    qi = pl.program_id(2)
    slope = slopes_ref[h]
    scale = HEAD_DIM ** -0.5
    q = q_ref[...] * scale
    lane = lax.broadcasted_iota(jnp.int32, q.shape, 1)
    zero = jnp.zeros_like(q)
    qa = jnp.where(lane < HEAD_DIM, q, zero)
    qb = jnp.where(lane >= HEAD_DIM, q, zero)
    for m_sc, l_sc, a_sc in ((m1, l1, a1), (m2, l2, a2)):
        m_sc[...] = jnp.full(m_sc.shape, NEG_BIG, F32)
        l_sc[...] = jnp.zeros(l_sc.shape, F32)
        a_sc[...] = jnp.zeros(a_sc.shape, F32)
    qstart = qi * tq

    def block(j, masked):
        kstart = pl.multiple_of(j * tq, tq)
        k = k_ref[pl.ds(kstart, tq), :]
        v = v_ref[pl.ds(kstart, tq), :]
        kpos = kstart + lax.broadcasted_iota(jnp.int32, (1, tq), 1)
        bias = slope * (kpos - qstart).astype(F32)
        if masked:
            row = lax.broadcasted_iota(jnp.int32, (tq, tq), 0)
            col = lax.broadcasted_iota(jnp.int32, (tq, tq), 1)
            keep = col <= row
        for qq, m_sc, l_sc, a_sc in ((qa, m1, l1, a1), (qb, m2, l2, a2)):
            s = lax.dot_general(qq, k, (((1,), (1,)), ((), ())),
                                preferred_element_type=F32) + bias
            if masked:
                s = jnp.where(keep, s, NEG_BIG)
            m_old = m_sc[...]
            m_new = jnp.maximum(m_old, jnp.max(s, axis=1, keepdims=True))
            alpha = jnp.exp(m_old - m_new)
            p = jnp.exp(s - m_new)
            l_sc[...] = alpha * l_sc[...] + jnp.sum(p, axis=1, keepdims=True)
            a_sc[...] = alpha * a_sc[...] + jnp.dot(p.astype(BF16), v,
                                                    preferred_element_type=F32)
            m_sc[...] = m_new

    def body(j, carry):
        block(j, False)
        return carry

    lax.fori_loop(0, qi, body, 0)
    block(qi, True)

    lam = (jnp.exp(jnp.sum(lq1_ref[...] * lk1_ref[...], axis=1, keepdims=True))
           - jnp.exp(jnp.sum(lq2_ref[...] * lk2_ref[...], axis=1, keepdims=True))
           + lam_init)
    o = a1[...] / l1[...] - lam * (a2[...] / l2[...])
    y = o * lax.rsqrt(jnp.mean(jnp.square(o), axis=1, keepdims=True) + RMS_EPS)
    y = (y * w_ref[...]) * (1.0 - lam_init)
    o_ref[...] = y.astype(o_ref.dtype)


def _diff_attention(proj, slopes, lq1, lk1, lq2, lk2, subln_w, *, batch, seq, lam_init, tq):
    n = proj.shape[0]
    nq = seq // tq
    kern = functools.partial(_diff_attn_kernel, tq=tq, lam_init=lam_init)
    vec = pl.BlockSpec((1, HEAD_DIM), lambda b, h, i: (0, 0))
    cq, ck, cv = _COL_DQ // LANES, _COL_DK // LANES, _COL_DV // LANES
    return pl.pallas_call(
        kern,
        out_shape=jax.ShapeDtypeStruct((n, DIFF_HEADS * 2 * HEAD_DIM), BF16),
        grid=(batch, DIFF_HEADS, nq),
        in_specs=[pl.BlockSpec(memory_space=pltpu.SMEM),
                  vec, vec, vec, vec,
                  pl.BlockSpec((tq, LANES), lambda b, h, i: (b * nq + i, cq + h)),
                  pl.BlockSpec((seq, LANES), lambda b, h, i: (b, ck + h)),
                  pl.BlockSpec((seq, LANES), lambda b, h, i: (b, cv + h)),
                  pl.BlockSpec((1, 2 * HEAD_DIM), lambda b, h, i: (0, 0))],
        out_specs=pl.BlockSpec((tq, LANES), lambda b, h, i: (b * nq + i, h)),
        scratch_shapes=[pltpu.VMEM((tq, 1), F32), pltpu.VMEM((tq, 1), F32),
                        pltpu.VMEM((tq, 2 * HEAD_DIM), F32),
                        pltpu.VMEM((tq, 1), F32), pltpu.VMEM((tq, 1), F32),
                        pltpu.VMEM((tq, 2 * HEAD_DIM), F32)],
        compiler_params=_params(("parallel", "parallel", "arbitrary")),
        name="diff_attn",
    )(slopes, lq1, lk1, lq2, lk2, proj, proj, proj, subln_w)


def _swa_kernel(slopes_ref, sinks_ref, q_ref, kp_ref, kc_ref, vp_ref, vc_ref, o_ref, *, tq):
    qi = pl.program_id(1)
    scale = HEAD_DIM ** -0.5
    kcat = jnp.concatenate([kp_ref[...], kc_ref[...]], axis=0)
    vcat = jnp.concatenate([vp_ref[...], vc_ref[...]], axis=0)
    row = lax.broadcasted_iota(jnp.int32, (WINDOW, 2 * WINDOW), 0)
    col = lax.broadcasted_iota(jnp.int32, (WINDOW, 2 * WINDOW), 1)
    dist = row + WINDOW - col
    valid = (dist >= 0) & (dist < WINDOW)
    distf = dist.astype(F32)
    valid_first = valid & ((col >= WINDOW) | (qi > 0))
    for j in range(tq // WINDOW):
        kj = kcat[j * WINDOW:(j + 2) * WINDOW]
        vj = vcat[j * WINDOW:(j + 2) * WINDOW]
        qj = q_ref[j * WINDOW:(j + 1) * WINDOW, :] * scale
        vmask = valid_first if j == 0 else valid
        for hk in range(SWA_KV_HEADS):
            kk = kj[:, hk * HEAD_DIM:(hk + 1) * HEAD_DIM]
            vv = vj[:, hk * HEAD_DIM:(hk + 1) * HEAD_DIM]
            for g in range(SWA_GROUP):
                hq = hk * SWA_GROUP + g
                qh = qj[:, hq * HEAD_DIM:(hq + 1) * HEAD_DIM]
                s = lax.dot_general(qh, kk, (((1,), (1,)), ((), ())),
                                    preferred_element_type=F32)
                s = s - slopes_ref[hq] * distf
                s = jnp.where(vmask, s, NEG_BIG)
                sink = sinks_ref[hq]
                m = jnp.maximum(jnp.max(s, axis=1, keepdims=True), sink)
                e = jnp.exp(s - m)
                denom = jnp.sum(e, axis=1, keepdims=True) + jnp.exp(sink - m)
                p = (e / denom).astype(BF16)
                o = jnp.dot(p, vv, preferred_element_type=F32)
                o_ref[j * WINDOW:(j + 1) * WINDOW,
                      hq * HEAD_DIM:(hq + 1) * HEAD_DIM] = o.astype(o_ref.dtype)


def _swa_attention(proj, slopes, sinks, *, batch, seq, tq):
    n = proj.shape[0]
    nq = seq // tq
    sub = tq // WINDOW
    nwin = seq // WINDOW
    kern = functools.partial(_swa_kernel, tq=tq)
    cq = _COL_SQ // (SWA_HEADS * HEAD_DIM)
    ck, cv = _COL_SK // LANES, _COL_SV // LANES
    prev = lambda c: (lambda b, i: (b * nwin + jnp.maximum(i * sub - 1, 0), c))
    cur = lambda c: (lambda b, i: (b * nq + i, c))
    smem = pl.BlockSpec(memory_space=pltpu.SMEM)
    return pl.pallas_call(
        kern,
        out_shape=jax.ShapeDtypeStruct((n, SWA_HEADS * HEAD_DIM), BF16),
        grid=(batch, nq),
        in_specs=[smem, smem,
                  pl.BlockSpec((tq, SWA_HEADS * HEAD_DIM), cur(cq)),
                  pl.BlockSpec((WINDOW, LANES), prev(ck)),
                  pl.BlockSpec((tq, LANES), cur(ck)),
                  pl.BlockSpec((WINDOW, LANES), prev(cv)),
                  pl.BlockSpec((tq, LANES), cur(cv))],
        out_specs=pl.BlockSpec((tq, SWA_HEADS * HEAD_DIM), lambda b, i: (b * nq + i, 0)),
        compiler_params=_params(("parallel", "arbitrary")),
        name="swa_attn",
    )(slopes, sinks, proj, proj, proj, proj, proj)


def _layer_norm(y, g, b):
    mu = jnp.mean(y, axis=1, keepdims=True)
    var = jnp.mean(jnp.square(y - mu), axis=1, keepdims=True)
    return (y - mu) * lax.rsqrt(var + LN_EPS) * g + b


def _merge_kernel(x_ref, od_ref, os_ref, ga_ref, gb_ref, p_ref,
                  wa_ref, wb_ref, wo_ref, bo_ref, g1_ref, b1_ref, wpg_ref, wpp_ref,
                  x1_ref, r_ref, *, dn_alpha):
    a = jnp.dot(od_ref[...], wa_ref[...], preferred_element_type=F32)
    b = jnp.dot(os_ref[...], wb_ref[...], preferred_element_type=F32)
    merged = (jax.nn.sigmoid(ga_ref[...].astype(F32)) * a
              + jax.nn.sigmoid(gb_ref[...].astype(F32)) * b)
    mix = jnp.dot(merged.astype(BF16), wo_ref[...], preferred_element_type=F32) + bo_ref[...]
    x1 = _layer_norm(dn_alpha * x_ref[...] + mix, g1_ref[...], b1_ref[...])
    x1_ref[...] = x1
    gate = jax.nn.sigmoid(jnp.dot(x1.astype(BF16), wpg_ref[...], preferred_element_type=F32))
    ple = gate * jnp.dot(p_ref[...].astype(BF16), wpp_ref[...], preferred_element_type=F32)
    r_ref[...] = dn_alpha * x1 + ple


def _merge(x, od, osw, proj, p, wa, wb, wo, bo, g1, b1, wpg, wpp, *, dn_alpha, tm):
    n, d = x.shape
    pd = p.shape[1]
    row = lambda c: (lambda i: (i, c))
    full = lambda shape: pl.BlockSpec(shape, lambda i: (0, 0))
    kern = functools.partial(_merge_kernel, dn_alpha=dn_alpha)
    return pl.pallas_call(
        kern,
        out_shape=(jax.ShapeDtypeStruct((n, d), F32), jax.ShapeDtypeStruct((n, d), F32)),
        grid=(n // tm,),
        in_specs=[pl.BlockSpec((tm, d), row(0)),
                  pl.BlockSpec((tm, d), row(0)),
                  pl.BlockSpec((tm, d), row(0)),
                  pl.BlockSpec((tm, d), row(_COL_GA // d)),
                  pl.BlockSpec((tm, d), row(_COL_GB // d)),
                  pl.BlockSpec((tm, pd), row(0)),
                  full((d, d)), full((d, d)), full((d, d)), full((1, d)),
                  full((1, d)), full((1, d)), full((d, d)), full((pd, d))],
        out_specs=(pl.BlockSpec((tm, d), row(0)), pl.BlockSpec((tm, d), row(0))),
        compiler_params=_params(("parallel",)),
        name="merge_ln1",
    )(x, od, osw, proj, proj, p, wa, wb, wo, bo, g1, b1, wpg, wpp)


def _router_kernel(x_ref, w_ref, b_ref, code_ref, gate_ref, cnt_ref, carry, *, tm):
    @pl.when(pl.program_id(0) == 0)
    def _():
        carry[...] = jnp.zeros(carry.shape, F32)

    logits = jnp.dot(x_ref[...], w_ref[...], preferred_element_type=F32,
                     precision=lax.Precision.HIGHEST) + b_ref[...]
    lane = lax.broadcasted_iota(jnp.int32, logits.shape, 1)
    lanef = lane.astype(F32)
    work = logits
    tops, idxs = [], []
    onehot = jnp.zeros(logits.shape, F32)
    for _ in range(TOP_K):
        m = jnp.max(work, axis=1, keepdims=True)
        idx = jnp.min(jnp.where(work == m, lanef, float(LANES)), axis=1, keepdims=True)
        sel = lanef == idx
        onehot = jnp.where(sel, 1.0, onehot)
        work = jnp.where(sel, -jnp.inf, work)
        tops.append(m)
        idxs.append(idx)
    es = [jnp.exp(t - tops[0]) for t in tops]
    denom = es[0] + es[1] + es[2] + es[3]
    r = lax.broadcasted_iota(jnp.int32, (tm, tm), 0)
    c = lax.broadcasted_iota(jnp.int32, (tm, tm), 1)
    tri = jnp.where(c < r, 1.0, 0.0).astype(BF16)
    before = jnp.dot(tri, onehot.astype(BF16), preferred_element_type=F32) + carry[0:1, :]
    code = jnp.zeros(logits.shape, jnp.int32)
    gate = jnp.zeros(logits.shape, F32)
    for k in range(TOP_K):
        rank = jnp.sum(jnp.where(lanef == idxs[k], before, 0.0), axis=1, keepdims=True)
        ck = (idxs[k] * float(1 << RANK_BITS) + rank).astype(jnp.int32)
        code = jnp.where(lane == k, ck, code)
        gate = jnp.where(lane == k, es[k] / denom, gate)
    code_ref[...] = code
    gate_ref[...] = gate
    carry[0:1, :] = carry[0:1, :] + jnp.sum(onehot, axis=0, keepdims=True)
    cnt_ref[...] = carry[...]


def _router(x1, w, b, *, tm):
    n, d = x1.shape
    kern = functools.partial(_router_kernel, tm=tm)
    return pl.pallas_call(
        kern,
        out_shape=(jax.ShapeDtypeStruct((n, LANES), jnp.int32),
                   jax.ShapeDtypeStruct((n, LANES), F32),
                   jax.ShapeDtypeStruct((8, LANES), F32)),
        grid=(n // tm,),
        in_specs=[pl.BlockSpec((tm, d), lambda i: (i, 0)),
                  pl.BlockSpec((d, LANES), lambda i: (0, 0)),
                  pl.BlockSpec((1, LANES), lambda i: (0, 0))],
        out_specs=(pl.BlockSpec((tm, LANES), lambda i: (i, 0)),
                   pl.BlockSpec((tm, LANES), lambda i: (i, 0)),
                   pl.BlockSpec((8, LANES), lambda i: (0, 0))),
        scratch_shapes=[pltpu.VMEM((8, LANES), F32)],
        compiler_params=_params(("arbitrary",)),
        name="router",
    )(x1, w, b)


def _slot(code, off_ref):
    return off_ref[code >> RANK_BITS] + (code & ((1 << RANK_BITS) - 1))


def _dispatch_kernel(off_ref, codes_ref, x_ref, xs_in_ref, xs_ref, codes_smem, csem, sem, *, rows):
    del xs_in_ref
    i = pl.program_id(0)
    n_assign = rows * TOP_K
    load = pltpu.make_async_copy(codes_ref.at[pl.ds(i * n_assign, n_assign)], codes_smem, csem)
    load.start()
    load.wait()

    def row_copy(a):
        tok = i * rows + a // TOP_K
        return pltpu.make_async_copy(x_ref.at[pl.ds(tok, 1)],
                                     xs_ref.at[pl.ds(_slot(codes_smem[a], off_ref), 1)], sem)

    def issue(a, carry):
        row_copy(a).start()
        return carry

    def drain(a, carry):
        row_copy(a).wait()
        return carry

    lax.fori_loop(0, n_assign, issue, 0)
    lax.fori_loop(0, n_assign, drain, 0)


def _dispatch(off, codes, x1, xs_zero, *, rows):
    n = x1.shape[0]
    kern = functools.partial(_dispatch_kernel, rows=rows)
    any_spec = pl.BlockSpec(memory_space=pl.ANY)
    return pl.pallas_call(
        kern,
        out_shape=jax.ShapeDtypeStruct(xs_zero.shape, xs_zero.dtype),
        grid_spec=pltpu.PrefetchScalarGridSpec(
            num_scalar_prefetch=1,
            grid=(n // rows,),
            in_specs=[any_spec, any_spec, any_spec],
            out_specs=any_spec,
            scratch_shapes=[pltpu.SMEM((rows * TOP_K,), jnp.int32),
                            pltpu.SemaphoreType.DMA, pltpu.SemaphoreType.DMA]),
        input_output_aliases={3: 0},
        compiler_params=_params(("arbitrary",)),
        name="moe_dispatch",
    )(off, codes, x1, xs_zero)


def _expert_kernel(be_ref, xs_ref, wgu_ref, bgu_ref, wd_ref, bd_ref, y_ref):
    del be_ref
    dff = wd_ref.shape[1]
    gu = jnp.dot(xs_ref[...].astype(BF16), wgu_ref[0], preferred_element_type=F32) + bgu_ref[0]
    gate = jnp.minimum(gu[:, :dff], SWIGLU_LIMIT)
    up = jnp.clip(gu[:, dff:], -SWIGLU_LIMIT, SWIGLU_LIMIT)
    act = (up + 1.0) * (gate * jax.nn.sigmoid(SWIGLU_ALPHA * gate))
    y_ref[...] = jnp.dot(act.astype(BF16), wd_ref[0], preferred_element_type=F32) + bd_ref[0]


def _experts(block_e, xs, wgu, bgu, wd, bd):
    n_slots, d = xs.shape
    dff = wd.shape[1]
    nb = n_slots // MOE_BLOCK
    return pl.pallas_call(
        _expert_kernel,
        out_shape=jax.ShapeDtypeStruct((n_slots, d), F32),
        grid_spec=pltpu.PrefetchScalarGridSpec(
            num_scalar_prefetch=1,
            grid=(nb,),
            in_specs=[pl.BlockSpec((MOE_BLOCK, d), lambda j, be: (j, 0)),
                      pl.BlockSpec((1, d, 2 * dff), lambda j, be: (be[j], 0, 0)),
                      pl.BlockSpec((1, 1, 2 * dff), lambda j, be: (be[j], 0, 0)),
                      pl.BlockSpec((1, dff, d), lambda j, be: (be[j], 0, 0)),
                      pl.BlockSpec((1, 1, d), lambda j, be: (be[j], 0, 0))],
            out_specs=pl.BlockSpec((MOE_BLOCK, d), lambda j, be: (j, 0))),
        compiler_params=_params(("arbitrary",)),
        name="moe_experts",
    )(block_e, xs, wgu, bgu, wd, bd)


def _combine_kernel(off_ref, codes_ref, gate_ref, r_ref, y_ref, g2_ref, b2_ref,
                    x2_ref, xb_ref, codes_smem, ybuf, csem, sem, *, rows):
    i = pl.program_id(0)
    n_assign = rows * TOP_K
    load = pltpu.make_async_copy(codes_ref.at[pl.ds(i * n_assign, n_assign)], codes_smem, csem)
    load.start()
    load.wait()

    def row_copy(a):
        dst = (a % TOP_K) * rows + a // TOP_K
        return pltpu.make_async_copy(y_ref.at[pl.ds(_slot(codes_smem[a], off_ref), 1)],
                                     ybuf.at[pl.ds(dst, 1)], sem)

    def issue(a, carry):
        row_copy(a).start()
        return carry

    def drain(a, carry):
        row_copy(a).wait()
        return carry

    lax.fori_loop(0, n_assign, issue, 0)
    lax.fori_loop(0, n_assign, drain, 0)

    gate = gate_ref[...]
    acc = r_ref[...]
    for k in range(TOP_K):
        acc = acc + gate[:, k:k + 1] * ybuf[k * rows:(k + 1) * rows, :]
    x2 = _layer_norm(acc, g2_ref[...], b2_ref[...])
    x2_ref[...] = x2
    xb_ref[...] = x2.astype(BF16)


def _combine(off, codes, gates, r, y, g2, b2, *, rows):
    n, d = r.shape
    kern = functools.partial(_combine_kernel, rows=rows)
    any_spec = pl.BlockSpec(memory_space=pl.ANY)
    return pl.pallas_call(
        kern,
        out_shape=(jax.ShapeDtypeStruct((n, d), F32), jax.ShapeDtypeStruct((n, d), BF16)),
        grid_spec=pltpu.PrefetchScalarGridSpec(
            num_scalar_prefetch=1,
            grid=(n // rows,),
            in_specs=[any_spec,
                      pl.BlockSpec((rows, LANES), lambda i, off: (i, 0)),
                      pl.BlockSpec((rows, d), lambda i, off: (i, 0)),
                      any_spec,
                      pl.BlockSpec((1, d), lambda i, off: (0, 0)),
                      pl.BlockSpec((1, d), lambda i, off: (0, 0))],
            out_specs=(pl.BlockSpec((rows, d), lambda i, off: (i, 0)),
                       pl.BlockSpec((rows, d), lambda i, off: (i, 0))),
            scratch_shapes=[pltpu.SMEM((rows * TOP_K,), jnp.int32),
                            pltpu.VMEM((rows * TOP_K, d), F32),
                            pltpu.SemaphoreType.DMA, pltpu.SemaphoreType.DMA]),
        compiler_params=_params(("arbitrary",)),
        name="moe_combine_ln2",
    )(off, codes, gates, r, y, g2, b2)


def _alibi_slopes(n_heads):
    h = jnp.arange(1, n_heads + 1, dtype=F32)
    return jnp.exp2(-8.0 * h / n_heads)


def _block_size(n, target):
    t = min(n, target)
    while n % t:
        t //= 2
    return t


def kernel(x, p, w_in, b_in, lambda_q1, lambda_k1, lambda_q2, lambda_k2, subln_w, sinks,
           w_br_diff, w_br_swa, w_out, b_out, ln1_g, ln1_b, w_router, b_router,
           w_gate_up, b_gate_up, w_down, b_down, w_ple_gate, w_ple_proj, ln2_g, ln2_b):
    batch, seq, d = x.shape
    depth = w_in.shape[0]
    n = batch * seq
    dn_alpha = (2 * depth) ** 0.25
    n_assign = n * TOP_K
    n_blocks = n_assign // MOE_BLOCK + N_EXPERTS + 1
    n_slots = n_blocks * MOE_BLOCK

    n_in = w_in.shape[2]
    perm = jnp.concatenate([jnp.arange(n_in - 2 * d, n_in), jnp.arange(0, n_in - 2 * d)])
    diff_slopes = _alibi_slopes(DIFF_HEADS)
    swa_slopes = _alibi_slopes(SWA_HEADS)

    tm_lin = _block_size(n, 1024)
    tq_diff = _block_size(seq, 512)
    tq_swa = _block_size(seq, 256)
    tm_merge = _block_size(n, 512)
    tm_router = _block_size(n, 512)
    rows_moe = _block_size(n, 256)

    xf = x.reshape(n, d)
    xb = xf.astype(BF16)
    for i in range(depth):
        lam_init = 0.8 - 0.6 * math.exp(-0.3 * i)
        w_in_b = jnp.take(w_in[i], perm, axis=1).astype(BF16)
        b_in_p = jnp.take(b_in[i], perm)[None, :]
        proj = _linear(xb, w_in_b, b_in_p, tm=tm_lin, tn=1280)

        od = _diff_attention(proj, diff_slopes,
                             lambda_q1[i][None, :], lambda_k1[i][None, :],
                             lambda_q2[i][None, :], lambda_k2[i][None, :],
                             subln_w[i][None, :], batch=batch, seq=seq,
                             lam_init=lam_init, tq=tq_diff)
        osw = _swa_attention(proj, swa_slopes, sinks[i].astype(F32), batch=batch, seq=seq, tq=tq_swa)

        x1, r = _merge(xf, od, osw, proj, p[i].reshape(n, -1),
                       w_br_diff[i].astype(BF16), w_br_swa[i].astype(BF16),
                       w_out[i].astype(BF16), b_out[i][None, :],
                       ln1_g[i][None, :], ln1_b[i][None, :],
                       w_ple_gate[i].astype(BF16), w_ple_proj[i].astype(BF16),
                       dn_alpha=dn_alpha, tm=tm_merge)

        wr = jnp.zeros((d, LANES), F32).at[:, :N_EXPERTS].set(w_router[i])
        br = jnp.full((1, LANES), NEG_BIG, F32).at[0, :N_EXPERTS].set(b_router[i])
        code, gates, cnt = _router(x1, wr, br, tm=tm_router)

        counts = cnt[0, :N_EXPERTS].astype(jnp.int32)
        padded = (counts + MOE_BLOCK - 1) // MOE_BLOCK * MOE_BLOCK
        padded_end = jnp.cumsum(padded)
        off = (padded_end - padded).astype(jnp.int32)
        block_e = jnp.minimum(
            jnp.searchsorted(padded_end, jnp.arange(n_blocks) * MOE_BLOCK, side='right'),
            N_EXPERTS - 1).astype(jnp.int32)
        codes = code[:, :TOP_K].reshape(n_assign)

        xs = _dispatch(off, codes, x1, jnp.zeros((n_slots, d), F32), rows=rows_moe)
        y = _experts(block_e, xs, w_gate_up[i].astype(BF16), b_gate_up[i][:, None, :],
                     w_down[i].astype(BF16), b_down[i][:, None, :])
        xf, xb = _combine(off, codes, gates, r, y, ln2_g[i][None, :], ln2_b[i][None, :],
                          rows=rows_moe)
    return xf.reshape(batch, seq, d)
```

```python
import functools
import math

import jax
import jax.numpy as jnp
from jax import lax
from jax.experimental import pallas as pl
from jax.experimental.pallas import tpu as pltpu

F32 = jnp.float32
BF16 = jnp.bfloat16

HEAD_DIM = 64
DIFF_HEADS = 8
SWA_HEADS = 16
SWA_KV_HEADS = 2
SWA_GROUP = SWA_HEADS // SWA_KV_HEADS
WINDOW = 128
N_EXPERTS = 32
TOP_K = 4
MOE_BLOCK = 512
SWIGLU_LIMIT = 7.0
SWIGLU_ALPHA = 1.702
LN_EPS = 1e-5
RMS_EPS = 1e-5
NEG_BIG = -1e30

LANES = 128
VMEM_LIMIT = 56 * 1024 * 1024
RANK_BITS = 16

_COL_GA, _COL_GB, _COL_DQ, _COL_DK, _COL_DV, _COL_SQ, _COL_SK, _COL_SV = (
    0, 1024, 2048, 3072, 4096, 5120, 6144, 6272)


def _params(semantics):
    return pltpu.CompilerParams(dimension_semantics=semantics,
                                vmem_limit_bytes=VMEM_LIMIT)


def _dma_params(semantics):
    return pltpu.CompilerParams(dimension_semantics=semantics,
                                vmem_limit_bytes=VMEM_LIMIT,
                                disable_bounds_checks=True)


def _linear_kernel(x_ref, w_ref, b_ref, o_ref):
    acc = jnp.dot(x_ref[...], w_ref[...], preferred_element_type=F32)
    o_ref[...] = (acc + b_ref[...]).astype(o_ref.dtype)


def _linear(x, w, b, *, tm, tn):
    n, k = x.shape
    nout = w.shape[1]
    return pl.pallas_call(
        _linear_kernel,
        out_shape=jax.ShapeDtypeStruct((n, nout), BF16),
        grid=(n // tm, nout // tn),
        in_specs=[pl.BlockSpec((tm, k), lambda i, j: (i, 0)),
                  pl.BlockSpec((k, tn), lambda i, j: (0, j)),
                  pl.BlockSpec((1, tn), lambda i, j: (0, j))],
        out_specs=pl.BlockSpec((tm, tn), lambda i, j: (i, j)),
        compiler_params=_params(("parallel", "arbitrary")),
        name="in_proj",
    )(x, w, b)


N_AUG = 6


def _key_aug(tk, first_lane):
    r = lax.broadcasted_iota(jnp.int32, (tk, LANES), 0)
    lane = lax.broadcasted_iota(jnp.int32, (tk, LANES), 1) - first_lane
    hi = ((r >> 7) << 7).astype(F32)
    lo = (r & 127).astype(F32)
    return jnp.where((lane >= 0) & (lane < 3), hi,
                     jnp.where((lane >= 3) & (lane < N_AUG), lo, 0.0))


def _diff_attn_kernel(slopes_ref, lq1_ref, lk1_ref, lq2_ref, lk2_ref, qaug_ref,
                      q_ref, k_ref, v_ref, w_ref, o_ref,
                      k1a_sc, k2a_sc, vt_sc, a1, a2, *, tq, lam_init):
    h = pl.program_id(1)
    qi = pl.program_id(2)
    slope = slopes_ref[h]
    n_chunks = k1a_sc.shape[0]

    @pl.when(qi == 0)
    def _():
        lane = lax.broadcasted_iota(jnp.int32, (tq, LANES), 1)
        aug1 = _key_aug(tq, HEAD_DIM).astype(BF16)
        aug2 = _key_aug(tq, 0).astype(BF16)

        def build(c, carry):
            rows = pl.ds(pl.multiple_of(c * tq, tq), tq)
            k = k_ref[rows, :]
            k1a_sc[c] = jnp.where(lane < HEAD_DIM, k, aug1)
            k2a_sc[c] = jnp.where(lane >= HEAD_DIM, k, aug2)
            vt_sc[c] = v_ref[rows, :].astype(F32).T.astype(BF16)
            return carry

        lax.fori_loop(0, n_chunks, build, 0)

    qt = q_ref[...].astype(F32).T
    row = lax.broadcasted_iota(jnp.int32, qt.shape, 0)
    qt1 = jnp.where(row < HEAD_DIM, qt, qaug_ref[0, 0]).astype(BF16)
    qt2 = jnp.where(row >= HEAD_DIM, qt, qaug_ref[0, 1]).astype(BF16)
    a1[...] = jnp.zeros(a1.shape, F32)
    a2[...] = jnp.zeros(a2.shape, F32)

    def block(j, carry, masked):
        c = slope * jnp.full((1, tq), (j - qi) * tq, jnp.int32).astype(F32)
        if masked:
            krow = lax.broadcasted_iota(jnp.int32, (tq, tq), 0)
            qcol = lax.broadcasted_iota(jnp.int32, (tq, tq), 1)
            keep = krow <= qcol
        new = []
        for ka_sc, qq, a_sc, (m, l) in ((k1a_sc, qt1, a1, carry[0]), (k2a_sc, qt2, a2, carry[1])):
            s = jnp.dot(ka_sc[j], qq, preferred_element_type=F32)
            if masked:
                s = jnp.where(keep, s, NEG_BIG)
            m_new = jnp.maximum(m, jnp.max(s, axis=0, keepdims=True) + c)
            alpha = jnp.exp2(m - m_new)
            p = jnp.exp2(s - (m_new - c))
            l_new = alpha * l + jnp.sum(p, axis=0, keepdims=True)
            a_sc[...] = alpha * a_sc[...] + jnp.dot(vt_sc[j], p.astype(BF16),
                                                    preferred_element_type=F32)
            new.append((m_new, l_new))
        return tuple(new)

    init = ((jnp.full((1, tq), NEG_BIG, F32), jnp.zeros((1, tq), F32)),) * 2
    carry = lax.fori_loop(0, qi, lambda j, cr: block(j, cr, False), init)
    (_, l1), (_, l2) = block(qi, carry, True)

    lam = (jnp.exp(jnp.sum(lq1_ref[...] * lk1_ref[...], axis=1, keepdims=True))
           - jnp.exp(jnp.sum(lq2_ref[...] * lk2_ref[...], axis=1, keepdims=True))
           + lam_init)
    o = (a1[...] / l1 - lam * (a2[...] / l2)).T
    y = o * lax.rsqrt(jnp.mean(jnp.square(o), axis=1, keepdims=True) + RMS_EPS)
    y = (y * w_ref[...]) * (1.0 - lam_init)
    o_ref[...] = y.astype(o_ref.dtype)


def _diff_attention(proj, slopes_l2, qaug, lq1, lk1, lq2, lk2, subln_w, *, batch, seq, lam_init, tq):
    n = proj.shape[0]
    nq = seq // tq
    kern = functools.partial(_diff_attn_kernel, tq=tq, lam_init=lam_init)
    vec = pl.BlockSpec((1, HEAD_DIM), lambda b, h, i: (0, 0))
    cq, ck, cv = _COL_DQ // LANES, _COL_DK // LANES, _COL_DV // LANES
    return pl.pallas_call(
        kern,
        out_shape=jax.ShapeDtypeStruct((n, DIFF_HEADS * 2 * HEAD_DIM), BF16),
        grid=(batch, DIFF_HEADS, nq),
        in_specs=[pl.BlockSpec(memory_space=pltpu.SMEM),
                  vec, vec, vec, vec,
                  pl.BlockSpec((1, 2, LANES, tq), lambda b, h, i: (h, 0, 0, 0)),
                  pl.BlockSpec((tq, LANES), lambda b, h, i: (b * nq + i, cq + h)),
                  pl.BlockSpec((seq, LANES), lambda b, h, i: (b, ck + h)),
                  pl.BlockSpec((seq, LANES), lambda b, h, i: (b, cv + h)),
                  pl.BlockSpec((1, 2 * HEAD_DIM), lambda b, h, i: (0, 0))],
        out_specs=pl.BlockSpec((tq, LANES), lambda b, h, i: (b * nq + i, h)),
        scratch_shapes=[pltpu.VMEM((nq, tq, LANES), BF16), pltpu.VMEM((nq, tq, LANES), BF16),
                        pltpu.VMEM((nq, LANES, tq), BF16),
                        pltpu.VMEM((2 * HEAD_DIM, tq), F32), pltpu.VMEM((2 * HEAD_DIM, tq), F32)],
        compiler_params=_params(("arbitrary", "arbitrary", "arbitrary")),
        name="diff_attn",
    )(slopes_l2, lq1, lk1, lq2, lk2, qaug, proj, proj, proj, subln_w)


def _diff_query_aug(slopes_l2, tq):
    s0 = slopes_l2.astype(BF16).astype(F32)
    s1 = (slopes_l2 - s0).astype(BF16).astype(F32)
    s2 = (slopes_l2 - s0 - s1).astype(BF16).astype(F32)
    parts = jnp.stack([s0, s1, s2, s0, s1, s2], axis=1)
    n_heads = slopes_l2.shape[0]
    cols = jnp.zeros((n_heads, 2, LANES), F32)
    cols = cols.at[:, 0, HEAD_DIM:HEAD_DIM + N_AUG].set(parts)
    cols = cols.at[:, 1, 0:N_AUG].set(parts)
    return jnp.broadcast_to(cols[:, :, :, None], (n_heads, 2, LANES, tq))


def _swa_kernel(slopes_ref, sinks_ref, q_ref, kp_ref, kc_ref, vp_ref, vc_ref, o_ref, *, tq):
    qi = pl.program_id(1)
    scale = HEAD_DIM ** -0.5
    kcat = jnp.concatenate([kp_ref[...], kc_ref[...]], axis=0)
    vcat = jnp.concatenate([vp_ref[...], vc_ref[...]], axis=0)
    row = lax.broadcasted_iota(jnp.int32, (WINDOW, 2 * WINDOW), 0)
    col = lax.broadcasted_iota(jnp.int32, (WINDOW, 2 * WINDOW), 1)
    dist = row + WINDOW - col
    valid = (dist >= 0) & (dist < WINDOW)
    distf = dist.astype(F32)
    valid_first = valid & ((col >= WINDOW) | (qi > 0))
    for j in range(tq // WINDOW):
        kj = kcat[j * WINDOW:(j + 2) * WINDOW]
        vj = vcat[j * WINDOW:(j + 2) * WINDOW]
        qj = q_ref[j * WINDOW:(j + 1) * WINDOW, :] * scale
        vmask = valid_first if j == 0 else valid
        for hk in range(SWA_KV_HEADS):
            kk = kj[:, hk * HEAD_DIM:(hk + 1) * HEAD_DIM]
            vv = vj[:, hk * HEAD_DIM:(hk + 1) * HEAD_DIM]
            for g in range(SWA_GROUP):
                hq = hk * SWA_GROUP + g
                qh = qj[:, hq * HEAD_DIM:(hq + 1) * HEAD_DIM]
                s = lax.dot_general(qh, kk, (((1,), (1,)), ((), ())),
                                    preferred_element_type=F32)
                s = s - slopes_ref[hq] * distf
                s = jnp.where(vmask, s, NEG_BIG)
                sink = sinks_ref[hq]
                m = jnp.maximum(jnp.max(s, axis=1, keepdims=True), sink)
                e = jnp.exp(s - m)
                denom = jnp.sum(e, axis=1, keepdims=True) + jnp.exp(sink - m)
                p = (e / denom).astype(BF16)
                o = jnp.dot(p, vv, preferred_element_type=F32)
                o_ref[j * WINDOW:(j + 1) * WINDOW,
                      hq * HEAD_DIM:(hq + 1) * HEAD_DIM] = o.astype(o_ref.dtype)


def _swa_attention(proj, slopes, sinks, *, batch, seq, tq):
    n = proj.shape[0]
    nq = seq // tq
    sub = tq // WINDOW
    nwin = seq // WINDOW
    kern = functools.partial(_swa_kernel, tq=tq)
    cq = _COL_SQ // (SWA_HEADS * HEAD_DIM)
    ck, cv = _COL_SK // LANES, _COL_SV // LANES
    prev = lambda c: (lambda b, i: (b * nwin + jnp.maximum(i * sub - 1, 0), c))
    cur = lambda c: (lambda b, i: (b * nq + i, c))
    smem = pl.BlockSpec(memory_space=pltpu.SMEM)
    return pl.pallas_call(
        kern,
        out_shape=jax.ShapeDtypeStruct((n, SWA_HEADS * HEAD_DIM), BF16),
        grid=(batch, nq),
        in_specs=[smem, smem,
                  pl.BlockSpec((tq, SWA_HEADS * HEAD_DIM), cur(cq)),
                  pl.BlockSpec((WINDOW, LANES), prev(ck)),
                  pl.BlockSpec((tq, LANES), cur(ck)),
                  pl.BlockSpec((WINDOW, LANES), prev(cv)),
                  pl.BlockSpec((tq, LANES), cur(cv))],
        out_specs=pl.BlockSpec((tq, SWA_HEADS * HEAD_DIM), lambda b, i: (b * nq + i, 0)),
        compiler_params=_params(("parallel", "arbitrary")),
        name="swa_attn",
    )(slopes, sinks, proj, proj, proj, proj, proj)


def _layer_norm(y, g, b):
    mu = jnp.mean(y, axis=1, keepdims=True)
    var = jnp.mean(jnp.square(y - mu), axis=1, keepdims=True)
    return (y - mu) * lax.rsqrt(var + LN_EPS) * g + b


def _merge_kernel(x_ref, od_ref, os_ref, ga_ref, gb_ref, p_ref,
                  wa_ref, wb_ref, wo_ref, bo_ref, g1_ref, b1_ref, wpg_ref, wpp_ref,
                  x1_ref, r_ref, *, dn_alpha):
    a = jnp.dot(od_ref[...], wa_ref[...], preferred_element_type=F32)
    b = jnp.dot(os_ref[...], wb_ref[...], preferred_element_type=F32)
    merged = (jax.nn.sigmoid(ga_ref[...].astype(F32)) * a
              + jax.nn.sigmoid(gb_ref[...].astype(F32)) * b)
    mix = jnp.dot(merged.astype(BF16), wo_ref[...], preferred_element_type=F32) + bo_ref[...]
    x1 = _layer_norm(dn_alpha * x_ref[...] + mix, g1_ref[...], b1_ref[...])
    x1_ref[...] = x1
    gate = jax.nn.sigmoid(jnp.dot(x1.astype(BF16), wpg_ref[...], preferred_element_type=F32))
    ple = gate * jnp.dot(p_ref[...].astype(BF16), wpp_ref[...], preferred_element_type=F32)
    r_ref[...] = dn_alpha * x1 + ple


def _merge(x, od, osw, proj, p, wa, wb, wo, bo, g1, b1, wpg, wpp, *, dn_alpha, tm):
    n, d = x.shape
    pd = p.shape[1]
    row = lambda c: (lambda i: (i, c))
    full = lambda shape: pl.BlockSpec(shape, lambda i: (0, 0))
    kern = functools.partial(_merge_kernel, dn_alpha=dn_alpha)
    return pl.pallas_call(
        kern,
        out_shape=(jax.ShapeDtypeStruct((n, d), F32), jax.ShapeDtypeStruct((n, d), F32)),
        grid=(n // tm,),
        in_specs=[pl.BlockSpec((tm, d), row(0)),
                  pl.BlockSpec((tm, d), row(0)),
                  pl.BlockSpec((tm, d), row(0)),
                  pl.BlockSpec((tm, d), row(_COL_GA // d)),
                  pl.BlockSpec((tm, d), row(_COL_GB // d)),
                  pl.BlockSpec((tm, pd), row(0)),
                  full((d, d)), full((d, d)), full((d, d)), full((1, d)),
                  full((1, d)), full((1, d)), full((d, d)), full((pd, d))],
        out_specs=(pl.BlockSpec((tm, d), row(0)), pl.BlockSpec((tm, d), row(0))),
        compiler_params=_params(("parallel",)),
        name="merge_ln1",
    )(x, od, osw, proj, proj, p, wa, wb, wo, bo, g1, b1, wpg, wpp)


def _router_kernel(x_ref, w_ref, b_ref, code_ref, gate_ref, cnt_ref, carry, *, tm):
    @pl.when(pl.program_id(0) == 0)
    def _():
        carry[...] = jnp.zeros(carry.shape, F32)

    logits = jnp.dot(x_ref[...], w_ref[...], preferred_element_type=F32,
                     precision=lax.Precision.HIGHEST) + b_ref[...]
    lane = lax.broadcasted_iota(jnp.int32, logits.shape, 1)
    lanef = lane.astype(F32)
    work = logits
    tops, idxs = [], []
    onehot = jnp.zeros(logits.shape, F32)
    for _ in range(TOP_K):
        m = jnp.max(work, axis=1, keepdims=True)
        idx = jnp.min(jnp.where(work == m, lanef, float(LANES)), axis=1, keepdims=True)
        sel = lanef == idx
        onehot = jnp.where(sel, 1.0, onehot)
        work = jnp.where(sel, -jnp.inf, work)
        tops.append(m)
        idxs.append(idx)
    es = [jnp.exp(t - tops[0]) for t in tops]
    denom = es[0] + es[1] + es[2] + es[3]
    r = lax.broadcasted_iota(jnp.int32, (tm, tm), 0)
    c = lax.broadcasted_iota(jnp.int32, (tm, tm), 1)
    tri = jnp.where(c < r, 1.0, 0.0).astype(BF16)
    before = jnp.dot(tri, onehot.astype(BF16), preferred_element_type=F32) + carry[0:1, :]
    code = jnp.zeros(logits.shape, jnp.int32)
    gate = jnp.zeros(logits.shape, F32)
    for k in range(TOP_K):
        rank = jnp.sum(jnp.where(lanef == idxs[k], before, 0.0), axis=1, keepdims=True)
        ck = (idxs[k] * float(1 << RANK_BITS) + rank).astype(jnp.int32)
        code = jnp.where(lane == k, ck, code)
        gate = jnp.where(lane == k, es[k] / denom, gate)
    code_ref[...] = code
    gate_ref[...] = gate
    carry[0:1, :] = carry[0:1, :] + jnp.sum(onehot, axis=0, keepdims=True)
    cnt_ref[...] = carry[...]


def _router(x1, w, b, *, tm):
    n, d = x1.shape
    kern = functools.partial(_router_kernel, tm=tm)
    return pl.pallas_call(
        kern,
        out_shape=(jax.ShapeDtypeStruct((n, LANES), jnp.int32),
                   jax.ShapeDtypeStruct((n, LANES), F32),
                   jax.ShapeDtypeStruct((8, LANES), F32)),
        grid=(n // tm,),
        in_specs=[pl.BlockSpec((tm, d), lambda i: (i, 0)),
                  pl.BlockSpec((d, LANES), lambda i: (0, 0)),
                  pl.BlockSpec((1, LANES), lambda i: (0, 0))],
        out_specs=(pl.BlockSpec((tm, LANES), lambda i: (i, 0)),
                   pl.BlockSpec((tm, LANES), lambda i: (i, 0)),
                   pl.BlockSpec((8, LANES), lambda i: (0, 0))),
        scratch_shapes=[pltpu.VMEM((8, LANES), F32)],
        compiler_params=_params(("arbitrary",)),
        name="router",
    )(x1, w, b)


def _slot(code, off_ref):
    return off_ref[code >> RANK_BITS] + (code & ((1 << RANK_BITS) - 1))


def _dispatch_kernel(off_ref, codes_ref, x_ref, xs_in_ref, xs_ref, codes_smem, csem, sem, *, rows):
    del xs_in_ref
    i = pl.program_id(0)
    n_assign = rows * TOP_K
    load = pltpu.make_async_copy(codes_ref.at[pl.ds(i * n_assign, n_assign)], codes_smem, csem)
    load.start()
    load.wait()

    def issue(a, carry):
        pltpu.make_async_copy(x_ref.at[pl.ds(a // TOP_K, 1)],
                              xs_ref.at[pl.ds(_slot(codes_smem[a], off_ref), 1)], sem).start()
        return carry

    lax.fori_loop(0, n_assign, issue, 0, unroll=8)
    pltpu.make_async_copy(xs_ref.at[pl.ds(0, n_assign)], xs_ref.at[pl.ds(0, n_assign)], sem).wait()


def _dispatch(off, codes, x1, xs_zero, *, rows):
    n, d = x1.shape
    kern = functools.partial(_dispatch_kernel, rows=rows)
    any_spec = pl.BlockSpec(memory_space=pl.ANY)
    return pl.pallas_call(
        kern,
        out_shape=jax.ShapeDtypeStruct(xs_zero.shape, xs_zero.dtype),
        grid_spec=pltpu.PrefetchScalarGridSpec(
            num_scalar_prefetch=1,
            grid=(n // rows,),
            in_specs=[any_spec, pl.BlockSpec((rows, d), lambda i, off: (i, 0)), any_spec],
            out_specs=any_spec,
            scratch_shapes=[pltpu.SMEM((rows * TOP_K,), jnp.int32),
                            pltpu.SemaphoreType.DMA, pltpu.SemaphoreType.DMA]),
        input_output_aliases={3: 0},
        compiler_params=_dma_params(("arbitrary",)),
        name="moe_dispatch",
    )(off, codes, x1, xs_zero)


def _expert_kernel(be_ref, xs_ref, wgu_ref, bgu_ref, wd_ref, bd_ref, y_ref):
    del be_ref
    dff = wd_ref.shape[1]
    gu = jnp.dot(xs_ref[...].astype(BF16), wgu_ref[0], preferred_element_type=F32) + bgu_ref[0]
    gate = jnp.minimum(gu[:, :dff], SWIGLU_LIMIT)
    up = jnp.clip(gu[:, dff:], -SWIGLU_LIMIT, SWIGLU_LIMIT)
    act = (up + 1.0) * (gate * jax.nn.sigmoid(SWIGLU_ALPHA * gate))
    y_ref[...] = jnp.dot(act.astype(BF16), wd_ref[0], preferred_element_type=F32) + bd_ref[0]


def _experts(block_e, xs, wgu, bgu, wd, bd):
    n_slots, d = xs.shape
    dff = wd.shape[1]
    nb = n_slots // MOE_BLOCK
    return pl.pallas_call(
        _expert_kernel,
        out_shape=jax.ShapeDtypeStruct((n_slots, d), F32),
        grid_spec=pltpu.PrefetchScalarGridSpec(
            num_scalar_prefetch=1,
            grid=(nb,),
            in_specs=[pl.BlockSpec((MOE_BLOCK, d), lambda j, be: (j, 0)),
                      pl.BlockSpec((1, d, 2 * dff), lambda j, be: (be[j], 0, 0)),
                      pl.BlockSpec((1, 1, 2 * dff), lambda j, be: (be[j], 0, 0)),
                      pl.BlockSpec((1, dff, d), lambda j, be: (be[j], 0, 0)),
                      pl.BlockSpec((1, 1, d), lambda j, be: (be[j], 0, 0))],
            out_specs=pl.BlockSpec((MOE_BLOCK, d), lambda j, be: (j, 0))),
        compiler_params=_params(("arbitrary",)),
        name="moe_experts",
    )(block_e, xs, wgu, bgu, wd, bd)


def _combine_kernel(off_ref, codes_ref, gate_ref, r_ref, y_ref, g2_ref, b2_ref,
                    x2_ref, xb_ref, codes_smem, ybuf, csem, sem, *, rows):
    i = pl.program_id(0)
    n_assign = rows * TOP_K
    load = pltpu.make_async_copy(codes_ref.at[pl.ds(i * n_assign, n_assign)], codes_smem, csem)
    load.start()
    load.wait()

    def issue(a, carry):
        dst = (a % TOP_K) * rows + a // TOP_K
        pltpu.make_async_copy(y_ref.at[pl.ds(_slot(codes_smem[a], off_ref), 1)],
                              ybuf.at[pl.ds(dst, 1)], sem).start()
        return carry

    lax.fori_loop(0, n_assign, issue, 0, unroll=8)
    pltpu.make_async_copy(y_ref.at[pl.ds(0, n_assign)], ybuf, sem).wait()

    gate = gate_ref[...]
    acc = r_ref[...]
    for k in range(TOP_K):
        acc = acc + gate[:, k:k + 1] * ybuf[k * rows:(k + 1) * rows, :]
    x2 = _layer_norm(acc, g2_ref[...], b2_ref[...])
    x2_ref[...] = x2
    xb_ref[...] = x2.astype(BF16)


def _combine(off, codes, gates, r, y, g2, b2, *, rows):
    n, d = r.shape
    kern = functools.partial(_combine_kernel, rows=rows)
    any_spec = pl.BlockSpec(memory_space=pl.ANY)
    return pl.pallas_call(
        kern,
        out_shape=(jax.ShapeDtypeStruct((n, d), F32), jax.ShapeDtypeStruct((n, d), BF16)),
        grid_spec=pltpu.PrefetchScalarGridSpec(
            num_scalar_prefetch=1,
            grid=(n // rows,),
            in_specs=[any_spec,
                      pl.BlockSpec((rows, LANES), lambda i, off: (i, 0)),
                      pl.BlockSpec((rows, d), lambda i, off: (i, 0)),
                      any_spec,
                      pl.BlockSpec((1, d), lambda i, off: (0, 0)),
                      pl.BlockSpec((1, d), lambda i, off: (0, 0))],
            out_specs=(pl.BlockSpec((rows, d), lambda i, off: (i, 0)),
                       pl.BlockSpec((rows, d), lambda i, off: (i, 0))),
            scratch_shapes=[pltpu.SMEM((rows * TOP_K,), jnp.int32),
                            pltpu.VMEM((rows * TOP_K, d), F32),
                            pltpu.SemaphoreType.DMA, pltpu.SemaphoreType.DMA]),
        compiler_params=_dma_params(("arbitrary",)),
        name="moe_combine_ln2",
    )(off, codes, gates, r, y, g2, b2)


def _alibi_slopes(n_heads):
    h = jnp.arange(1, n_heads + 1, dtype=F32)
    return jnp.exp2(-8.0 * h / n_heads)


def _block_size(n, target):
    t = min(n, target)
    while n % t:
        t //= 2
    return t


def kernel(x, p, w_in, b_in, lambda_q1, lambda_k1, lambda_q2, lambda_k2, subln_w, sinks,
           w_br_diff, w_br_swa, w_out, b_out, ln1_g, ln1_b, w_router, b_router,
           w_gate_up, b_gate_up, w_down, b_down, w_ple_gate, w_ple_proj, ln2_g, ln2_b):
    batch, seq, d = x.shape
    depth = w_in.shape[0]
    n = batch * seq
    dn_alpha = (2 * depth) ** 0.25
    n_assign = n * TOP_K
    n_blocks = n_assign // MOE_BLOCK + N_EXPERTS + 1
    n_slots = n_blocks * MOE_BLOCK

    n_in = w_in.shape[2]
    perm = jnp.concatenate([jnp.arange(n_in - 2 * d, n_in), jnp.arange(0, n_in - 2 * d)])
    log2e = math.log2(math.e)
    diff_slopes = _alibi_slopes(DIFF_HEADS) * log2e
    swa_slopes = _alibi_slopes(SWA_HEADS)

    tm_lin = _block_size(n, 1024)
    tq_diff = _block_size(seq, 512)
    tq_swa = _block_size(seq, 256)
    tm_merge = _block_size(n, 512)
    tm_router = _block_size(n, 512)
    rows_moe = _block_size(n, 256)

    xf = x.reshape(n, d)
    xb = xf.astype(BF16)
    for i in range(depth):
        lam_init = 0.8 - 0.6 * math.exp(-0.3 * i)
        col_scale = jnp.ones((n_in,), F32).at[_COL_DQ:_COL_DK].set(HEAD_DIM ** -0.5 * log2e)
        w_in_b = (jnp.take(w_in[i], perm, axis=1) * col_scale).astype(BF16)
        b_in_p = (jnp.take(b_in[i], perm) * col_scale)[None, :]
        proj = _linear(xb, w_in_b, b_in_p, tm=tm_lin, tn=1280)

        od = _diff_attention(proj, diff_slopes, _diff_query_aug(diff_slopes, tq_diff),
                             lambda_q1[i][None, :], lambda_k1[i][None, :],
                             lambda_q2[i][None, :], lambda_k2[i][None, :],
                             subln_w[i][None, :], batch=batch, seq=seq,
                             lam_init=lam_init, tq=tq_diff)
        osw = _swa_attention(proj, swa_slopes, sinks[i].astype(F32), batch=batch, seq=seq, tq=tq_swa)

        x1, r = _merge(xf, od, osw, proj, p[i].reshape(n, -1),
                       w_br_diff[i].astype(BF16), w_br_swa[i].astype(BF16),
                       w_out[i].astype(BF16), b_out[i][None, :],
                       ln1_g[i][None, :], ln1_b[i][None, :],
                       w_ple_gate[i].astype(BF16), w_ple_proj[i].astype(BF16),
                       dn_alpha=dn_alpha, tm=tm_merge)

        wr = jnp.zeros((d, LANES), F32).at[:, :N_EXPERTS].set(w_router[i])
        br = jnp.full((1, LANES), NEG_BIG, F32).at[0, :N_EXPERTS].set(b_router[i])
        code, gates, cnt = _router(x1, wr, br, tm=tm_router)

        counts = cnt[0, :N_EXPERTS].astype(jnp.int32)
        padded = (counts + MOE_BLOCK - 1) // MOE_BLOCK * MOE_BLOCK
        padded_end = jnp.cumsum(padded)
        off = (padded_end - padded).astype(jnp.int32)
        block_e = jnp.minimum(
            jnp.searchsorted(padded_end, jnp.arange(n_blocks) * MOE_BLOCK, side='right'),
            N_EXPERTS - 1).astype(jnp.int32)
        codes = code[:, :TOP_K].reshape(n_assign)

        xs = _dispatch(off, codes, x1, jnp.zeros((n_slots, d), F32), rows=rows_moe)
        y = _experts(block_e, xs, w_gate_up[i].astype(BF16), b_gate_up[i][:, None, :],
                     w_down[i].astype(BF16), b_down[i][:, None, :])
        xf, xb = _combine(off, codes, gates, r, y, ln2_g[i][None, :], ln2_b[i][None, :],
                          rows=rows_moe)
    return xf.reshape(batch, seq, d)
```

```python
import functools
import math

import jax
import jax.numpy as jnp
from jax import lax
from jax.experimental import pallas as pl
from jax.experimental.pallas import tpu as pltpu

F32 = jnp.float32
BF16 = jnp.bfloat16

HEAD_DIM = 64
DIFF_HEADS = 8
SWA_HEADS = 16
SWA_KV_HEADS = 2
SWA_GROUP = SWA_HEADS // SWA_KV_HEADS
WINDOW = 128
N_EXPERTS = 32
TOP_K = 4
MOE_BLOCK = 512
SWIGLU_LIMIT = 7.0
SWIGLU_ALPHA = 1.702
LN_EPS = 1e-5
RMS_EPS = 1e-5
NEG_BIG = -1e30

LANES = 128
VMEM_LIMIT = 56 * 1024 * 1024
RANK_BITS = 16

_COL_GA, _COL_GB, _COL_DQ, _COL_DK, _COL_DV, _COL_SQ, _COL_SK, _COL_SV = (
    0, 1024, 2048, 3072, 4096, 5120, 6144, 6272)


def _params(semantics):
    return pltpu.CompilerParams(dimension_semantics=semantics,
                                vmem_limit_bytes=VMEM_LIMIT)


def _dma_params(semantics):
    return pltpu.CompilerParams(dimension_semantics=semantics,
                                vmem_limit_bytes=VMEM_LIMIT,
                                disable_bounds_checks=True)


def _linear_kernel(x_ref, w_ref, b_ref, o_ref):
    acc = jnp.dot(x_ref[...], w_ref[...], preferred_element_type=F32)
    o_ref[...] = (acc + b_ref[...]).astype(o_ref.dtype)


def _linear(x, w, b, *, tm, tn):
    n, k = x.shape
    nout = w.shape[1]
    return pl.pallas_call(
        _linear_kernel,
        out_shape=jax.ShapeDtypeStruct((n, nout), BF16),
        grid=(n // tm, nout // tn),
        in_specs=[pl.BlockSpec((tm, k), lambda i, j: (i, 0)),
                  pl.BlockSpec((k, tn), lambda i, j: (0, j)),
                  pl.BlockSpec((1, tn), lambda i, j: (0, j))],
        out_specs=pl.BlockSpec((tm, tn), lambda i, j: (i, j)),
        compiler_params=_params(("parallel", "arbitrary")),
        name="in_proj",
    )(x, w, b)


N_AUG = 6


def _key_aug(tk, first_lane):
    r = lax.broadcasted_iota(jnp.int32, (tk, LANES), 0)
    lane = lax.broadcasted_iota(jnp.int32, (tk, LANES), 1) - first_lane
    hi = ((r >> 7) << 7).astype(F32)
    lo = (r & 127).astype(F32)
    return jnp.where((lane >= 0) & (lane < 3), hi,
                     jnp.where((lane >= 3) & (lane < N_AUG), lo, 0.0))


def _diff_attn_kernel(slopes_ref, lq1_ref, lk1_ref, lq2_ref, lk2_ref, qaug_ref,
                      q_ref, k_ref, v_ref, w_ref, o_ref,
                      k1a_sc, k2a_sc, vt_sc, a1, a2, st, sa1, sa2, sb1, sb2, *, tq, lam_init):
    h = pl.program_id(1)
    qi = pl.program_id(2)
    slope = slopes_ref[h]
    n_chunks = k1a_sc.shape[0]

    @pl.when(qi == 0)
    def _():
        lane = lax.broadcasted_iota(jnp.int32, (tq, LANES), 1)
        aug1 = _key_aug(tq, HEAD_DIM).astype(BF16)
        aug2 = _key_aug(tq, 0).astype(BF16)

        def build(c, carry):
            rows = pl.ds(pl.multiple_of(c * tq, tq), tq)
            k = k_ref[rows, :]
            k1a_sc[c] = jnp.where(lane < HEAD_DIM, k, aug1)
            k2a_sc[c] = jnp.where(lane >= HEAD_DIM, k, aug2)
            vt_sc[c] = v_ref[rows, :].astype(F32).T.astype(BF16)
            return carry

        lax.fori_loop(0, n_chunks, build, 0)

    qt = q_ref[...].astype(F32).T
    row = lax.broadcasted_iota(jnp.int32, qt.shape, 0)
    qt1 = jnp.where(row < HEAD_DIM, qt, qaug_ref[0, 0]).astype(BF16)
    qt2 = jnp.where(row >= HEAD_DIM, qt, qaug_ref[0, 1]).astype(BF16)
    a1[...] = jnp.zeros(a1.shape, F32)
    a2[...] = jnp.zeros(a2.shape, F32)

    M1, L1, M2, L2, XA1, XA2 = range(6)
    for r_, v_ in ((M1, NEG_BIG), (L1, 0.0), (M2, NEG_BIG), (L2, 0.0)):
        st[r_:r_ + 1, :] = jnp.full((1, tq), v_, F32)

    def scores(j, dst1, dst2, masked=False):
        if masked:
            krow = lax.broadcasted_iota(jnp.int32, (tq, tq), 0)
            qcol = lax.broadcasted_iota(jnp.int32, (tq, tq), 1)
            keep = krow <= qcol
        out = []
        for ka_sc, qq, dst in ((k1a_sc, qt1, dst1), (k2a_sc, qt2, dst2)):
            s = jnp.dot(ka_sc[j], qq, preferred_element_type=F32)
            if masked:
                s = jnp.where(keep, s, NEG_BIG)
            dst[...] = s
            out.append(jnp.max(s, axis=0, keepdims=True))
        return out

    def accumulate(j, src1, src2, mx1, mx2):
        c = slope * jnp.full((1, tq), (j - qi) * tq, jnp.int32).astype(F32)
        for src, mx, a_sc, mr, lr in ((src1, mx1, a1, M1, L1), (src2, mx2, a2, M2, L2)):
            m = st[mr:mr + 1, :]
            m_new = jnp.maximum(m, mx + c)
            alpha = jnp.exp2(m - m_new)
            p = jnp.exp2(src[...] - (m_new - c))
            st[lr:lr + 1, :] = alpha * st[lr:lr + 1, :] + jnp.sum(p, axis=0, keepdims=True)
            st[mr:mr + 1, :] = m_new
            a_sc[...] = alpha * a_sc[...] + jnp.dot(vt_sc[j], p.astype(BF16),
                                                    preferred_element_type=F32)

    xa = scores(0, sa1, sa2)
    st[XA1:XA1 + 1, :] = xa[0]
    st[XA2:XA2 + 1, :] = xa[1]

    def pair(i, carry):
        j0 = 2 * i
        xb = scores(j0 + 1, sb1, sb2)
        accumulate(j0, sa1, sa2, st[XA1:XA1 + 1, :], st[XA2:XA2 + 1, :])
        xa = scores(jnp.minimum(j0 + 2, qi - 1), sa1, sa2)
        accumulate(j0 + 1, sb1, sb2, xb[0], xb[1])
        st[XA1:XA1 + 1, :] = xa[0]
        st[XA2:XA2 + 1, :] = xa[1]
        return carry

    lax.fori_loop(0, qi // 2, pair, 0)

    @pl.when(qi % 2 == 1)
    def _():
        accumulate(qi - 1, sa1, sa2, st[XA1:XA1 + 1, :], st[XA2:XA2 + 1, :])

    xd = scores(qi, sa1, sa2, masked=True)
    accumulate(qi, sa1, sa2, xd[0], xd[1])
    l1 = st[L1:L1 + 1, :]
    l2 = st[L2:L2 + 1, :]

    lam = (jnp.exp(jnp.sum(lq1_ref[...] * lk1_ref[...], axis=1, keepdims=True))
           - jnp.exp(jnp.sum(lq2_ref[...] * lk2_ref[...], axis=1, keepdims=True))
           + lam_init)
    o = (a1[...] / l1 - lam * (a2[...] / l2)).T
    y = o * lax.rsqrt(jnp.mean(jnp.square(o), axis=1, keepdims=True) + RMS_EPS)
    y = (y * w_ref[...]) * (1.0 - lam_init)
    o_ref[...] = y.astype(o_ref.dtype)


def _diff_attention(proj, slopes_l2, qaug, lq1, lk1, lq2, lk2, subln_w, *, batch, seq, lam_init, tq):
    n = proj.shape[0]
    nq = seq // tq
    kern = functools.partial(_diff_attn_kernel, tq=tq, lam_init=lam_init)
    vec = pl.BlockSpec((1, HEAD_DIM), lambda b, h, i: (0, 0))
    cq, ck, cv = _COL_DQ // LANES, _COL_DK // LANES, _COL_DV // LANES
    return pl.pallas_call(
        kern,
        out_shape=jax.ShapeDtypeStruct((n, DIFF_HEADS * 2 * HEAD_DIM), BF16),
        grid=(batch, DIFF_HEADS, nq),
        in_specs=[pl.BlockSpec(memory_space=pltpu.SMEM),
                  vec, vec, vec, vec,
                  pl.BlockSpec((1, 2, LANES, tq), lambda b, h, i: (h, 0, 0, 0)),
                  pl.BlockSpec((tq, LANES), lambda b, h, i: (b * nq + i, cq + h)),
                  pl.BlockSpec((seq, LANES), lambda b, h, i: (b, ck + h)),
                  pl.BlockSpec((seq, LANES), lambda b, h, i: (b, cv + h)),
                  pl.BlockSpec((1, 2 * HEAD_DIM), lambda b, h, i: (0, 0))],
        out_specs=pl.BlockSpec((tq, LANES), lambda b, h, i: (b * nq + i, h)),
        scratch_shapes=[pltpu.VMEM((nq, tq, LANES), BF16), pltpu.VMEM((nq, tq, LANES), BF16),
                        pltpu.VMEM((nq, LANES, tq), BF16),
                        pltpu.VMEM((2 * HEAD_DIM, tq), F32), pltpu.VMEM((2 * HEAD_DIM, tq), F32),
                        pltpu.VMEM((8, tq), F32)] + [pltpu.VMEM((tq, tq), F32)] * 4,
        compiler_params=_params(("arbitrary", "arbitrary", "arbitrary")),
        name="diff_attn",
    )(slopes_l2, lq1, lk1, lq2, lk2, qaug, proj, proj, proj, subln_w)


def _diff_query_aug(slopes_l2, tq):
    s0 = slopes_l2.astype(BF16).astype(F32)
    s1 = (slopes_l2 - s0).astype(BF16).astype(F32)
    s2 = (slopes_l2 - s0 - s1).astype(BF16).astype(F32)
    parts = jnp.stack([s0, s1, s2, s0, s1, s2], axis=1)
    n_heads = slopes_l2.shape[0]
    cols = jnp.zeros((n_heads, 2, LANES), F32)
    cols = cols.at[:, 0, HEAD_DIM:HEAD_DIM + N_AUG].set(parts)
    cols = cols.at[:, 1, 0:N_AUG].set(parts)
    return jnp.broadcast_to(cols[:, :, :, None], (n_heads, 2, LANES, tq))


def _swa_kernel(slopes_ref, sinks_ref, q_ref, kp_ref, kc_ref, vp_ref, vc_ref, o_ref, *, tq):
    qi = pl.program_id(1)
    scale = HEAD_DIM ** -0.5
    kcat = jnp.concatenate([kp_ref[...], kc_ref[...]], axis=0)
    vcat = jnp.concatenate([vp_ref[...], vc_ref[...]], axis=0)
    row = lax.broadcasted_iota(jnp.int32, (WINDOW, 2 * WINDOW), 0)
    col = lax.broadcasted_iota(jnp.int32, (WINDOW, 2 * WINDOW), 1)
    dist = row + WINDOW - col
    valid = (dist >= 0) & (dist < WINDOW)
    distf = dist.astype(F32)
    valid_first = valid & ((col >= WINDOW) | (qi > 0))
    for j in range(tq // WINDOW):
        kj = kcat[j * WINDOW:(j + 2) * WINDOW]
        vj = vcat[j * WINDOW:(j + 2) * WINDOW]
        qj = q_ref[j * WINDOW:(j + 1) * WINDOW, :] * scale
        vmask = valid_first if j == 0 else valid
        for hk in range(SWA_KV_HEADS):
            kk = kj[:, hk * HEAD_DIM:(hk + 1) * HEAD_DIM]
            vv = vj[:, hk * HEAD_DIM:(hk + 1) * HEAD_DIM]
            for g in range(SWA_GROUP):
                hq = hk * SWA_GROUP + g
                qh = qj[:, hq * HEAD_DIM:(hq + 1) * HEAD_DIM]
                s = lax.dot_general(qh, kk, (((1,), (1,)), ((), ())),
                                    preferred_element_type=F32)
                s = s - slopes_ref[hq] * distf
                s = jnp.where(vmask, s, NEG_BIG)
                sink = sinks_ref[hq]
                m = jnp.maximum(jnp.max(s, axis=1, keepdims=True), sink)
                e = jnp.exp(s - m)
                denom = jnp.sum(e, axis=1, keepdims=True) + jnp.exp(sink - m)
                p = (e / denom).astype(BF16)
                o = jnp.dot(p, vv, preferred_element_type=F32)
                o_ref[j * WINDOW:(j + 1) * WINDOW,
                      hq * HEAD_DIM:(hq + 1) * HEAD_DIM] = o.astype(o_ref.dtype)


def _swa_attention(proj, slopes, sinks, *, batch, seq, tq):
    n = proj.shape[0]
    nq = seq // tq
    sub = tq // WINDOW
    nwin = seq // WINDOW
    kern = functools.partial(_swa_kernel, tq=tq)
    cq = _COL_SQ // (SWA_HEADS * HEAD_DIM)
    ck, cv = _COL_SK // LANES, _COL_SV // LANES
    prev = lambda c: (lambda b, i: (b * nwin + jnp.maximum(i * sub - 1, 0), c))
    cur = lambda c: (lambda b, i: (b * nq + i, c))
    smem = pl.BlockSpec(memory_space=pltpu.SMEM)
    return pl.pallas_call(
        kern,
        out_shape=jax.ShapeDtypeStruct((n, SWA_HEADS * HEAD_DIM), BF16),
        grid=(batch, nq),
        in_specs=[smem, smem,
                  pl.BlockSpec((tq, SWA_HEADS * HEAD_DIM), cur(cq)),
                  pl.BlockSpec((WINDOW, LANES), prev(ck)),
                  pl.BlockSpec((tq, LANES), cur(ck)),
                  pl.BlockSpec((WINDOW, LANES), prev(cv)),
                  pl.BlockSpec((tq, LANES), cur(cv))],
        out_specs=pl.BlockSpec((tq, SWA_HEADS * HEAD_DIM), lambda b, i: (b * nq + i, 0)),
        compiler_params=_params(("parallel", "arbitrary")),
        name="swa_attn",
    )(slopes, sinks, proj, proj, proj, proj, proj)


def _layer_norm(y, g, b):
    mu = jnp.mean(y, axis=1, keepdims=True)
    var = jnp.mean(jnp.square(y - mu), axis=1, keepdims=True)
    return (y - mu) * lax.rsqrt(var + LN_EPS) * g + b


def _merge_kernel(x_ref, od_ref, os_ref, ga_ref, gb_ref, p_ref,
                  wa_ref, wb_ref, wo_ref, bo_ref, g1_ref, b1_ref, wpg_ref, wpp_ref,
                  x1_ref, r_ref, *, dn_alpha):
    a = jnp.dot(od_ref[...], wa_ref[...], preferred_element_type=F32)
    b = jnp.dot(os_ref[...], wb_ref[...], preferred_element_type=F32)
    merged = (jax.nn.sigmoid(ga_ref[...].astype(F32)) * a
              + jax.nn.sigmoid(gb_ref[...].astype(F32)) * b)
    mix = jnp.dot(merged.astype(BF16), wo_ref[...], preferred_element_type=F32) + bo_ref[...]
    x1 = _layer_norm(dn_alpha * x_ref[...] + mix, g1_ref[...], b1_ref[...])
    x1_ref[...] = x1
    gate = jax.nn.sigmoid(jnp.dot(x1.astype(BF16), wpg_ref[...], preferred_element_type=F32))
    ple = gate * jnp.dot(p_ref[...].astype(BF16), wpp_ref[...], preferred_element_type=F32)
    r_ref[...] = dn_alpha * x1 + ple


def _merge(x, od, osw, proj, p, wa, wb, wo, bo, g1, b1, wpg, wpp, *, dn_alpha, tm):
    n, d = x.shape
    pd = p.shape[1]
    row = lambda c: (lambda i: (i, c))
    full = lambda shape: pl.BlockSpec(shape, lambda i: (0, 0))
    kern = functools.partial(_merge_kernel, dn_alpha=dn_alpha)
    return pl.pallas_call(
        kern,
        out_shape=(jax.ShapeDtypeStruct((n, d), F32), jax.ShapeDtypeStruct((n, d), F32)),
        grid=(n // tm,),
        in_specs=[pl.BlockSpec((tm, d), row(0)),
                  pl.BlockSpec((tm, d), row(0)),
                  pl.BlockSpec((tm, d), row(0)),
                  pl.BlockSpec((tm, d), row(_COL_GA // d)),
                  pl.BlockSpec((tm, d), row(_COL_GB // d)),
                  pl.BlockSpec((tm, pd), row(0)),
                  full((d, d)), full((d, d)), full((d, d)), full((1, d)),
                  full((1, d)), full((1, d)), full((d, d)), full((pd, d))],
        out_specs=(pl.BlockSpec((tm, d), row(0)), pl.BlockSpec((tm, d), row(0))),
        compiler_params=_params(("parallel",)),
        name="merge_ln1",
    )(x, od, osw, proj, proj, p, wa, wb, wo, bo, g1, b1, wpg, wpp)


def _router_kernel(x_ref, w_ref, b_ref, code_ref, gate_ref, cnt_ref, carry, *, tm):
    @pl.when(pl.program_id(0) == 0)
    def _():
        carry[...] = jnp.zeros(carry.shape, F32)

    logits = jnp.dot(x_ref[...], w_ref[...], preferred_element_type=F32,
                     precision=lax.Precision.HIGHEST) + b_ref[...]
    lane = lax.broadcasted_iota(jnp.int32, logits.shape, 1)
    lanef = lane.astype(F32)
    work = logits
    tops, idxs = [], []
    onehot = jnp.zeros(logits.shape, F32)
    for _ in range(TOP_K):
        m = jnp.max(work, axis=1, keepdims=True)
        idx = jnp.min(jnp.where(work == m, lanef, float(LANES)), axis=1, keepdims=True)
        sel = lanef == idx
        onehot = jnp.where(sel, 1.0, onehot)
        work = jnp.where(sel, -jnp.inf, work)
        tops.append(m)
        idxs.append(idx)
    es = [jnp.exp(t - tops[0]) for t in tops]
    denom = es[0] + es[1] + es[2] + es[3]
    r = lax.broadcasted_iota(jnp.int32, (tm, tm), 0)
    c = lax.broadcasted_iota(jnp.int32, (tm, tm), 1)
    tri = jnp.where(c < r, 1.0, 0.0).astype(BF16)
    before = jnp.dot(tri, onehot.astype(BF16), preferred_element_type=F32) + carry[0:1, :]
    code = jnp.zeros(logits.shape, jnp.int32)
    gate = jnp.zeros(logits.shape, F32)
    for k in range(TOP_K):
        rank = jnp.sum(jnp.where(lanef == idxs[k], before, 0.0), axis=1, keepdims=True)
        ck = (idxs[k] * float(1 << RANK_BITS) + rank).astype(jnp.int32)
        code = jnp.where(lane == k, ck, code)
        gate = jnp.where(lane == k, es[k] / denom, gate)
    code_ref[...] = code
    gate_ref[...] = gate
    carry[0:1, :] = carry[0:1, :] + jnp.sum(onehot, axis=0, keepdims=True)
    cnt_ref[...] = carry[...]


def _router(x1, w, b, *, tm):
    n, d = x1.shape
    kern = functools.partial(_router_kernel, tm=tm)
    return pl.pallas_call(
        kern,
        out_shape=(jax.ShapeDtypeStruct((n, LANES), jnp.int32),
                   jax.ShapeDtypeStruct((n, LANES), F32),
                   jax.ShapeDtypeStruct((8, LANES), F32)),
        grid=(n // tm,),
        in_specs=[pl.BlockSpec((tm, d), lambda i: (i, 0)),
                  pl.BlockSpec((d, LANES), lambda i: (0, 0)),
                  pl.BlockSpec((1, LANES), lambda i: (0, 0))],
        out_specs=(pl.BlockSpec((tm, LANES), lambda i: (i, 0)),
                   pl.BlockSpec((tm, LANES), lambda i: (i, 0)),
                   pl.BlockSpec((8, LANES), lambda i: (0, 0))),
        scratch_shapes=[pltpu.VMEM((8, LANES), F32)],
        compiler_params=_params(("arbitrary",)),
        name="router",
    )(x1, w, b)


def _slot(code, off_ref):
    return off_ref[code >> RANK_BITS] + (code & ((1 << RANK_BITS) - 1))


def _dispatch_kernel(off_ref, codes_ref, x_ref, xs_in_ref, xs_ref, codes_smem, csem, sem, *, rows):
    del xs_in_ref
    i = pl.program_id(0)
    n_assign = rows * TOP_K
    load = pltpu.make_async_copy(codes_ref.at[pl.ds(i * n_assign, n_assign)], codes_smem, csem)
    load.start()
    load.wait()

    def issue(a, carry):
        pltpu.make_async_copy(x_ref.at[pl.ds(a // TOP_K, 1)],
                              xs_ref.at[pl.ds(_slot(codes_smem[a], off_ref), 1)], sem).start()
        return carry

    lax.fori_loop(0, n_assign, issue, 0, unroll=8)
    pltpu.make_async_copy(xs_ref.at[pl.ds(0, n_assign)], xs_ref.at[pl.ds(0, n_assign)], sem).wait()


def _dispatch(off, codes, x1, xs_zero, *, rows):
    n, d = x1.shape
    kern = functools.partial(_dispatch_kernel, rows=rows)
    any_spec = pl.BlockSpec(memory_space=pl.ANY)
    return pl.pallas_call(
        kern,
        out_shape=jax.ShapeDtypeStruct(xs_zero.shape, xs_zero.dtype),
        grid_spec=pltpu.PrefetchScalarGridSpec(
            num_scalar_prefetch=1,
            grid=(n // rows,),
            in_specs=[any_spec, pl.BlockSpec((rows, d), lambda i, off: (i, 0)), any_spec],
            out_specs=any_spec,
            scratch_shapes=[pltpu.SMEM((rows * TOP_K,), jnp.int32),
                            pltpu.SemaphoreType.DMA, pltpu.SemaphoreType.DMA]),
        input_output_aliases={3: 0},
        compiler_params=_dma_params(("arbitrary",)),
        name="moe_dispatch",
    )(off, codes, x1, xs_zero)


def _expert_kernel(be_ref, xs_ref, wgu_ref, bgu_ref, wd_ref, bd_ref, y_ref):
    del be_ref
    dff = wd_ref.shape[1]
    gu = jnp.dot(xs_ref[...].astype(BF16), wgu_ref[0], preferred_element_type=F32) + bgu_ref[0]
    gate = jnp.minimum(gu[:, :dff], SWIGLU_LIMIT)
    up = jnp.clip(gu[:, dff:], -SWIGLU_LIMIT, SWIGLU_LIMIT)
    act = (up + 1.0) * (gate * jax.nn.sigmoid(SWIGLU_ALPHA * gate))
    y_ref[...] = jnp.dot(act.astype(BF16), wd_ref[0], preferred_element_type=F32) + bd_ref[0]


def _experts(block_e, xs, wgu, bgu, wd, bd):
    n_slots, d = xs.shape
    dff = wd.shape[1]
    nb = n_slots // MOE_BLOCK
    return pl.pallas_call(
        _expert_kernel,
        out_shape=jax.ShapeDtypeStruct((n_slots, d), F32),
        grid_spec=pltpu.PrefetchScalarGridSpec(
            num_scalar_prefetch=1,
            grid=(nb,),
            in_specs=[pl.BlockSpec((MOE_BLOCK, d), lambda j, be: (j, 0)),
                      pl.BlockSpec((1, d, 2 * dff), lambda j, be: (be[j], 0, 0)),
                      pl.BlockSpec((1, 1, 2 * dff), lambda j, be: (be[j], 0, 0)),
                      pl.BlockSpec((1, dff, d), lambda j, be: (be[j], 0, 0)),
                      pl.BlockSpec((1, 1, d), lambda j, be: (be[j], 0, 0))],
            out_specs=pl.BlockSpec((MOE_BLOCK, d), lambda j, be: (j, 0))),
        compiler_params=_params(("arbitrary",)),
        name="moe_experts",
    )(block_e, xs, wgu, bgu, wd, bd)


def _combine_kernel(off_ref, codes_ref, gate_ref, r_ref, y_ref, g2_ref, b2_ref,
                    x2_ref, xb_ref, codes_smem, ybuf, csem, sem, *, rows):
    i = pl.program_id(0)
    n_assign = rows * TOP_K
    load = pltpu.make_async_copy(codes_ref.at[pl.ds(i * n_assign, n_assign)], codes_smem, csem)
    load.start()
    load.wait()

    def issue(a, carry):
        dst = (a % TOP_K) * rows + a // TOP_K
        pltpu.make_async_copy(y_ref.at[pl.ds(_slot(codes_smem[a], off_ref), 1)],
                              ybuf.at[pl.ds(dst, 1)], sem).start()
        return carry

    lax.fori_loop(0, n_assign, issue, 0, unroll=8)
    pltpu.make_async_copy(y_ref.at[pl.ds(0, n_assign)], ybuf, sem).wait()

    gate = gate_ref[...]
    acc = r_ref[...]
    for k in range(TOP_K):
        acc = acc + gate[:, k:k + 1] * ybuf[k * rows:(k + 1) * rows, :]
    x2 = _layer_norm(acc, g2_ref[...], b2_ref[...])
    x2_ref[...] = x2
    xb_ref[...] = x2.astype(BF16)


def _combine(off, codes, gates, r, y, g2, b2, *, rows):
    n, d = r.shape
    kern = functools.partial(_combine_kernel, rows=rows)
    any_spec = pl.BlockSpec(memory_space=pl.ANY)
    return pl.pallas_call(
        kern,
        out_shape=(jax.ShapeDtypeStruct((n, d), F32), jax.ShapeDtypeStruct((n, d), BF16)),
        grid_spec=pltpu.PrefetchScalarGridSpec(
            num_scalar_prefetch=1,
            grid=(n // rows,),
            in_specs=[any_spec,
                      pl.BlockSpec((rows, LANES), lambda i, off: (i, 0)),
                      pl.BlockSpec((rows, d), lambda i, off: (i, 0)),
                      any_spec,
                      pl.BlockSpec((1, d), lambda i, off: (0, 0)),
                      pl.BlockSpec((1, d), lambda i, off: (0, 0))],
            out_specs=(pl.BlockSpec((rows, d), lambda i, off: (i, 0)),
                       pl.BlockSpec((rows, d), lambda i, off: (i, 0))),
            scratch_shapes=[pltpu.SMEM((rows * TOP_K,), jnp.int32),
                            pltpu.VMEM((rows * TOP_K, d), F32),
                            pltpu.SemaphoreType.DMA, pltpu.SemaphoreType.DMA]),
        compiler_params=_dma_params(("arbitrary",)),
        name="moe_combine_ln2",
    )(off, codes, gates, r, y, g2, b2)


def _alibi_slopes(n_heads):
    h = jnp.arange(1, n_heads + 1, dtype=F32)
    return jnp.exp2(-8.0 * h / n_heads)


def _block_size(n, target):
    t = min(n, target)
    while n % t:
        t //= 2
    return t


def kernel(x, p, w_in, b_in, lambda_q1, lambda_k1, lambda_q2, lambda_k2, subln_w, sinks,
           w_br_diff, w_br_swa, w_out, b_out, ln1_g, ln1_b, w_router, b_router,
           w_gate_up, b_gate_up, w_down, b_down, w_ple_gate, w_ple_proj, ln2_g, ln2_b):
    batch, seq, d = x.shape
    depth = w_in.shape[0]
    n = batch * seq
    dn_alpha = (2 * depth) ** 0.25
    n_assign = n * TOP_K
    n_blocks = n_assign // MOE_BLOCK + N_EXPERTS + 1
    n_slots = n_blocks * MOE_BLOCK

    n_in = w_in.shape[2]
    perm = jnp.concatenate([jnp.arange(n_in - 2 * d, n_in), jnp.arange(0, n_in - 2 * d)])
    log2e = math.log2(math.e)
    diff_slopes = _alibi_slopes(DIFF_HEADS) * log2e
    swa_slopes = _alibi_slopes(SWA_HEADS)

    tm_lin = _block_size(n, 1024)
    tq_diff = _block_size(seq, 512)
    tq_swa = _block_size(seq, 256)
    tm_merge = _block_size(n, 512)
    tm_router = _block_size(n, 512)
    rows_moe = _block_size(n, 256)

    xf = x.reshape(n, d)
    xb = xf.astype(BF16)
    for i in range(depth):
        lam_init = 0.8 - 0.6 * math.exp(-0.3 * i)
        col_scale = jnp.ones((n_in,), F32).at[_COL_DQ:_COL_DK].set(HEAD_DIM ** -0.5 * log2e)
        w_in_b = (jnp.take(w_in[i], perm, axis=1) * col_scale).astype(BF16)
        b_in_p = (jnp.take(b_in[i], perm) * col_scale)[None, :]
        proj = _linear(xb, w_in_b, b_in_p, tm=tm_lin, tn=1280)

        od = _diff_attention(proj, diff_slopes, _diff_query_aug(diff_slopes, tq_diff),
                             lambda_q1[i][None, :], lambda_k1[i][None, :],
                             lambda_q2[i][None, :], lambda_k2[i][None, :],
                             subln_w[i][None, :], batch=batch, seq=seq,
                             lam_init=lam_init, tq=tq_diff)
        osw = _swa_attention(proj, swa_slopes, sinks[i].astype(F32), batch=batch, seq=seq, tq=tq_swa)

        x1, r = _merge(xf, od, osw, proj, p[i].reshape(n, -1),
                       w_br_diff[i].astype(BF16), w_br_swa[i].astype(BF16),
                       w_out[i].astype(BF16), b_out[i][None, :],
                       ln1_g[i][None, :], ln1_b[i][None, :],
                       w_ple_gate[i].astype(BF16), w_ple_proj[i].astype(BF16),
                       dn_alpha=dn_alpha, tm=tm_merge)

        wr = jnp.zeros((d, LANES), F32).at[:, :N_EXPERTS].set(w_router[i])
        br = jnp.full((1, LANES), NEG_BIG, F32).at[0, :N_EXPERTS].set(b_router[i])
        code, gates, cnt = _router(x1, wr, br, tm=tm_router)

        counts = cnt[0, :N_EXPERTS].astype(jnp.int32)
        padded = (counts + MOE_BLOCK - 1) // MOE_BLOCK * MOE_BLOCK
        padded_end = jnp.cumsum(padded)
        off = (padded_end - padded).astype(jnp.int32)
        block_e = jnp.minimum(
            jnp.searchsorted(padded_end, jnp.arange(n_blocks) * MOE_BLOCK, side='right'),
            N_EXPERTS - 1).astype(jnp.int32)
        codes = code[:, :TOP_K].reshape(n_assign)

        xs = _dispatch(off, codes, x1, jnp.zeros((n_slots, d), F32), rows=rows_moe)
        y = _experts(block_e, xs, w_gate_up[i].astype(BF16), b_gate_up[i][:, None, :],
                     w_down[i].astype(BF16), b_down[i][:, None, :])
        xf, xb = _combine(off, codes, gates, r, y, ln2_g[i][None, :], ln2_b[i][None, :],
                          rows=rows_moe)
    return xf.reshape(batch, seq, d)
```

```python
import functools
import math

import jax
import jax.numpy as jnp
from jax import lax
from jax.experimental import pallas as pl
from jax.experimental.pallas import tpu as pltpu

F32 = jnp.float32
BF16 = jnp.bfloat16

HEAD_DIM = 64
DIFF_HEADS = 8
SWA_HEADS = 16
SWA_KV_HEADS = 2
SWA_GROUP = SWA_HEADS // SWA_KV_HEADS
WINDOW = 128
N_EXPERTS = 32
TOP_K = 4
MOE_BLOCK = 512
SWIGLU_LIMIT = 7.0
SWIGLU_ALPHA = 1.702
LN_EPS = 1e-5
RMS_EPS = 1e-5
NEG_BIG = -1e30

LANES = 128
VMEM_LIMIT = 56 * 1024 * 1024
RANK_BITS = 16

_COL_GA, _COL_GB, _COL_DQ, _COL_DK, _COL_DV, _COL_SQ, _COL_SK, _COL_SV = (
    0, 1024, 2048, 3072, 4096, 5120, 6144, 6272)


def _params(semantics):
    return pltpu.CompilerParams(dimension_semantics=semantics,
                                vmem_limit_bytes=VMEM_LIMIT)


def _dma_params(semantics):
    return pltpu.CompilerParams(dimension_semantics=semantics,
                                vmem_limit_bytes=VMEM_LIMIT,
                                disable_bounds_checks=True)


def _linear_kernel(x_ref, w_ref, b_ref, o_ref):
    acc = jnp.dot(x_ref[...], w_ref[...], preferred_element_type=F32)
    o_ref[...] = (acc + b_ref[...]).astype(o_ref.dtype)


def _linear(x, w, b, *, tm, tn):
    n, k = x.shape
    nout = w.shape[1]
    return pl.pallas_call(
        _linear_kernel,
        out_shape=jax.ShapeDtypeStruct((n, nout), BF16),
        grid=(n // tm, nout // tn),
        in_specs=[pl.BlockSpec((tm, k), lambda i, j: (i, 0)),
                  pl.BlockSpec((k, tn), lambda i, j: (0, j)),
                  pl.BlockSpec((1, tn), lambda i, j: (0, j))],
        out_specs=pl.BlockSpec((tm, tn), lambda i, j: (i, j)),
        compiler_params=_params(("parallel", "arbitrary")),
        name="in_proj",
    )(x, w, b)


N_AUG = 6
STRIP = 256


def _key_aug(tk, first_lane):
    r = lax.broadcasted_iota(jnp.int32, (tk, LANES), 0)
    lane = lax.broadcasted_iota(jnp.int32, (tk, LANES), 1) - first_lane
    hi = ((r >> 7) << 7).astype(F32)
    lo = (r & 127).astype(F32)
    return jnp.where((lane >= 0) & (lane < 3), hi,
                     jnp.where((lane >= 3) & (lane < N_AUG), lo, 0.0))


def _diff_attn_kernel(slopes_ref, lq1_ref, lk1_ref, lq2_ref, lk2_ref, qaug_ref,
                      q_ref, k_ref, v_ref, w_ref, o_ref,
                      k1a_sc, k2a_sc, vt_sc, a1, a2, st, sa1, sa2, sb1, sb2, *, tq, lam_init):
    h = pl.program_id(1)
    qi = pl.program_id(2)
    slope = slopes_ref[h]
    n_chunks = k1a_sc.shape[0]

    @pl.when(qi == 0)
    def _():
        lane = lax.broadcasted_iota(jnp.int32, (tq, LANES), 1)
        aug1 = _key_aug(tq, HEAD_DIM).astype(BF16)
        aug2 = _key_aug(tq, 0).astype(BF16)

        def build(c, carry):
            rows = pl.ds(pl.multiple_of(c * tq, tq), tq)
            k = k_ref[rows, :]
            k1a_sc[c] = jnp.where(lane < HEAD_DIM, k, aug1)
            k2a_sc[c] = jnp.where(lane >= HEAD_DIM, k, aug2)
            vt_sc[c] = v_ref[rows, :].astype(F32).T.astype(BF16)
            return carry

        lax.fori_loop(0, n_chunks, build, 0)

    qt = q_ref[...].astype(F32).T
    row = lax.broadcasted_iota(jnp.int32, qt.shape, 0)
    qt1 = jnp.where(row < HEAD_DIM, qt, qaug_ref[0, 0]).astype(BF16)
    qt2 = jnp.where(row >= HEAD_DIM, qt, qaug_ref[0, 1]).astype(BF16)
    a1[...] = jnp.zeros(a1.shape, F32)
    a2[...] = jnp.zeros(a2.shape, F32)

    M1, L1, M2, L2, XA1, XA2 = range(6)
    for r_, v_ in ((M1, NEG_BIG), (L1, 0.0), (M2, NEG_BIG), (L2, 0.0)):
        st[r_:r_ + 1, :] = jnp.full((1, tq), v_, F32)

    maps = ((k1a_sc, qt1, a1, M1, L1, XA1), (k2a_sc, qt2, a2, M2, L2, XA2))
    tc = min(tq, STRIP)
    pieces = [(mp, slice(h * tc, (h + 1) * tc)) for h in range(tq // tc) for mp in range(2)]

    def scores(j, mp, cols, dst, masked=False):
        s = jnp.dot(maps[mp][0][j], maps[mp][1][:, cols], preferred_element_type=F32)
        if masked:
            krow = lax.broadcasted_iota(jnp.int32, s.shape, 0)
            qcol = lax.broadcasted_iota(jnp.int32, s.shape, 1) + cols.start
            s = jnp.where(krow <= qcol, s, NEG_BIG)
        dst[mp][:, cols] = s
        return jnp.max(s, axis=0, keepdims=True)

    def accumulate(j, mp, cols, src, mx):
        _, _, a_sc, mr, lr, _ = maps[mp]
        c = slope * jnp.full((1, tc), (j - qi) * tq, jnp.int32).astype(F32)
        m = st[mr:mr + 1, cols]
        m_new = jnp.maximum(m, mx + c)
        alpha = jnp.exp2(m - m_new)
        p = jnp.exp2(src[mp][:, cols] - (m_new - c))
        st[lr:lr + 1, cols] = alpha * st[lr:lr + 1, cols] + jnp.sum(p, axis=0, keepdims=True)
        st[mr:mr + 1, cols] = m_new
        a_sc[:, cols] = alpha * a_sc[:, cols] + jnp.dot(vt_sc[j], p.astype(BF16),
                                                        preferred_element_type=F32)

    buf_a, buf_b = (sa1, sa2), (sb1, sb2)

    def parked_max(mp, cols):
        xr = maps[mp][5]
        return st[xr:xr + 1, cols]

    for mp, cols in pieces:
        xr = maps[mp][5]
        st[xr:xr + 1, cols] = scores(0, mp, cols, buf_a)

    def pair(i, carry):
        j0 = 2 * i
        j2 = jnp.minimum(j0 + 2, qi - 1)
        xb = []
        for mp, cols in pieces:
            xb.append(scores(j0 + 1, mp, cols, buf_b))
            accumulate(j0, mp, cols, buf_a, parked_max(mp, cols))
        for (mp, cols), xbk in zip(pieces, xb):
            xa = scores(j2, mp, cols, buf_a)
            accumulate(j0 + 1, mp, cols, buf_b, xbk)
            xr = maps[mp][5]
            st[xr:xr + 1, cols] = xa
        return carry

    lax.fori_loop(0, qi // 2, pair, 0)

    @pl.when(qi % 2 == 1)
    def _():
        for mp, cols in pieces:
            accumulate(qi - 1, mp, cols, buf_a, parked_max(mp, cols))

    xd = [scores(qi, mp, cols, buf_a, masked=True) for mp, cols in pieces]
    for (mp, cols), xdk in zip(pieces, xd):
        accumulate(qi, mp, cols, buf_a, xdk)
    l1 = st[L1:L1 + 1, :]
    l2 = st[L2:L2 + 1, :]

    lam = (jnp.exp(jnp.sum(lq1_ref[...] * lk1_ref[...], axis=1, keepdims=True))
           - jnp.exp(jnp.sum(lq2_ref[...] * lk2_ref[...], axis=1, keepdims=True))
           + lam_init)
    o = (a1[...] / l1 - lam * (a2[...] / l2)).T
    y = o * lax.rsqrt(jnp.mean(jnp.square(o), axis=1, keepdims=True) + RMS_EPS)
    y = (y * w_ref[...]) * (1.0 - lam_init)
    o_ref[...] = y.astype(o_ref.dtype)


def _diff_attention(proj, slopes_l2, qaug, lq1, lk1, lq2, lk2, subln_w, *, batch, seq, lam_init, tq):
    n = proj.shape[0]
    nq = seq // tq
    kern = functools.partial(_diff_attn_kernel, tq=tq, lam_init=lam_init)
    vec = pl.BlockSpec((1, HEAD_DIM), lambda b, h, i: (0, 0))
    cq, ck, cv = _COL_DQ // LANES, _COL_DK // LANES, _COL_DV // LANES
    return pl.pallas_call(
        kern,
        out_shape=jax.ShapeDtypeStruct((n, DIFF_HEADS * 2 * HEAD_DIM), BF16),
        grid=(batch, DIFF_HEADS, nq),
        in_specs=[pl.BlockSpec(memory_space=pltpu.SMEM),
                  vec, vec, vec, vec,
                  pl.BlockSpec((1, 2, LANES, tq), lambda b, h, i: (h, 0, 0, 0)),
                  pl.BlockSpec((tq, LANES), lambda b, h, i: (b * nq + i, cq + h)),
                  pl.BlockSpec((seq, LANES), lambda b, h, i: (b, ck + h)),
                  pl.BlockSpec((seq, LANES), lambda b, h, i: (b, cv + h)),
                  pl.BlockSpec((1, 2 * HEAD_DIM), lambda b, h, i: (0, 0))],
        out_specs=pl.BlockSpec((tq, LANES), lambda b, h, i: (b * nq + i, h)),
        scratch_shapes=[pltpu.VMEM((nq, tq, LANES), BF16), pltpu.VMEM((nq, tq, LANES), BF16),
                        pltpu.VMEM((nq, LANES, tq), BF16),
                        pltpu.VMEM((2 * HEAD_DIM, tq), F32), pltpu.VMEM((2 * HEAD_DIM, tq), F32),
                        pltpu.VMEM((8, tq), F32)] + [pltpu.VMEM((tq, tq), F32)] * 4,
        compiler_params=_params(("arbitrary", "arbitrary", "arbitrary")),
        name="diff_attn",
    )(slopes_l2, lq1, lk1, lq2, lk2, qaug, proj, proj, proj, subln_w)


def _diff_query_aug(slopes_l2, tq):
    s0 = slopes_l2.astype(BF16).astype(F32)
    s1 = (slopes_l2 - s0).astype(BF16).astype(F32)
    s2 = (slopes_l2 - s0 - s1).astype(BF16).astype(F32)
    parts = jnp.stack([s0, s1, s2, s0, s1, s2], axis=1)
    n_heads = slopes_l2.shape[0]
    cols = jnp.zeros((n_heads, 2, LANES), F32)
    cols = cols.at[:, 0, HEAD_DIM:HEAD_DIM + N_AUG].set(parts)
    cols = cols.at[:, 1, 0:N_AUG].set(parts)
    return jnp.broadcast_to(cols[:, :, :, None], (n_heads, 2, LANES, tq))


def _swa_kernel(slopes_ref, sinks_ref, q_ref, kp_ref, kc_ref, vp_ref, vc_ref, o_ref, *, tq):
    qi = pl.program_id(1)
    scale = HEAD_DIM ** -0.5
    kcat = jnp.concatenate([kp_ref[...], kc_ref[...]], axis=0)
    vcat = jnp.concatenate([vp_ref[...], vc_ref[...]], axis=0)
    row = lax.broadcasted_iota(jnp.int32, (WINDOW, 2 * WINDOW), 0)
    col = lax.broadcasted_iota(jnp.int32, (WINDOW, 2 * WINDOW), 1)
    dist = row + WINDOW - col
    valid = (dist >= 0) & (dist < WINDOW)
    distf = dist.astype(F32)
    valid_first = valid & ((col >= WINDOW) | (qi > 0))
    for j in range(tq // WINDOW):
        kj = kcat[j * WINDOW:(j + 2) * WINDOW]
        vj = vcat[j * WINDOW:(j + 2) * WINDOW]
        qj = q_ref[j * WINDOW:(j + 1) * WINDOW, :] * scale
        vmask = valid_first if j == 0 else valid
        for hk in range(SWA_KV_HEADS):
            kk = kj[:, hk * HEAD_DIM:(hk + 1) * HEAD_DIM]
            vv = vj[:, hk * HEAD_DIM:(hk + 1) * HEAD_DIM]
            for g in range(SWA_GROUP):
                hq = hk * SWA_GROUP + g
                qh = qj[:, hq * HEAD_DIM:(hq + 1) * HEAD_DIM]
                s = lax.dot_general(qh, kk, (((1,), (1,)), ((), ())),
                                    preferred_element_type=F32)
                s = s - slopes_ref[hq] * distf
                s = jnp.where(vmask, s, NEG_BIG)
                sink = sinks_ref[hq]
                m = jnp.maximum(jnp.max(s, axis=1, keepdims=True), sink)
                e = jnp.exp(s - m)
                denom = jnp.sum(e, axis=1, keepdims=True) + jnp.exp(sink - m)
                p = (e / denom).astype(BF16)
                o = jnp.dot(p, vv, preferred_element_type=F32)
                o_ref[j * WINDOW:(j + 1) * WINDOW,
                      hq * HEAD_DIM:(hq + 1) * HEAD_DIM] = o.astype(o_ref.dtype)


def _swa_attention(proj, slopes, sinks, *, batch, seq, tq):
    n = proj.shape[0]
    nq = seq // tq
    sub = tq // WINDOW
    nwin = seq // WINDOW
    kern = functools.partial(_swa_kernel, tq=tq)
    cq = _COL_SQ // (SWA_HEADS * HEAD_DIM)
    ck, cv = _COL_SK // LANES, _COL_SV // LANES
    prev = lambda c: (lambda b, i: (b * nwin + jnp.maximum(i * sub - 1, 0), c))
    cur = lambda c: (lambda b, i: (b * nq + i, c))
    smem = pl.BlockSpec(memory_space=pltpu.SMEM)
    return pl.pallas_call(
        kern,
        out_shape=jax.ShapeDtypeStruct((n, SWA_HEADS * HEAD_DIM), BF16),
        grid=(batch, nq),
        in_specs=[smem, smem,
                  pl.BlockSpec((tq, SWA_HEADS * HEAD_DIM), cur(cq)),
                  pl.BlockSpec((WINDOW, LANES), prev(ck)),
                  pl.BlockSpec((tq, LANES), cur(ck)),
                  pl.BlockSpec((WINDOW, LANES), prev(cv)),
                  pl.BlockSpec((tq, LANES), cur(cv))],
        out_specs=pl.BlockSpec((tq, SWA_HEADS * HEAD_DIM), lambda b, i: (b * nq + i, 0)),
        compiler_params=_params(("parallel", "arbitrary")),
        name="swa_attn",
    )(slopes, sinks, proj, proj, proj, proj, proj)


def _layer_norm(y, g, b):
    mu = jnp.mean(y, axis=1, keepdims=True)
    var = jnp.mean(jnp.square(y - mu), axis=1, keepdims=True)
    return (y - mu) * lax.rsqrt(var + LN_EPS) * g + b


def _merge_kernel(x_ref, od_ref, os_ref, ga_ref, gb_ref, p_ref,
                  wa_ref, wb_ref, wo_ref, bo_ref, g1_ref, b1_ref, wpg_ref, wpp_ref,
                  x1_ref, r_ref, *, dn_alpha):
    a = jnp.dot(od_ref[...], wa_ref[...], preferred_element_type=F32)
    b = jnp.dot(os_ref[...], wb_ref[...], preferred_element_type=F32)
    merged = (jax.nn.sigmoid(ga_ref[...].astype(F32)) * a
              + jax.nn.sigmoid(gb_ref[...].astype(F32)) * b)
    mix = jnp.dot(merged.astype(BF16), wo_ref[...], preferred_element_type=F32) + bo_ref[...]
    x1 = _layer_norm(dn_alpha * x_ref[...] + mix, g1_ref[...], b1_ref[...])
    x1_ref[...] = x1
    gate = jax.nn.sigmoid(jnp.dot(x1.astype(BF16), wpg_ref[...], preferred_element_type=F32))
    ple = gate * jnp.dot(p_ref[...].astype(BF16), wpp_ref[...], preferred_element_type=F32)
    r_ref[...] = dn_alpha * x1 + ple


def _merge(x, od, osw, proj, p, wa, wb, wo, bo, g1, b1, wpg, wpp, *, dn_alpha, tm):
    n, d = x.shape
    pd = p.shape[1]
    row = lambda c: (lambda i: (i, c))
    full = lambda shape: pl.BlockSpec(shape, lambda i: (0, 0))
    kern = functools.partial(_merge_kernel, dn_alpha=dn_alpha)
    return pl.pallas_call(
        kern,
        out_shape=(jax.ShapeDtypeStruct((n, d), F32), jax.ShapeDtypeStruct((n, d), F32)),
        grid=(n // tm,),
        in_specs=[pl.BlockSpec((tm, d), row(0)),
                  pl.BlockSpec((tm, d), row(0)),
                  pl.BlockSpec((tm, d), row(0)),
                  pl.BlockSpec((tm, d), row(_COL_GA // d)),
                  pl.BlockSpec((tm, d), row(_COL_GB // d)),
                  pl.BlockSpec((tm, pd), row(0)),
                  full((d, d)), full((d, d)), full((d, d)), full((1, d)),
                  full((1, d)), full((1, d)), full((d, d)), full((pd, d))],
        out_specs=(pl.BlockSpec((tm, d), row(0)), pl.BlockSpec((tm, d), row(0))),
        compiler_params=_params(("parallel",)),
        name="merge_ln1",
    )(x, od, osw, proj, proj, p, wa, wb, wo, bo, g1, b1, wpg, wpp)


def _router_kernel(x_ref, w_ref, b_ref, code_ref, gate_ref, cnt_ref, carry, *, tm):
    @pl.when(pl.program_id(0) == 0)
    def _():
        carry[...] = jnp.zeros(carry.shape, F32)

    logits = jnp.dot(x_ref[...], w_ref[...], preferred_element_type=F32,
                     precision=lax.Precision.HIGHEST) + b_ref[...]
    lane = lax.broadcasted_iota(jnp.int32, logits.shape, 1)
    lanef = lane.astype(F32)
    work = logits
    tops, idxs = [], []
    onehot = jnp.zeros(logits.shape, F32)
    for _ in range(TOP_K):
        m = jnp.max(work, axis=1, keepdims=True)
        idx = jnp.min(jnp.where(work == m, lanef, float(LANES)), axis=1, keepdims=True)
        sel = lanef == idx
        onehot = jnp.where(sel, 1.0, onehot)
        work = jnp.where(sel, -jnp.inf, work)
        tops.append(m)
        idxs.append(idx)
    es = [jnp.exp(t - tops[0]) for t in tops]
    denom = es[0] + es[1] + es[2] + es[3]
    r = lax.broadcasted_iota(jnp.int32, (tm, tm), 0)
    c = lax.broadcasted_iota(jnp.int32, (tm, tm), 1)
    tri = jnp.where(c < r, 1.0, 0.0).astype(BF16)
    before = jnp.dot(tri, onehot.astype(BF16), preferred_element_type=F32) + carry[0:1, :]
    code = jnp.zeros(logits.shape, jnp.int32)
    gate = jnp.zeros(logits.shape, F32)
    for k in range(TOP_K):
        rank = jnp.sum(jnp.where(lanef == idxs[k], before, 0.0), axis=1, keepdims=True)
        ck = (idxs[k] * float(1 << RANK_BITS) + rank).astype(jnp.int32)
        code = jnp.where(lane == k, ck, code)
        gate = jnp.where(lane == k, es[k] / denom, gate)
    code_ref[...] = code
    gate_ref[...] = gate
    carry[0:1, :] = carry[0:1, :] + jnp.sum(onehot, axis=0, keepdims=True)
    cnt_ref[...] = carry[...]


def _router(x1, w, b, *, tm):
    n, d = x1.shape
    kern = functools.partial(_router_kernel, tm=tm)
    return pl.pallas_call(
        kern,
        out_shape=(jax.ShapeDtypeStruct((n, LANES), jnp.int32),
                   jax.ShapeDtypeStruct((n, LANES), F32),
                   jax.ShapeDtypeStruct((8, LANES), F32)),
        grid=(n // tm,),
        in_specs=[pl.BlockSpec((tm, d), lambda i: (i, 0)),
                  pl.BlockSpec((d, LANES), lambda i: (0, 0)),
                  pl.BlockSpec((1, LANES), lambda i: (0, 0))],
        out_specs=(pl.BlockSpec((tm, LANES), lambda i: (i, 0)),
                   pl.BlockSpec((tm, LANES), lambda i: (i, 0)),
                   pl.BlockSpec((8, LANES), lambda i: (0, 0))),
        scratch_shapes=[pltpu.VMEM((8, LANES), F32)],
        compiler_params=_params(("arbitrary",)),
        name="router",
    )(x1, w, b)


def _slot(code, off_ref):
    return off_ref[code >> RANK_BITS] + (code & ((1 << RANK_BITS) - 1))


def _dispatch_kernel(off_ref, codes_ref, x_ref, xs_in_ref, xs_ref, codes_smem, csem, sem, *, rows):
    del xs_in_ref
    i = pl.program_id(0)
    n_assign = rows * TOP_K
    load = pltpu.make_async_copy(codes_ref.at[pl.ds(i * n_assign, n_assign)], codes_smem, csem)
    load.start()
    load.wait()

    def issue(t, carry):
        src = x_ref.at[pl.ds(t, 1)]
        for k in range(TOP_K):
            slot = _slot(codes_smem[t * TOP_K + k], off_ref)
            pltpu.make_async_copy(src, xs_ref.at[pl.ds(slot, 1)], sem).start()
        return carry

    lax.fori_loop(0, rows, issue, 0, unroll=2)
    pltpu.make_async_copy(xs_ref.at[pl.ds(0, n_assign)], xs_ref.at[pl.ds(0, n_assign)], sem).wait()


def _dispatch(off, codes, x1, xs_zero, *, rows):
    n, d = x1.shape
    kern = functools.partial(_dispatch_kernel, rows=rows)
    any_spec = pl.BlockSpec(memory_space=pl.ANY)
    return pl.pallas_call(
        kern,
        out_shape=jax.ShapeDtypeStruct(xs_zero.shape, xs_zero.dtype),
        grid_spec=pltpu.PrefetchScalarGridSpec(
            num_scalar_prefetch=1,
            grid=(n // rows,),
            in_specs=[any_spec, pl.BlockSpec((rows, d), lambda i, off: (i, 0)), any_spec],
            out_specs=any_spec,
            scratch_shapes=[pltpu.SMEM((rows * TOP_K,), jnp.int32),
                            pltpu.SemaphoreType.DMA, pltpu.SemaphoreType.DMA]),
        input_output_aliases={3: 0},
        compiler_params=_dma_params(("arbitrary",)),
        name="moe_dispatch",
    )(off, codes, x1, xs_zero)


def _expert_kernel(be_ref, xs_ref, wgu_ref, bgu_ref, wd_ref, bd_ref, y_ref):
    del be_ref
    dff = wd_ref.shape[1]
    gu = jnp.dot(xs_ref[...].astype(BF16), wgu_ref[0], preferred_element_type=F32) + bgu_ref[0]
    gate = jnp.minimum(gu[:, :dff], SWIGLU_LIMIT)
    up = jnp.clip(gu[:, dff:], -SWIGLU_LIMIT, SWIGLU_LIMIT)
    act = (up + 1.0) * (gate * jax.nn.sigmoid(SWIGLU_ALPHA * gate))
    y_ref[...] = jnp.dot(act.astype(BF16), wd_ref[0], preferred_element_type=F32) + bd_ref[0]


def _experts(block_e, xs, wgu, bgu, wd, bd):
    n_slots, d = xs.shape
    dff = wd.shape[1]
    nb = n_slots // MOE_BLOCK
    return pl.pallas_call(
        _expert_kernel,
        out_shape=jax.ShapeDtypeStruct((n_slots, d), F32),
        grid_spec=pltpu.PrefetchScalarGridSpec(
            num_scalar_prefetch=1,
            grid=(nb,),
            in_specs=[pl.BlockSpec((MOE_BLOCK, d), lambda j, be: (j, 0)),
                      pl.BlockSpec((1, d, 2 * dff), lambda j, be: (be[j], 0, 0)),
                      pl.BlockSpec((1, 1, 2 * dff), lambda j, be: (be[j], 0, 0)),
                      pl.BlockSpec((1, dff, d), lambda j, be: (be[j], 0, 0)),
                      pl.BlockSpec((1, 1, d), lambda j, be: (be[j], 0, 0))],
            out_specs=pl.BlockSpec((MOE_BLOCK, d), lambda j, be: (j, 0))),
        compiler_params=_params(("arbitrary",)),
        name="moe_experts",
    )(block_e, xs, wgu, bgu, wd, bd)


def _combine_kernel(off_ref, codes_ref, gate_ref, r_ref, y_ref, g2_ref, b2_ref,
                    x2_ref, xb_ref, codes_smem, ybuf, csem, sem, *, rows):
    i = pl.program_id(0)
    n_assign = rows * TOP_K
    load = pltpu.make_async_copy(codes_ref.at[pl.ds(i * n_assign, n_assign)], codes_smem, csem)
    load.start()
    load.wait()

    def issue(t, carry):
        for k in range(TOP_K):
            slot = _slot(codes_smem[t * TOP_K + k], off_ref)
            pltpu.make_async_copy(y_ref.at[pl.ds(slot, 1)],
                                  ybuf.at[pl.ds(k * rows + t, 1)], sem).start()
        return carry

    lax.fori_loop(0, rows, issue, 0, unroll=2)
    pltpu.make_async_copy(y_ref.at[pl.ds(0, n_assign)], ybuf, sem).wait()

    gate = gate_ref[...]
    acc = r_ref[...]
    for k in range(TOP_K):
        acc = acc + gate[:, k:k + 1] * ybuf[k * rows:(k + 1) * rows, :]
    x2 = _layer_norm(acc, g2_ref[...], b2_ref[...])
    x2_ref[...] = x2
    xb_ref[...] = x2.astype(BF16)


def _combine(off, codes, gates, r, y, g2, b2, *, rows):
    n, d = r.shape
    kern = functools.partial(_combine_kernel, rows=rows)
    any_spec = pl.BlockSpec(memory_space=pl.ANY)
    return pl.pallas_call(
        kern,
        out_shape=(jax.ShapeDtypeStruct((n, d), F32), jax.ShapeDtypeStruct((n, d), BF16)),
        grid_spec=pltpu.PrefetchScalarGridSpec(
            num_scalar_prefetch=1,
            grid=(n // rows,),
            in_specs=[any_spec,
                      pl.BlockSpec((rows, LANES), lambda i, off: (i, 0)),
                      pl.BlockSpec((rows, d), lambda i, off: (i, 0)),
                      any_spec,
                      pl.BlockSpec((1, d), lambda i, off: (0, 0)),
                      pl.BlockSpec((1, d), lambda i, off: (0, 0))],
            out_specs=(pl.BlockSpec((rows, d), lambda i, off: (i, 0)),
                       pl.BlockSpec((rows, d), lambda i, off: (i, 0))),
            scratch_shapes=[pltpu.SMEM((rows * TOP_K,), jnp.int32),
                            pltpu.VMEM((rows * TOP_K, d), F32),
                            pltpu.SemaphoreType.DMA, pltpu.SemaphoreType.DMA]),
        compiler_params=_dma_params(("arbitrary",)),
        name="moe_combine_ln2",
    )(off, codes, gates, r, y, g2, b2)


def _alibi_slopes(n_heads):
    h = jnp.arange(1, n_heads + 1, dtype=F32)
    return jnp.exp2(-8.0 * h / n_heads)


def _block_size(n, target):
    t = min(n, target)
    while n % t:
        t //= 2
    return t


def kernel(x, p, w_in, b_in, lambda_q1, lambda_k1, lambda_q2, lambda_k2, subln_w, sinks,
           w_br_diff, w_br_swa, w_out, b_out, ln1_g, ln1_b, w_router, b_router,
           w_gate_up, b_gate_up, w_down, b_down, w_ple_gate, w_ple_proj, ln2_g, ln2_b):
    batch, seq, d = x.shape
    depth = w_in.shape[0]
    n = batch * seq
    dn_alpha = (2 * depth) ** 0.25
    n_assign = n * TOP_K
    n_blocks = n_assign // MOE_BLOCK + N_EXPERTS + 1
    n_slots = n_blocks * MOE_BLOCK

    n_in = w_in.shape[2]
    perm = jnp.concatenate([jnp.arange(n_in - 2 * d, n_in), jnp.arange(0, n_in - 2 * d)])
    log2e = math.log2(math.e)
    diff_slopes = _alibi_slopes(DIFF_HEADS) * log2e
    swa_slopes = _alibi_slopes(SWA_HEADS)

    tm_lin = _block_size(n, 1024)
    tq_diff = _block_size(seq, 512)
    tq_swa = _block_size(seq, 256)
    tm_merge = _block_size(n, 512)
    tm_router = _block_size(n, 512)
    rows_moe = _block_size(n, 256)

    xf = x.reshape(n, d)
    xb = xf.astype(BF16)
    for i in range(depth):
        lam_init = 0.8 - 0.6 * math.exp(-0.3 * i)
        col_scale = jnp.ones((n_in,), F32).at[_COL_DQ:_COL_DK].set(HEAD_DIM ** -0.5 * log2e)
        w_in_b = (jnp.take(w_in[i], perm, axis=1) * col_scale).astype(BF16)
        b_in_p = (jnp.take(b_in[i], perm) * col_scale)[None, :]
        proj = _linear(xb, w_in_b, b_in_p, tm=tm_lin, tn=1280)

        od = _diff_attention(proj, diff_slopes, _diff_query_aug(diff_slopes, tq_diff),
                             lambda_q1[i][None, :], lambda_k1[i][None, :],
                             lambda_q2[i][None, :], lambda_k2[i][None, :],
                             subln_w[i][None, :], batch=batch, seq=seq,
                             lam_init=lam_init, tq=tq_diff)
        osw = _swa_attention(proj, swa_slopes, sinks[i].astype(F32), batch=batch, seq=seq, tq=tq_swa)

        x1, r = _merge(xf, od, osw, proj, p[i].reshape(n, -1),
                       w_br_diff[i].astype(BF16), w_br_swa[i].astype(BF16),
                       w_out[i].astype(BF16), b_out[i][None, :],
                       ln1_g[i][None, :], ln1_b[i][None, :],
                       w_ple_gate[i].astype(BF16), w_ple_proj[i].astype(BF16),
                       dn_alpha=dn_alpha, tm=tm_merge)

        wr = jnp.zeros((d, LANES), F32).at[:, :N_EXPERTS].set(w_router[i])
        br = jnp.full((1, LANES), NEG_BIG, F32).at[0, :N_EXPERTS].set(b_router[i])
        code, gates, cnt = _router(x1, wr, br, tm=tm_router)

        counts = cnt[0, :N_EXPERTS].astype(jnp.int32)
        padded = (counts + MOE_BLOCK - 1) // MOE_BLOCK * MOE_BLOCK
        padded_end = jnp.cumsum(padded)
        off = (padded_end - padded).astype(jnp.int32)
        block_start = jnp.arange(n_blocks, dtype=jnp.int32) * MOE_BLOCK
        block_e = jnp.minimum(
            jnp.sum((block_start[:, None] >= padded_end[None, :]).astype(jnp.int32), axis=1),
            N_EXPERTS - 1)
        codes = code[:, :TOP_K].reshape(n_assign)

        xs = _dispatch(off, codes, x1, jnp.zeros((n_slots, d), F32), rows=rows_moe)
        y = _experts(block_e, xs, w_gate_up[i].astype(BF16), b_gate_up[i][:, None, :],
                     w_down[i].astype(BF16), b_down[i][:, None, :])
        xf, xb = _combine(off, codes, gates, r, y, ln2_g[i][None, :], ln2_b[i][None, :],
                          rows=rows_moe)
    return xf.reshape(batch, seq, d)
```

```python
import functools
import math

import jax
import jax.numpy as jnp
from jax import lax
from jax.experimental import pallas as pl
from jax.experimental.pallas import tpu as pltpu

F32 = jnp.float32
BF16 = jnp.bfloat16

HEAD_DIM = 64
DIFF_HEADS = 8
SWA_HEADS = 16
SWA_KV_HEADS = 2
SWA_GROUP = SWA_HEADS // SWA_KV_HEADS
WINDOW = 128
N_EXPERTS = 32
TOP_K = 4
MOE_BLOCK = 512
SWIGLU_LIMIT = 7.0
SWIGLU_ALPHA = 1.702
LN_EPS = 1e-5
RMS_EPS = 1e-5
NEG_BIG = -1e30

LANES = 128
VMEM_LIMIT = 56 * 1024 * 1024
RANK_BITS = 16

_COL_GA, _COL_GB, _COL_DQ, _COL_DK, _COL_DV, _COL_SQ, _COL_SK, _COL_SV = (
    0, 1024, 2048, 3072, 4096, 5120, 6144, 6272)


def _params(semantics):
    return pltpu.CompilerParams(dimension_semantics=semantics,
                                vmem_limit_bytes=VMEM_LIMIT)


def _dma_params(semantics):
    return pltpu.CompilerParams(dimension_semantics=semantics,
                                vmem_limit_bytes=VMEM_LIMIT,
                                disable_bounds_checks=True)


def _linear_kernel(x_ref, w_ref, b_ref, o_ref):
    acc = jnp.dot(x_ref[...], w_ref[...], preferred_element_type=F32)
    o_ref[...] = (acc + b_ref[...]).astype(o_ref.dtype)


def _linear(x, w, b, *, tm, tn):
    n, k = x.shape
    nout = w.shape[1]
    return pl.pallas_call(
        _linear_kernel,
        out_shape=jax.ShapeDtypeStruct((n, nout), BF16),
        grid=(n // tm, nout // tn),
        in_specs=[pl.BlockSpec((tm, k), lambda i, j: (i, 0)),
                  pl.BlockSpec((k, tn), lambda i, j: (0, j)),
                  pl.BlockSpec((1, tn), lambda i, j: (0, j))],
        out_specs=pl.BlockSpec((tm, tn), lambda i, j: (i, j)),
        compiler_params=_params(("parallel", "arbitrary")),
        name="in_proj",
    )(x, w, b)


N_AUG = 6
STRIP = 256
DV = 2 * HEAD_DIM
ONES_ROWS = 16


def _key_aug(tk, first_lane):
    r = lax.broadcasted_iota(jnp.int32, (tk, LANES), 0)
    lane = lax.broadcasted_iota(jnp.int32, (tk, LANES), 1) - first_lane
    hi = ((r >> 7) << 7).astype(F32)
    lo = (r & 127).astype(F32)
    return jnp.where((lane >= 0) & (lane < 3), hi,
                     jnp.where((lane >= 3) & (lane < N_AUG), lo, 0.0))


def _diff_attn_kernel(slopes_ref, lq1_ref, lk1_ref, lq2_ref, lk2_ref, qaug_ref,
                      q_ref, k_ref, v_ref, w_ref, o_ref,
                      k1a_sc, k2a_sc, vt_sc, a1, a2, st, sa1, sa2, sb1, sb2, *, tq, lam_init):
    h = pl.program_id(1)
    qi = pl.program_id(2)
    slope = slopes_ref[h]
    n_chunks = k1a_sc.shape[0]

    @pl.when(qi == 0)
    def _():
        lane = lax.broadcasted_iota(jnp.int32, (tq, LANES), 1)
        aug1 = _key_aug(tq, HEAD_DIM).astype(BF16)
        aug2 = _key_aug(tq, 0).astype(BF16)

        def build(c, carry):
            rows = pl.ds(pl.multiple_of(c * tq, tq), tq)
            k = k_ref[rows, :]
            k1a_sc[c] = jnp.where(lane < HEAD_DIM, k, aug1)
            k2a_sc[c] = jnp.where(lane >= HEAD_DIM, k, aug2)
            vt_sc[c, 0:DV, :] = v_ref[rows, :].astype(F32).T.astype(BF16)
            vt_sc[c, DV:DV + ONES_ROWS, :] = jnp.where(
                lax.broadcasted_iota(jnp.int32, (ONES_ROWS, tq), 0) == 0, 1.0, 0.0).astype(BF16)
            return carry

        lax.fori_loop(0, n_chunks, build, 0)

    qt = q_ref[...].astype(F32).T
    row = lax.broadcasted_iota(jnp.int32, qt.shape, 0)
    qt1 = jnp.where(row < HEAD_DIM, qt, qaug_ref[0, 0]).astype(BF16)
    qt2 = jnp.where(row >= HEAD_DIM, qt, qaug_ref[0, 1]).astype(BF16)
    a1[...] = jnp.zeros(a1.shape, F32)
    a2[...] = jnp.zeros(a2.shape, F32)

    M1, M2, XA1, XA2 = range(4)
    for r_ in (M1, M2):
        st[r_:r_ + 1, :] = jnp.full((1, tq), NEG_BIG, F32)

    maps = ((k1a_sc, qt1, a1, M1, XA1), (k2a_sc, qt2, a2, M2, XA2))
    tc = min(tq, STRIP)
    pieces = [(mp, slice(h * tc, (h + 1) * tc)) for h in range(tq // tc) for mp in range(2)]

    def scores(j, mp, cols, dst, masked=False):
        s = jnp.dot(maps[mp][0][j], maps[mp][1][:, cols], preferred_element_type=F32)
        if masked:
            krow = lax.broadcasted_iota(jnp.int32, s.shape, 0)
            qcol = lax.broadcasted_iota(jnp.int32, s.shape, 1) + cols.start
            s = jnp.where(krow <= qcol, s, NEG_BIG)
        dst[mp][:, cols] = s
        return jnp.max(s, axis=0, keepdims=True)

    def accumulate(j, mp, cols, src, mx):
        _, _, a_sc, mr, _ = maps[mp]
        c = slope * jnp.full((1, tc), (j - qi) * tq, jnp.int32).astype(F32)
        m = st[mr:mr + 1, cols]
        m_new = jnp.maximum(m, mx + c)
        alpha = jnp.exp2(m - m_new)
        p = jnp.exp2(src[mp][:, cols] - (m_new - c))
        st[mr:mr + 1, cols] = m_new
        a_sc[:, cols] = alpha * a_sc[:, cols] + jnp.dot(vt_sc[j], p.astype(BF16),
                                                        preferred_element_type=F32)

    bufs = ((sa1, sa2), (sb1, sb2))

    def park(maxima):
        for (mp, cols), x in zip(pieces, maxima):
            xr = maps[mp][4]
            st[xr:xr + 1, cols] = x

    def parked():
        return [st[maps[mp][4]:maps[mp][4] + 1, cols] for mp, cols in pieces]

    def step(s_blk, s_dst, p_blk, p_src, p_max, masked=False):
        out = []
        for (mp, cols), x in zip(pieces, p_max):
            out.append(scores(s_blk, mp, cols, s_dst, masked))
            accumulate(p_blk, mp, cols, p_src, x)
        return out

    @pl.when(qi == 0)
    def _():
        xd = [scores(0, mp, cols, bufs[0], masked=True) for mp, cols in pieces]
        for (mp, cols), x in zip(pieces, xd):
            accumulate(0, mp, cols, bufs[0], x)

    @pl.when(qi > 0)
    def _():
        park([scores(0, mp, cols, bufs[0]) for mp, cols in pieces])
        n_loop = qi - 1
        n_quad = n_loop // 4

        def quad(i, carry):
            j = 4 * i
            x = parked()
            for t in range(4):
                x = step(j + t + 1, bufs[(t + 1) % 2], j + t, bufs[t % 2], x)
            park(x)
            return carry

        lax.fori_loop(0, n_quad, quad, 0)
        j = 4 * n_quad
        rem = n_loop - j
        for k in range(4):
            @pl.when(rem == k)
            def _(k=k):
                x = parked()
                for t in range(k):
                    x = step(j + t + 1, bufs[(t + 1) % 2], j + t, bufs[t % 2], x)
                cur, nxt = bufs[k % 2], bufs[(k + 1) % 2]
                xd = step(qi, nxt, qi - 1, cur, x, masked=True)
                for (mp, cols), xk in zip(pieces, xd):
                    accumulate(qi, mp, cols, nxt, xk)

    lam = (jnp.exp(jnp.sum(lq1_ref[...] * lk1_ref[...], axis=1, keepdims=True))
           - jnp.exp(jnp.sum(lq2_ref[...] * lk2_ref[...], axis=1, keepdims=True))
           + lam_init)
    o1 = a1[0:DV, :] / a1[DV:DV + 1, :]
    o2 = a2[0:DV, :] / a2[DV:DV + 1, :]
    o = (o1 - lam * o2).T
    y = o * lax.rsqrt(jnp.mean(jnp.square(o), axis=1, keepdims=True) + RMS_EPS)
    y = (y * w_ref[...]) * (1.0 - lam_init)
    o_ref[...] = y.astype(o_ref.dtype)


def _diff_attention(proj, slopes_l2, qaug, lq1, lk1, lq2, lk2, subln_w, *, batch, seq, lam_init, tq):
    n = proj.shape[0]
    nq = seq // tq
    kern = functools.partial(_diff_attn_kernel, tq=tq, lam_init=lam_init)
    vec = pl.BlockSpec((1, HEAD_DIM), lambda b, h, i: (0, 0))
    cq, ck, cv = _COL_DQ // LANES, _COL_DK // LANES, _COL_DV // LANES
    return pl.pallas_call(
        kern,
        out_shape=jax.ShapeDtypeStruct((n, DIFF_HEADS * 2 * HEAD_DIM), BF16),
        grid=(batch, DIFF_HEADS, nq),
        in_specs=[pl.BlockSpec(memory_space=pltpu.SMEM),
                  vec, vec, vec, vec,
                  pl.BlockSpec((1, 2, LANES, tq), lambda b, h, i: (h, 0, 0, 0)),
                  pl.BlockSpec((tq, LANES), lambda b, h, i: (b * nq + i, cq + h)),
                  pl.BlockSpec((seq, LANES), lambda b, h, i: (b, ck + h)),
                  pl.BlockSpec((seq, LANES), lambda b, h, i: (b, cv + h)),
                  pl.BlockSpec((1, 2 * HEAD_DIM), lambda b, h, i: (0, 0))],
        out_specs=pl.BlockSpec((tq, LANES), lambda b, h, i: (b * nq + i, h)),
        scratch_shapes=[pltpu.VMEM((nq, tq, LANES), BF16), pltpu.VMEM((nq, tq, LANES), BF16),
                        pltpu.VMEM((nq, DV + ONES_ROWS, tq), BF16),
                        pltpu.VMEM((DV + ONES_ROWS, tq), F32), pltpu.VMEM((DV + ONES_ROWS, tq), F32),
                        pltpu.VMEM((8, tq), F32)] + [pltpu.VMEM((tq, tq), F32)] * 4,
        compiler_params=_params(("arbitrary", "arbitrary", "arbitrary")),
        name="diff_attn",
    )(slopes_l2, lq1, lk1, lq2, lk2, qaug, proj, proj, proj, subln_w)


def _diff_query_aug(slopes_l2, tq):
    s0 = slopes_l2.astype(BF16).astype(F32)
    s1 = (slopes_l2 - s0).astype(BF16).astype(F32)
    s2 = (slopes_l2 - s0 - s1).astype(BF16).astype(F32)
    parts = jnp.stack([s0, s1, s2, s0, s1, s2], axis=1)
    n_heads = slopes_l2.shape[0]
    cols = jnp.zeros((n_heads, 2, LANES), F32)
    cols = cols.at[:, 0, HEAD_DIM:HEAD_DIM + N_AUG].set(parts)
    cols = cols.at[:, 1, 0:N_AUG].set(parts)
    return jnp.broadcast_to(cols[:, :, :, None], (n_heads, 2, LANES, tq))


def _swa_kernel(slopes_ref, sinks_ref, q_ref, kp_ref, kc_ref, vp_ref, vc_ref, o_ref, *, tq):
    qi = pl.program_id(1)
    scale = HEAD_DIM ** -0.5
    kcat = jnp.concatenate([kp_ref[...], kc_ref[...]], axis=0)
    vcat = jnp.concatenate([vp_ref[...], vc_ref[...]], axis=0)
    row = lax.broadcasted_iota(jnp.int32, (WINDOW, 2 * WINDOW), 0)
    col = lax.broadcasted_iota(jnp.int32, (WINDOW, 2 * WINDOW), 1)
    dist = row + WINDOW - col
    valid = (dist >= 0) & (dist < WINDOW)
    distf = dist.astype(F32)
    valid_first = valid & ((col >= WINDOW) | (qi > 0))
    for j in range(tq // WINDOW):
        kj = kcat[j * WINDOW:(j + 2) * WINDOW]
        vj = vcat[j * WINDOW:(j + 2) * WINDOW]
        qj = q_ref[j * WINDOW:(j + 1) * WINDOW, :] * scale
        vmask = valid_first if j == 0 else valid
        for hk in range(SWA_KV_HEADS):
            kk = kj[:, hk * HEAD_DIM:(hk + 1) * HEAD_DIM]
            vv = vj[:, hk * HEAD_DIM:(hk + 1) * HEAD_DIM]
            for g in range(SWA_GROUP):
                hq = hk * SWA_GROUP + g
                qh = qj[:, hq * HEAD_DIM:(hq + 1) * HEAD_DIM]
                s = lax.dot_general(qh, kk, (((1,), (1,)), ((), ())),
                                    preferred_element_type=F32)
                s = s - slopes_ref[hq] * distf
                s = jnp.where(vmask, s, NEG_BIG)
                sink = sinks_ref[hq]
                m = jnp.maximum(jnp.max(s, axis=1, keepdims=True), sink)
                e = jnp.exp(s - m)
                denom = jnp.sum(e, axis=1, keepdims=True) + jnp.exp(sink - m)
                p = (e / denom).astype(BF16)
                o = jnp.dot(p, vv, preferred_element_type=F32)
                o_ref[j * WINDOW:(j + 1) * WINDOW,
                      hq * HEAD_DIM:(hq + 1) * HEAD_DIM] = o.astype(o_ref.dtype)


def _swa_attention(proj, slopes, sinks, *, batch, seq, tq):
    n = proj.shape[0]
    nq = seq // tq
    sub = tq // WINDOW
    nwin = seq // WINDOW
    kern = functools.partial(_swa_kernel, tq=tq)
    cq = _COL_SQ // (SWA_HEADS * HEAD_DIM)
    ck, cv = _COL_SK // LANES, _COL_SV // LANES
    prev = lambda c: (lambda b, i: (b * nwin + jnp.maximum(i * sub - 1, 0), c))
    cur = lambda c: (lambda b, i: (b * nq + i, c))
    smem = pl.BlockSpec(memory_space=pltpu.SMEM)
    return pl.pallas_call(
        kern,
        out_shape=jax.ShapeDtypeStruct((n, SWA_HEADS * HEAD_DIM), BF16),
        grid=(batch, nq),
        in_specs=[smem, smem,
                  pl.BlockSpec((tq, SWA_HEADS * HEAD_DIM), cur(cq)),
                  pl.BlockSpec((WINDOW, LANES), prev(ck)),
                  pl.BlockSpec((tq, LANES), cur(ck)),
                  pl.BlockSpec((WINDOW, LANES), prev(cv)),
                  pl.BlockSpec((tq, LANES), cur(cv))],
        out_specs=pl.BlockSpec((tq, SWA_HEADS * HEAD_DIM), lambda b, i: (b * nq + i, 0)),
        compiler_params=_params(("parallel", "arbitrary")),
        name="swa_attn",
    )(slopes, sinks, proj, proj, proj, proj, proj)


def _layer_norm(y, g, b):
    mu = jnp.mean(y, axis=1, keepdims=True)
    var = jnp.mean(jnp.square(y - mu), axis=1, keepdims=True)
    return (y - mu) * lax.rsqrt(var + LN_EPS) * g + b


def _merge_kernel(x_ref, od_ref, os_ref, ga_ref, gb_ref, p_ref,
                  wa_ref, wb_ref, wo_ref, bo_ref, g1_ref, b1_ref, wpg_ref, wpp_ref,
                  x1_ref, r_ref, *, dn_alpha):
    a = jnp.dot(od_ref[...], wa_ref[...], preferred_element_type=F32)
    b = jnp.dot(os_ref[...], wb_ref[...], preferred_element_type=F32)
    merged = (jax.nn.sigmoid(ga_ref[...].astype(F32)) * a
              + jax.nn.sigmoid(gb_ref[...].astype(F32)) * b)
    mix = jnp.dot(merged.astype(BF16), wo_ref[...], preferred_element_type=F32) + bo_ref[...]
    x1 = _layer_norm(dn_alpha * x_ref[...] + mix, g1_ref[...], b1_ref[...])
    x1_ref[...] = x1
    gate = jax.nn.sigmoid(jnp.dot(x1.astype(BF16), wpg_ref[...], preferred_element_type=F32))
    ple = gate * jnp.dot(p_ref[...].astype(BF16), wpp_ref[...], preferred_element_type=F32)
    r_ref[...] = dn_alpha * x1 + ple


def _merge(x, od, osw, proj, p, wa, wb, wo, bo, g1, b1, wpg, wpp, *, dn_alpha, tm):
    n, d = x.shape
    pd = p.shape[1]
    row = lambda c: (lambda i: (i, c))
    full = lambda shape: pl.BlockSpec(shape, lambda i: (0, 0))
    kern = functools.partial(_merge_kernel, dn_alpha=dn_alpha)
    return pl.pallas_call(
        kern,
        out_shape=(jax.ShapeDtypeStruct((n, d), F32), jax.ShapeDtypeStruct((n, d), F32)),
        grid=(n // tm,),
        in_specs=[pl.BlockSpec((tm, d), row(0)),
                  pl.BlockSpec((tm, d), row(0)),
                  pl.BlockSpec((tm, d), row(0)),
                  pl.BlockSpec((tm, d), row(_COL_GA // d)),
                  pl.BlockSpec((tm, d), row(_COL_GB // d)),
                  pl.BlockSpec((tm, pd), row(0)),
                  full((d, d)), full((d, d)), full((d, d)), full((1, d)),
                  full((1, d)), full((1, d)), full((d, d)), full((pd, d))],
        out_specs=(pl.BlockSpec((tm, d), row(0)), pl.BlockSpec((tm, d), row(0))),
        compiler_params=_params(("parallel",)),
        name="merge_ln1",
    )(x, od, osw, proj, proj, p, wa, wb, wo, bo, g1, b1, wpg, wpp)


def _router_kernel(x_ref, w_ref, b_ref, code_ref, gate_ref, cnt_ref, carry, *, tm):
    @pl.when(pl.program_id(0) == 0)
    def _():
        carry[...] = jnp.zeros(carry.shape, F32)

    logits = jnp.dot(x_ref[...], w_ref[...], preferred_element_type=F32,
                     precision=lax.Precision.HIGHEST) + b_ref[...]
    lane = lax.broadcasted_iota(jnp.int32, logits.shape, 1)
    lanef = lane.astype(F32)
    work = logits
    tops, idxs = [], []
    onehot = jnp.zeros(logits.shape, F32)
    for _ in range(TOP_K):
        m = jnp.max(work, axis=1, keepdims=True)
        idx = jnp.min(jnp.where(work == m, lanef, float(LANES)), axis=1, keepdims=True)
        sel = lanef == idx
        onehot = jnp.where(sel, 1.0, onehot)
        work = jnp.where(sel, -jnp.inf, work)
        tops.append(m)
        idxs.append(idx)
    es = [jnp.exp(t - tops[0]) for t in tops]
    denom = es[0] + es[1] + es[2] + es[3]
    r = lax.broadcasted_iota(jnp.int32, (tm, tm), 0)
    c = lax.broadcasted_iota(jnp.int32, (tm, tm), 1)
    tri = jnp.where(c < r, 1.0, 0.0).astype(BF16)
    before = jnp.dot(tri, onehot.astype(BF16), preferred_element_type=F32) + carry[0:1, :]
    code = jnp.zeros(logits.shape, jnp.int32)
    gate = jnp.zeros(logits.shape, F32)
    for k in range(TOP_K):
        rank = jnp.sum(jnp.where(lanef == idxs[k], before, 0.0), axis=1, keepdims=True)
        ck = (idxs[k] * float(1 << RANK_BITS) + rank).astype(jnp.int32)
        code = jnp.where(lane == k, ck, code)
        gate = jnp.where(lane == k, es[k] / denom, gate)
    code_ref[...] = code
    gate_ref[...] = gate
    carry[0:1, :] = carry[0:1, :] + jnp.sum(onehot, axis=0, keepdims=True)
    cnt_ref[...] = carry[...]


def _router(x1, w, b, *, tm):
    n, d = x1.shape
    kern = functools.partial(_router_kernel, tm=tm)
    return pl.pallas_call(
        kern,
        out_shape=(jax.ShapeDtypeStruct((n, LANES), jnp.int32),
                   jax.ShapeDtypeStruct((n, LANES), F32),
                   jax.ShapeDtypeStruct((8, LANES), F32)),
        grid=(n // tm,),
        in_specs=[pl.BlockSpec((tm, d), lambda i: (i, 0)),
                  pl.BlockSpec((d, LANES), lambda i: (0, 0)),
                  pl.BlockSpec((1, LANES), lambda i: (0, 0))],
        out_specs=(pl.BlockSpec((tm, LANES), lambda i: (i, 0)),
                   pl.BlockSpec((tm, LANES), lambda i: (i, 0)),
                   pl.BlockSpec((8, LANES), lambda i: (0, 0))),
        scratch_shapes=[pltpu.VMEM((8, LANES), F32)],
        compiler_params=_params(("arbitrary",)),
        name="router",
    )(x1, w, b)


def _slot(code, off_ref):
    return off_ref[code >> RANK_BITS] + (code & ((1 << RANK_BITS) - 1))


def _dispatch_kernel(off_ref, codes_ref, x_ref, xs_in_ref, xs_ref, codes_smem, csem, sem, *, rows):
    del xs_in_ref
    i = pl.program_id(0)
    n_assign = rows * TOP_K
    load = pltpu.make_async_copy(codes_ref.at[pl.ds(i * n_assign, n_assign)], codes_smem, csem)
    load.start()
    load.wait()

    def issue(t, carry):
        src = x_ref.at[pl.ds(t, 1)]
        for k in range(TOP_K):
            slot = _slot(codes_smem[t * TOP_K + k], off_ref)
            pltpu.make_async_copy(src, xs_ref.at[pl.ds(slot, 1)], sem).start()
        return carry

    lax.fori_loop(0, rows, issue, 0, unroll=2)
    pltpu.make_async_copy(xs_ref.at[pl.ds(0, n_assign)], xs_ref.at[pl.ds(0, n_assign)], sem).wait()


def _dispatch(off, codes, x1, xs_zero, *, rows):
    n, d = x1.shape
    kern = functools.partial(_dispatch_kernel, rows=rows)
    any_spec = pl.BlockSpec(memory_space=pl.ANY)
    return pl.pallas_call(
        kern,
        out_shape=jax.ShapeDtypeStruct(xs_zero.shape, xs_zero.dtype),
        grid_spec=pltpu.PrefetchScalarGridSpec(
            num_scalar_prefetch=1,
            grid=(n // rows,),
            in_specs=[any_spec, pl.BlockSpec((rows, d), lambda i, off: (i, 0)), any_spec],
            out_specs=any_spec,
            scratch_shapes=[pltpu.SMEM((rows * TOP_K,), jnp.int32),
                            pltpu.SemaphoreType.DMA, pltpu.SemaphoreType.DMA]),
        input_output_aliases={3: 0},
        compiler_params=_dma_params(("arbitrary",)),
        name="moe_dispatch",
    )(off, codes, x1, xs_zero)


def _expert_kernel(be_ref, xs_ref, wgu_ref, bgu_ref, wd_ref, bd_ref, y_ref):
    del be_ref
    dff = wd_ref.shape[1]
    gu = jnp.dot(xs_ref[...].astype(BF16), wgu_ref[0], preferred_element_type=F32) + bgu_ref[0]
    gate = jnp.minimum(gu[:, :dff], SWIGLU_LIMIT)
    up = jnp.clip(gu[:, dff:], -SWIGLU_LIMIT, SWIGLU_LIMIT)
    act = (up + 1.0) * (gate * jax.nn.sigmoid(SWIGLU_ALPHA * gate))
    y_ref[...] = jnp.dot(act.astype(BF16), wd_ref[0], preferred_element_type=F32) + bd_ref[0]


def _experts(block_e, xs, wgu, bgu, wd, bd):
    n_slots, d = xs.shape
    dff = wd.shape[1]
    nb = n_slots // MOE_BLOCK
    return pl.pallas_call(
        _expert_kernel,
        out_shape=jax.ShapeDtypeStruct((n_slots, d), F32),
        grid_spec=pltpu.PrefetchScalarGridSpec(
            num_scalar_prefetch=1,
            grid=(nb,),
            in_specs=[pl.BlockSpec((MOE_BLOCK, d), lambda j, be: (j, 0)),
                      pl.BlockSpec((1, d, 2 * dff), lambda j, be: (be[j], 0, 0)),
                      pl.BlockSpec((1, 1, 2 * dff), lambda j, be: (be[j], 0, 0)),
                      pl.BlockSpec((1, dff, d), lambda j, be: (be[j], 0, 0)),
                      pl.BlockSpec((1, 1, d), lambda j, be: (be[j], 0, 0))],
            out_specs=pl.BlockSpec((MOE_BLOCK, d), lambda j, be: (j, 0))),
        compiler_params=_params(("arbitrary",)),
        name="moe_experts",
    )(block_e, xs, wgu, bgu, wd, bd)


def _combine_kernel(off_ref, codes_ref, gate_ref, r_ref, y_ref, g2_ref, b2_ref,
                    x2_ref, xb_ref, codes_smem, ybuf, csem, sem, *, rows):
    i = pl.program_id(0)
    n_assign = rows * TOP_K
    load = pltpu.make_async_copy(codes_ref.at[pl.ds(i * n_assign, n_assign)], codes_smem, csem)
    load.start()
    load.wait()

    def issue(t, carry):
        for k in range(TOP_K):
            slot = _slot(codes_smem[t * TOP_K + k], off_ref)
            pltpu.make_async_copy(y_ref.at[pl.ds(slot, 1)],
                                  ybuf.at[pl.ds(k * rows + t, 1)], sem).start()
        return carry

    lax.fori_loop(0, rows, issue, 0, unroll=2)
    pltpu.make_async_copy(y_ref.at[pl.ds(0, n_assign)], ybuf, sem).wait()

    gate = gate_ref[...]
    acc = r_ref[...]
    for k in range(TOP_K):
        acc = acc + gate[:, k:k + 1] * ybuf[k * rows:(k + 1) * rows, :]
    x2 = _layer_norm(acc, g2_ref[...], b2_ref[...])
    x2_ref[...] = x2
    xb_ref[...] = x2.astype(BF16)


def _combine(off, codes, gates, r, y, g2, b2, *, rows):
    n, d = r.shape
    kern = functools.partial(_combine_kernel, rows=rows)
    any_spec = pl.BlockSpec(memory_space=pl.ANY)
    return pl.pallas_call(
        kern,
        out_shape=(jax.ShapeDtypeStruct((n, d), F32), jax.ShapeDtypeStruct((n, d), BF16)),
        grid_spec=pltpu.PrefetchScalarGridSpec(
            num_scalar_prefetch=1,
            grid=(n // rows,),
            in_specs=[any_spec,
                      pl.BlockSpec((rows, LANES), lambda i, off: (i, 0)),
                      pl.BlockSpec((rows, d), lambda i, off: (i, 0)),
                      any_spec,
                      pl.BlockSpec((1, d), lambda i, off: (0, 0)),
                      pl.BlockSpec((1, d), lambda i, off: (0, 0))],
            out_specs=(pl.BlockSpec((rows, d), lambda i, off: (i, 0)),
                       pl.BlockSpec((rows, d), lambda i, off: (i, 0))),
            scratch_shapes=[pltpu.SMEM((rows * TOP_K,), jnp.int32),
                            pltpu.VMEM((rows * TOP_K, d), F32),
                            pltpu.SemaphoreType.DMA, pltpu.SemaphoreType.DMA]),
        compiler_params=_dma_params(("arbitrary",)),
        name="moe_combine_ln2",
    )(off, codes, gates, r, y, g2, b2)


def _alibi_slopes(n_heads):
    h = jnp.arange(1, n_heads + 1, dtype=F32)
    return jnp.exp2(-8.0 * h / n_heads)


def _block_size(n, target):
    t = min(n, target)
    while n % t:
        t //= 2
    return t


def kernel(x, p, w_in, b_in, lambda_q1, lambda_k1, lambda_q2, lambda_k2, subln_w, sinks,
           w_br_diff, w_br_swa, w_out, b_out, ln1_g, ln1_b, w_router, b_router,
           w_gate_up, b_gate_up, w_down, b_down, w_ple_gate, w_ple_proj, ln2_g, ln2_b):
    batch, seq, d = x.shape
    depth = w_in.shape[0]
    n = batch * seq
    dn_alpha = (2 * depth) ** 0.25
    n_assign = n * TOP_K
    n_blocks = n_assign // MOE_BLOCK + N_EXPERTS + 1
    n_slots = n_blocks * MOE_BLOCK

    n_in = w_in.shape[2]
    perm = jnp.concatenate([jnp.arange(n_in - 2 * d, n_in), jnp.arange(0, n_in - 2 * d)])
    log2e = math.log2(math.e)
    diff_slopes = _alibi_slopes(DIFF_HEADS) * log2e
    swa_slopes = _alibi_slopes(SWA_HEADS)

    tm_lin = _block_size(n, 1024)
    tq_diff = _block_size(seq, 512)
    tq_swa = _block_size(seq, 256)
    tm_merge = _block_size(n, 512)
    tm_router = _block_size(n, 512)
    rows_moe = _block_size(n, 256)

    xf = x.reshape(n, d)
    xb = xf.astype(BF16)
    for i in range(depth):
        lam_init = 0.8 - 0.6 * math.exp(-0.3 * i)
        col_scale = jnp.ones((n_in,), F32).at[_COL_DQ:_COL_DK].set(HEAD_DIM ** -0.5 * log2e)
        w_in_b = (jnp.take(w_in[i], perm, axis=1) * col_scale).astype(BF16)
        b_in_p = (jnp.take(b_in[i], perm) * col_scale)[None, :]
        proj = _linear(xb, w_in_b, b_in_p, tm=tm_lin, tn=1280)

        od = _diff_attention(proj, diff_slopes, _diff_query_aug(diff_slopes, tq_diff),
                             lambda_q1[i][None, :], lambda_k1[i][None, :],
                             lambda_q2[i][None, :], lambda_k2[i][None, :],
                             subln_w[i][None, :], batch=batch, seq=seq,
                             lam_init=lam_init, tq=tq_diff)
        osw = _swa_attention(proj, swa_slopes, sinks[i].astype(F32), batch=batch, seq=seq, tq=tq_swa)

        x1, r = _merge(xf, od, osw, proj, p[i].reshape(n, -1),
                       w_br_diff[i].astype(BF16), w_br_swa[i].astype(BF16),
                       w_out[i].astype(BF16), b_out[i][None, :],
                       ln1_g[i][None, :], ln1_b[i][None, :],
                       w_ple_gate[i].astype(BF16), w_ple_proj[i].astype(BF16),
                       dn_alpha=dn_alpha, tm=tm_merge)

        wr = jnp.zeros((d, LANES), F32).at[:, :N_EXPERTS].set(w_router[i])
        br = jnp.full((1, LANES), NEG_BIG, F32).at[0, :N_EXPERTS].set(b_router[i])
        code, gates, cnt = _router(x1, wr, br, tm=tm_router)

        counts = cnt[0, :N_EXPERTS].astype(jnp.int32)
        padded = (counts + MOE_BLOCK - 1) // MOE_BLOCK * MOE_BLOCK
        padded_end = jnp.cumsum(padded)
        off = (padded_end - padded).astype(jnp.int32)
        block_start = jnp.arange(n_blocks, dtype=jnp.int32) * MOE_BLOCK
        block_e = jnp.minimum(
            jnp.sum((block_start[:, None] >= padded_end[None, :]).astype(jnp.int32), axis=1),
            N_EXPERTS - 1)
        codes = code[:, :TOP_K].reshape(n_assign)

        xs = _dispatch(off, codes, x1, jnp.zeros((n_slots, d), F32), rows=rows_moe)
        y = _experts(block_e, xs, w_gate_up[i].astype(BF16), b_gate_up[i][:, None, :],
                     w_down[i].astype(BF16), b_down[i][:, None, :])
        xf, xb = _combine(off, codes, gates, r, y, ln2_g[i][None, :], ln2_b[i][None, :],
                          rows=rows_moe)
    return xf.reshape(batch, seq, d)
```

```python
import functools
import math

import jax
import jax.numpy as jnp
from jax import lax
from jax.experimental import pallas as pl
from jax.experimental.pallas import tpu as pltpu

F32 = jnp.float32
BF16 = jnp.bfloat16

HEAD_DIM = 64
DIFF_HEADS = 8
SWA_HEADS = 16
SWA_KV_HEADS = 2
SWA_GROUP = SWA_HEADS // SWA_KV_HEADS
WINDOW = 128
N_EXPERTS = 32
TOP_K = 4
MOE_BLOCK = 512
SWIGLU_LIMIT = 7.0
SWIGLU_ALPHA = 1.702
LN_EPS = 1e-5
RMS_EPS = 1e-5
NEG_BIG = -1e30

LANES = 128
VMEM_LIMIT = 56 * 1024 * 1024
RANK_BITS = 16

_COL_GA, _COL_GB, _COL_DQ, _COL_DK, _COL_DV, _COL_SQ, _COL_SK, _COL_SV = (
    0, 1024, 2048, 3072, 4096, 5120, 6144, 6272)


def _params(semantics):
    return pltpu.CompilerParams(dimension_semantics=semantics,
                                vmem_limit_bytes=VMEM_LIMIT)


def _dma_params(semantics):
    return pltpu.CompilerParams(dimension_semantics=semantics,
                                vmem_limit_bytes=VMEM_LIMIT,
                                disable_bounds_checks=True)


def _linear_kernel(x_ref, w_ref, b_ref, o_ref):
    acc = jnp.dot(x_ref[...], w_ref[...], preferred_element_type=F32)
    o_ref[...] = (acc + b_ref[...]).astype(o_ref.dtype)


def _linear(x, w, b, *, tm, tn):
    n, k = x.shape
    nout = w.shape[1]
    return pl.pallas_call(
        _linear_kernel,
        out_shape=jax.ShapeDtypeStruct((n, nout), BF16),
        grid=(n // tm, nout // tn),
        in_specs=[pl.BlockSpec((tm, k), lambda i, j: (i, 0)),
                  pl.BlockSpec((k, tn), lambda i, j: (0, j)),
                  pl.BlockSpec((1, tn), lambda i, j: (0, j))],
        out_specs=pl.BlockSpec((tm, tn), lambda i, j: (i, j)),
        compiler_params=_params(("parallel", "arbitrary")),
        name="in_proj",
    )(x, w, b)


N_AUG = 6
STRIP = 256
DV = 2 * HEAD_DIM
ONES_ROWS = 16


def _key_aug(tk, first_lane):
    r = lax.broadcasted_iota(jnp.int32, (tk, LANES), 0)
    lane = lax.broadcasted_iota(jnp.int32, (tk, LANES), 1) - first_lane
    hi = ((r >> 7) << 7).astype(F32)
    lo = (r & 127).astype(F32)
    return jnp.where((lane >= 0) & (lane < 3), hi,
                     jnp.where((lane >= 3) & (lane < N_AUG), lo, 0.0))


def _diff_attn_kernel(slopes_ref, lq1_ref, lk1_ref, lq2_ref, lk2_ref, qaug_ref,
                      q_ref, k_ref, v_ref, w_ref, o_ref,
                      k1a_sc, k2a_sc, vt_sc, a1, a2, st, sa1, sa2, sb1, sb2, *, tq, lam_init):
    h = pl.program_id(1)
    qi = pl.program_id(2)
    slope = slopes_ref[h]
    n_chunks = k1a_sc.shape[0]

    @pl.when(qi == 0)
    def _():
        lane = lax.broadcasted_iota(jnp.int32, (tq, LANES), 1)
        aug1 = _key_aug(tq, HEAD_DIM).astype(BF16)
        aug2 = _key_aug(tq, 0).astype(BF16)

        def build(c, carry):
            rows = pl.ds(pl.multiple_of(c * tq, tq), tq)
            k = k_ref[rows, :]
            k1a_sc[c] = jnp.where(lane < HEAD_DIM, k, aug1)
            k2a_sc[c] = jnp.where(lane >= HEAD_DIM, k, aug2)
            vt_sc[c, 0:DV, :] = v_ref[rows, :].astype(F32).T.astype(BF16)
            vt_sc[c, DV:DV + ONES_ROWS, :] = jnp.where(
                lax.broadcasted_iota(jnp.int32, (ONES_ROWS, tq), 0) == 0, 1.0, 0.0).astype(BF16)
            return carry

        lax.fori_loop(0, n_chunks, build, 0)

    qt = q_ref[...].astype(F32).T
    row = lax.broadcasted_iota(jnp.int32, qt.shape, 0)
    qt1 = jnp.where(row < HEAD_DIM, qt, qaug_ref[0, 0]).astype(BF16)
    qt2 = jnp.where(row >= HEAD_DIM, qt, qaug_ref[0, 1]).astype(BF16)
    a1[...] = jnp.zeros(a1.shape, F32)
    a2[...] = jnp.zeros(a2.shape, F32)

    M1, M2, XA1, XA2 = range(4)
    for r_ in (M1, M2):
        st[r_:r_ + 1, :] = jnp.full((1, tq), NEG_BIG, F32)

    maps = ((k1a_sc, qt1, a1, M1, XA1), (k2a_sc, qt2, a2, M2, XA2))
    tc = min(tq, STRIP)
    pieces = [(mp, slice(h * tc, (h + 1) * tc)) for h in range(tq // tc) for mp in range(2)]

    def scores(j, mp, cols, dst, masked=False):
        s = jnp.dot(maps[mp][0][j], maps[mp][1][:, cols], preferred_element_type=F32)
        if masked:
            krow = lax.broadcasted_iota(jnp.int32, s.shape, 0)
            qcol = lax.broadcasted_iota(jnp.int32, s.shape, 1) + cols.start
            s = jnp.where(krow <= qcol, s, NEG_BIG)
        dst[mp][:, cols] = s
        return jnp.max(s, axis=0, keepdims=True)

    def accumulate(j, mp, cols, src, mx):
        _, _, a_sc, mr, _ = maps[mp]
        c = slope * jnp.full((1, tc), (j - qi) * tq, jnp.int32).astype(F32)
        m = st[mr:mr + 1, cols]
        m_new = jnp.maximum(m, mx + c)
        alpha = jnp.exp2(m - m_new)
        p = jnp.exp2(src[mp][:, cols] - (m_new - c))
        st[mr:mr + 1, cols] = m_new
        a_sc[:, cols] = alpha * a_sc[:, cols] + jnp.dot(vt_sc[j], p.astype(BF16),
                                                        preferred_element_type=F32)

    bufs = ((sa1, sa2), (sb1, sb2))

    def park(maxima):
        for (mp, cols), x in zip(pieces, maxima):
            xr = maps[mp][4]
            st[xr:xr + 1, cols] = x

    def parked():
        return [st[maps[mp][4]:maps[mp][4] + 1, cols] for mp, cols in pieces]

    def step(s_blk, s_dst, p_blk, p_src, p_max, masked=False):
        out = []
        for (mp, cols), x in zip(pieces, p_max):
            out.append(scores(s_blk, mp, cols, s_dst, masked))
            accumulate(p_blk, mp, cols, p_src, x)
        return out

    @pl.when(qi == 0)
    def _():
        xd = [scores(0, mp, cols, bufs[0], masked=True) for mp, cols in pieces]
        for (mp, cols), x in zip(pieces, xd):
            accumulate(0, mp, cols, bufs[0], x)

    @pl.when(qi > 0)
    def _():
        park([scores(0, mp, cols, bufs[0]) for mp, cols in pieces])
        n_loop = qi - 1
        n_quad = n_loop // 4

        def quad(i, carry):
            j = 4 * i
            x = parked()
            for t in range(4):
                x = step(j + t + 1, bufs[(t + 1) % 2], j + t, bufs[t % 2], x)
            park(x)
            return carry

        lax.fori_loop(0, n_quad, quad, 0)
        j = 4 * n_quad
        rem = n_loop - j
        for k in range(4):
            @pl.when(rem == k)
            def _(k=k):
                x = parked()
                for t in range(k):
                    x = step(j + t + 1, bufs[(t + 1) % 2], j + t, bufs[t % 2], x)
                cur, nxt = bufs[k % 2], bufs[(k + 1) % 2]
                xd = step(qi, nxt, qi - 1, cur, x, masked=True)
                for (mp, cols), xk in zip(pieces, xd):
                    accumulate(qi, mp, cols, nxt, xk)

    lam = (jnp.exp(jnp.sum(lq1_ref[...] * lk1_ref[...], axis=1, keepdims=True))
           - jnp.exp(jnp.sum(lq2_ref[...] * lk2_ref[...], axis=1, keepdims=True))
           + lam_init)
    o1 = a1[0:DV, :] / a1[DV:DV + 1, :]
    o2 = a2[0:DV, :] / a2[DV:DV + 1, :]
    o = (o1 - lam * o2).T
    y = o * lax.rsqrt(jnp.mean(jnp.square(o), axis=1, keepdims=True) + RMS_EPS)
    y = (y * w_ref[...]) * (1.0 - lam_init)
    o_ref[...] = y.astype(o_ref.dtype)


def _diff_attention(proj, slopes_l2, qaug, lq1, lk1, lq2, lk2, subln_w, *, batch, seq, lam_init, tq):
    n = proj.shape[0]
    nq = seq // tq
    kern = functools.partial(_diff_attn_kernel, tq=tq, lam_init=lam_init)
    vec = pl.BlockSpec((1, HEAD_DIM), lambda b, h, i: (0, 0))
    cq, ck, cv = _COL_DQ // LANES, _COL_DK // LANES, _COL_DV // LANES
    return pl.pallas_call(
        kern,
        out_shape=jax.ShapeDtypeStruct((n, DIFF_HEADS * 2 * HEAD_DIM), BF16),
        grid=(batch, DIFF_HEADS, nq),
        in_specs=[pl.BlockSpec(memory_space=pltpu.SMEM),
                  vec, vec, vec, vec,
                  pl.BlockSpec((1, 2, LANES, tq), lambda b, h, i: (h, 0, 0, 0)),
                  pl.BlockSpec((tq, LANES), lambda b, h, i: (b * nq + i, cq + h)),
                  pl.BlockSpec((seq, LANES), lambda b, h, i: (b, ck + h)),
                  pl.BlockSpec((seq, LANES), lambda b, h, i: (b, cv + h)),
                  pl.BlockSpec((1, 2 * HEAD_DIM), lambda b, h, i: (0, 0))],
        out_specs=pl.BlockSpec((tq, LANES), lambda b, h, i: (b * nq + i, h)),
        scratch_shapes=[pltpu.VMEM((nq, tq, LANES), BF16), pltpu.VMEM((nq, tq, LANES), BF16),
                        pltpu.VMEM((nq, DV + ONES_ROWS, tq), BF16),
                        pltpu.VMEM((DV + ONES_ROWS, tq), F32), pltpu.VMEM((DV + ONES_ROWS, tq), F32),
                        pltpu.VMEM((8, tq), F32)] + [pltpu.VMEM((tq, tq), F32)] * 4,
        compiler_params=_params(("arbitrary", "arbitrary", "arbitrary")),
        name="diff_attn",
    )(slopes_l2, lq1, lk1, lq2, lk2, qaug, proj, proj, proj, subln_w)


def _diff_query_aug(slopes_l2, tq):
    s0 = slopes_l2.astype(BF16).astype(F32)
    s1 = (slopes_l2 - s0).astype(BF16).astype(F32)
    s2 = (slopes_l2 - s0 - s1).astype(BF16).astype(F32)
    parts = jnp.stack([s0, s1, s2, s0, s1, s2], axis=1)
    n_heads = slopes_l2.shape[0]
    cols = jnp.zeros((n_heads, 2, LANES), F32)
    cols = cols.at[:, 0, HEAD_DIM:HEAD_DIM + N_AUG].set(parts)
    cols = cols.at[:, 1, 0:N_AUG].set(parts)
    return jnp.broadcast_to(cols[:, :, :, None], (n_heads, 2, LANES, tq))


def _swa_kernel(slopes_ref, sinks_ref, q_ref, kp_ref, kc_ref, vp_ref, vc_ref, o_ref, *, tq):
    qi = pl.program_id(1)
    scale = HEAD_DIM ** -0.5
    kcat = jnp.concatenate([kp_ref[...], kc_ref[...]], axis=0)
    vcat = jnp.concatenate([vp_ref[...], vc_ref[...]], axis=0)
    row = lax.broadcasted_iota(jnp.int32, (WINDOW, 2 * WINDOW), 0)
    col = lax.broadcasted_iota(jnp.int32, (WINDOW, 2 * WINDOW), 1)
    dist = row + WINDOW - col
    valid = (dist >= 0) & (dist < WINDOW)
    distf = dist.astype(F32)
    valid_first = valid & ((col >= WINDOW) | (qi > 0))
    for j in range(tq // WINDOW):
        kj = kcat[j * WINDOW:(j + 2) * WINDOW]
        vj = vcat[j * WINDOW:(j + 2) * WINDOW]
        qj = q_ref[j * WINDOW:(j + 1) * WINDOW, :] * scale
        vmask = valid_first if j == 0 else valid
        for hk in range(SWA_KV_HEADS):
            kk = kj[:, hk * HEAD_DIM:(hk + 1) * HEAD_DIM]
            vv = vj[:, hk * HEAD_DIM:(hk + 1) * HEAD_DIM]
            for g in range(SWA_GROUP):
                hq = hk * SWA_GROUP + g
                qh = qj[:, hq * HEAD_DIM:(hq + 1) * HEAD_DIM]
                s = lax.dot_general(qh, kk, (((1,), (1,)), ((), ())),
                                    preferred_element_type=F32)
                s = s - slopes_ref[hq] * distf
                s = jnp.where(vmask, s, NEG_BIG)
                sink = sinks_ref[hq]
                m = jnp.maximum(jnp.max(s, axis=1, keepdims=True), sink)
                e = jnp.exp(s - m)
                denom = jnp.sum(e, axis=1, keepdims=True) + jnp.exp(sink - m)
                p = (e / denom).astype(BF16)
                o = jnp.dot(p, vv, preferred_element_type=F32)
                o_ref[j * WINDOW:(j + 1) * WINDOW,
                      hq * HEAD_DIM:(hq + 1) * HEAD_DIM] = o.astype(o_ref.dtype)


def _swa_attention(proj, slopes, sinks, *, batch, seq, tq):
    n = proj.shape[0]
    nq = seq // tq
    sub = tq // WINDOW
    nwin = seq // WINDOW
    kern = functools.partial(_swa_kernel, tq=tq)
    cq = _COL_SQ // (SWA_HEADS * HEAD_DIM)
    ck, cv = _COL_SK // LANES, _COL_SV // LANES
    prev = lambda c: (lambda b, i: (b * nwin + jnp.maximum(i * sub - 1, 0), c))
    cur = lambda c: (lambda b, i: (b * nq + i, c))
    smem = pl.BlockSpec(memory_space=pltpu.SMEM)
    return pl.pallas_call(
        kern,
        out_shape=jax.ShapeDtypeStruct((n, SWA_HEADS * HEAD_DIM), BF16),
        grid=(batch, nq),
        in_specs=[smem, smem,
                  pl.BlockSpec((tq, SWA_HEADS * HEAD_DIM), cur(cq)),
                  pl.BlockSpec((WINDOW, LANES), prev(ck)),
                  pl.BlockSpec((tq, LANES), cur(ck)),
                  pl.BlockSpec((WINDOW, LANES), prev(cv)),
                  pl.BlockSpec((tq, LANES), cur(cv))],
        out_specs=pl.BlockSpec((tq, SWA_HEADS * HEAD_DIM), lambda b, i: (b * nq + i, 0)),
        compiler_params=_params(("parallel", "arbitrary")),
        name="swa_attn",
    )(slopes, sinks, proj, proj, proj, proj, proj)


def _layer_norm(y, g, b):
    mu = jnp.mean(y, axis=1, keepdims=True)
    var = jnp.mean(jnp.square(y - mu), axis=1, keepdims=True)
    return (y - mu) * lax.rsqrt(var + LN_EPS) * g + b


def _merge_kernel(x_ref, od_ref, os_ref, ga_ref, gb_ref, p_ref,
                  wa_ref, wb_ref, wo_ref, bo_ref, g1_ref, b1_ref, wpg_ref, wpp_ref,
                  x1_ref, r_ref, *, dn_alpha):
    a = jnp.dot(od_ref[...], wa_ref[...], preferred_element_type=F32)
    b = jnp.dot(os_ref[...], wb_ref[...], preferred_element_type=F32)
    merged = (jax.nn.sigmoid(ga_ref[...].astype(F32)) * a
              + jax.nn.sigmoid(gb_ref[...].astype(F32)) * b)
    mix = jnp.dot(merged.astype(BF16), wo_ref[...], preferred_element_type=F32) + bo_ref[...]
    x1 = _layer_norm(dn_alpha * x_ref[...] + mix, g1_ref[...], b1_ref[...])
    x1_ref[...] = x1
    gate = jax.nn.sigmoid(jnp.dot(x1.astype(BF16), wpg_ref[...], preferred_element_type=F32))
    ple = gate * jnp.dot(p_ref[...].astype(BF16), wpp_ref[...], preferred_element_type=F32)
    r_ref[...] = dn_alpha * x1 + ple


def _merge(x, od, osw, proj, p, wa, wb, wo, bo, g1, b1, wpg, wpp, *, dn_alpha, tm):
    n, d = x.shape
    pd = p.shape[1]
    row = lambda c: (lambda i: (i, c))
    full = lambda shape: pl.BlockSpec(shape, lambda i: (0, 0))
    kern = functools.partial(_merge_kernel, dn_alpha=dn_alpha)
    return pl.pallas_call(
        kern,
        out_shape=(jax.ShapeDtypeStruct((n, d), F32), jax.ShapeDtypeStruct((n, d), F32)),
        grid=(n // tm,),
        in_specs=[pl.BlockSpec((tm, d), row(0)),
                  pl.BlockSpec((tm, d), row(0)),
                  pl.BlockSpec((tm, d), row(0)),
                  pl.BlockSpec((tm, d), row(_COL_GA // d)),
                  pl.BlockSpec((tm, d), row(_COL_GB // d)),
                  pl.BlockSpec((tm, pd), row(0)),
                  full((d, d)), full((d, d)), full((d, d)), full((1, d)),
                  full((1, d)), full((1, d)), full((d, d)), full((pd, d))],
        out_specs=(pl.BlockSpec((tm, d), row(0)), pl.BlockSpec((tm, d), row(0))),
        compiler_params=_params(("parallel",)),
        name="merge_ln1",
    )(x, od, osw, proj, proj, p, wa, wb, wo, bo, g1, b1, wpg, wpp)


def _router_kernel(x_ref, w_ref, b_ref, code_ref, gate_ref, cnt_ref, carry, *, tm):
    @pl.when(pl.program_id(0) == 0)
    def _():
        carry[...] = jnp.zeros(carry.shape, F32)

    logits = jnp.dot(x_ref[...], w_ref[...], preferred_element_type=F32,
                     precision=lax.Precision.HIGHEST) + b_ref[...]
    lane = lax.broadcasted_iota(jnp.int32, logits.shape, 1)
    lanef = lane.astype(F32)
    work = logits
    tops, idxs = [], []
    onehot = jnp.zeros(logits.shape, F32)
    for _ in range(TOP_K):
        m = jnp.max(work, axis=1, keepdims=True)
        idx = jnp.min(jnp.where(work == m, lanef, float(LANES)), axis=1, keepdims=True)
        sel = lanef == idx
        onehot = jnp.where(sel, 1.0, onehot)
        work = jnp.where(sel, -jnp.inf, work)
        tops.append(m)
        idxs.append(idx)
    es = [jnp.exp(t - tops[0]) for t in tops]
    denom = es[0] + es[1] + es[2] + es[3]
    r = lax.broadcasted_iota(jnp.int32, (tm, tm), 0)
    c = lax.broadcasted_iota(jnp.int32, (tm, tm), 1)
    tri = jnp.where(c < r, 1.0, 0.0).astype(BF16)
    before = jnp.dot(tri, onehot.astype(BF16), preferred_element_type=F32) + carry[0:1, :]
    code = jnp.zeros(logits.shape, jnp.int32)
    gate = jnp.zeros(logits.shape, F32)
    for k in range(TOP_K):
        rank = jnp.sum(jnp.where(lanef == idxs[k], before, 0.0), axis=1, keepdims=True)
        ck = (idxs[k] * float(1 << RANK_BITS) + rank).astype(jnp.int32)
        code = jnp.where(lane == k, ck, code)
        gate = jnp.where(lane == k, es[k] / denom, gate)
    code_ref[...] = code
    gate_ref[...] = gate
    carry[0:1, :] = carry[0:1, :] + jnp.sum(onehot, axis=0, keepdims=True)
    cnt_ref[...] = carry[...]


def _router(x1, w, b, *, tm):
    n, d = x1.shape
    kern = functools.partial(_router_kernel, tm=tm)
    return pl.pallas_call(
        kern,
        out_shape=(jax.ShapeDtypeStruct((n, LANES), jnp.int32),
                   jax.ShapeDtypeStruct((n, LANES), F32),
                   jax.ShapeDtypeStruct((8, LANES), F32)),
        grid=(n // tm,),
        in_specs=[pl.BlockSpec((tm, d), lambda i: (i, 0)),
                  pl.BlockSpec((d, LANES), lambda i: (0, 0)),
                  pl.BlockSpec((1, LANES), lambda i: (0, 0))],
        out_specs=(pl.BlockSpec((tm, LANES), lambda i: (i, 0)),
                   pl.BlockSpec((tm, LANES), lambda i: (i, 0)),
                   pl.BlockSpec((8, LANES), lambda i: (0, 0))),
        scratch_shapes=[pltpu.VMEM((8, LANES), F32)],
        compiler_params=_params(("arbitrary",)),
        name="router",
    )(x1, w, b)


def _slot(code, off_ref):
    return off_ref[code >> RANK_BITS] + (code & ((1 << RANK_BITS) - 1))


def _dispatch_kernel(off_ref, tail_ref, codes_ref, x_ref, xs_ref, codes_smem, zeros, csem, sem, zsem,
                     *, rows):
    i = pl.program_id(0)
    n_steps = pl.num_programs(0)
    n_assign = rows * TOP_K

    def codes_load(step):
        return pltpu.make_async_copy(codes_ref.at[pl.ds(step * n_assign, n_assign)],
                                     codes_smem, csem)

    @pl.when(i == 0)
    def _():
        codes_load(0).start()
        zeros[...] = jnp.zeros(zeros.shape, zeros.dtype)

        def fill(e):
            first = pl.multiple_of(tail_ref[e], MOE_BLOCK)
            return pltpu.make_async_copy(zeros, xs_ref.at[pl.ds(first, MOE_BLOCK)], zsem)

        def start(e, carry):
            @pl.when(tail_ref[e] >= 0)
            def _():
                fill(e).start()
            return carry

        def finish(e, carry):
            @pl.when(tail_ref[e] >= 0)
            def _():
                fill(e).wait()
            return carry

        def unused(b):
            first = pl.multiple_of(b * MOE_BLOCK, MOE_BLOCK)
            return pltpu.make_async_copy(zeros, xs_ref.at[pl.ds(first, MOE_BLOCK)], zsem)

        n_used = tail_ref[N_EXPERTS]
        n_blocks = xs_ref.shape[0] // MOE_BLOCK
        lax.fori_loop(0, N_EXPERTS, start, 0)
        lax.fori_loop(n_used, n_blocks, lambda b, c: (unused(b).start(), c)[1], 0)
        lax.fori_loop(0, N_EXPERTS, finish, 0)
        lax.fori_loop(n_used, n_blocks, lambda b, c: (unused(b).wait(), c)[1], 0)

    codes_load(i).wait()

    def issue(t, carry):
        src = x_ref.at[pl.ds(t, 1)]
        for k in range(TOP_K):
            slot = _slot(codes_smem[t * TOP_K + k], off_ref)
            pltpu.make_async_copy(src, xs_ref.at[pl.ds(slot, 1)], sem).start()
        return carry

    lax.fori_loop(0, rows, issue, 0, unroll=2)

    @pl.when(i + 1 < n_steps)
    def _():
        codes_load(i + 1).start()

    pltpu.make_async_copy(xs_ref.at[pl.ds(0, n_assign)], xs_ref.at[pl.ds(0, n_assign)], sem).wait()


def _dispatch(off, tail, codes, x1, n_slots, *, rows):
    n, d = x1.shape
    kern = functools.partial(_dispatch_kernel, rows=rows)
    any_spec = pl.BlockSpec(memory_space=pl.ANY)
    return pl.pallas_call(
        kern,
        out_shape=jax.ShapeDtypeStruct((n_slots, d), x1.dtype),
        grid_spec=pltpu.PrefetchScalarGridSpec(
            num_scalar_prefetch=2,
            grid=(n // rows,),
            in_specs=[any_spec, pl.BlockSpec((rows, d), lambda i, off, tail: (i, 0))],
            out_specs=any_spec,
            scratch_shapes=[pltpu.SMEM((rows * TOP_K,), jnp.int32),
                            pltpu.VMEM((MOE_BLOCK, d), x1.dtype),
                            pltpu.SemaphoreType.DMA, pltpu.SemaphoreType.DMA,
                            pltpu.SemaphoreType.DMA]),
        compiler_params=_dma_params(("arbitrary",)),
        name="moe_dispatch",
    )(off, tail, codes, x1)


def _expert_kernel(be_ref, nu_ref, xs_ref, wgu_ref, bgu_ref, wd_ref, bd_ref, y_ref, wgu_b, wd_b):
    j = pl.program_id(0)
    dff = wd_ref.shape[2]

    @pl.when(j < nu_ref[0])
    def _():
        @pl.when((j == 0) | (be_ref[j] != be_ref[jnp.maximum(j - 1, 0)]))
        def _():
            wgu_b[...] = wgu_ref[0, 0].astype(BF16)
            wd_b[...] = wd_ref[0, 0].astype(BF16)

        gu = jnp.dot(xs_ref[...].astype(BF16), wgu_b[...], preferred_element_type=F32) + bgu_ref[0, 0]
        gate = jnp.minimum(gu[:, :dff], SWIGLU_LIMIT)
        up = jnp.clip(gu[:, dff:], -SWIGLU_LIMIT, SWIGLU_LIMIT)
        act = (up + 1.0) * (gate * jax.nn.sigmoid(SWIGLU_ALPHA * gate))
        y_ref[...] = jnp.dot(act.astype(BF16), wd_b[...], preferred_element_type=F32) + bd_ref[0, 0]

    @pl.when(j >= nu_ref[0])
    def _():
        y_ref[...] = jnp.zeros(y_ref.shape, y_ref.dtype)


def _experts(block_e, n_used, xs, wgu, bgu, wd, bd, *, layer):
    n_slots, d = xs.shape
    dff = wd.shape[2]
    nb = n_slots // MOE_BLOCK
    blk = lambda j, be, nu: jnp.minimum(j, nu[0] - 1)
    row = lambda j, be, nu: (blk(j, be, nu), 0)
    wsel = lambda j, be, nu: (layer, be[blk(j, be, nu)], 0, 0)
    return pl.pallas_call(
        _expert_kernel,
        out_shape=jax.ShapeDtypeStruct((n_slots, d), F32),
        grid_spec=pltpu.PrefetchScalarGridSpec(
            num_scalar_prefetch=2,
            grid=(nb,),
            in_specs=[pl.BlockSpec((MOE_BLOCK, d), row),
                      pl.BlockSpec((1, 1, d, 2 * dff), wsel),
                      pl.BlockSpec((1, 1, 1, 2 * dff), wsel),
                      pl.BlockSpec((1, 1, dff, d), wsel),
                      pl.BlockSpec((1, 1, 1, d), wsel)],
            out_specs=pl.BlockSpec((MOE_BLOCK, d), lambda j, be, nu: (j, 0)),
            scratch_shapes=[pltpu.VMEM((d, 2 * dff), BF16), pltpu.VMEM((dff, d), BF16)]),
        compiler_params=_params(("arbitrary",)),
        name="moe_experts",
    )(block_e, n_used, xs, wgu, bgu, wd, bd)


def _combine_kernel(off_ref, codes_ref, gate_ref, r_ref, y_ref, g2_ref, b2_ref,
                    x2_ref, xb_ref, codes_smem, ybuf0, ybuf1, csem, sem0, sem1, *, rows):
    i = pl.program_id(0)
    n_steps = pl.num_programs(0)
    n_assign = rows * TOP_K

    def codes_load(step):
        return pltpu.make_async_copy(codes_ref.at[pl.ds(step * n_assign, n_assign)],
                                     codes_smem, csem)

    def gather(step, ybuf, sem):
        def issue(t, carry):
            for k in range(TOP_K):
                slot = _slot(codes_smem[t * TOP_K + k], off_ref)
                pltpu.make_async_copy(y_ref.at[pl.ds(slot, 1)],
                                      ybuf.at[pl.ds(k * rows + t, 1)], sem).start()
            return carry

        lax.fori_loop(0, rows, issue, 0, unroll=2)

        @pl.when(step + 1 < n_steps)
        def _():
            codes_load(step + 1).start()

    @pl.when(i == 0)
    def _():
        first = codes_load(0)
        first.start()
        first.wait()
        gather(0, ybuf0, sem0)

    def step(ybuf, sem, ybuf_next, sem_next):
        @pl.when(i + 1 < n_steps)
        def _():
            codes_load(i + 1).wait()
            gather(i + 1, ybuf_next, sem_next)

        pltpu.make_async_copy(y_ref.at[pl.ds(0, n_assign)], ybuf, sem).wait()
        gate = gate_ref[...]
        acc = r_ref[...]
        for k in range(TOP_K):
            acc = acc + gate[:, k:k + 1] * ybuf[k * rows:(k + 1) * rows, :]
        x2 = _layer_norm(acc, g2_ref[...], b2_ref[...])
        x2_ref[...] = x2
        xb_ref[...] = x2.astype(BF16)

    @pl.when(i % 2 == 0)
    def _():
        step(ybuf0, sem0, ybuf1, sem1)

    @pl.when(i % 2 == 1)
    def _():
        step(ybuf1, sem1, ybuf0, sem0)


def _combine(off, codes, gates, r, y, g2, b2, *, rows):
    n, d = r.shape
    kern = functools.partial(_combine_kernel, rows=rows)
    any_spec = pl.BlockSpec(memory_space=pl.ANY)
    return pl.pallas_call(
        kern,
        out_shape=(jax.ShapeDtypeStruct((n, d), F32), jax.ShapeDtypeStruct((n, d), BF16)),
        grid_spec=pltpu.PrefetchScalarGridSpec(
            num_scalar_prefetch=1,
            grid=(n // rows,),
            in_specs=[any_spec,
                      pl.BlockSpec((rows, LANES), lambda i, off: (i, 0)),
                      pl.BlockSpec((rows, d), lambda i, off: (i, 0)),
                      any_spec,
                      pl.BlockSpec((1, d), lambda i, off: (0, 0)),
                      pl.BlockSpec((1, d), lambda i, off: (0, 0))],
            out_specs=(pl.BlockSpec((rows, d), lambda i, off: (i, 0)),
                       pl.BlockSpec((rows, d), lambda i, off: (i, 0))),
            scratch_shapes=[pltpu.SMEM((rows * TOP_K,), jnp.int32),
                            pltpu.VMEM((rows * TOP_K, d), F32), pltpu.VMEM((rows * TOP_K, d), F32),
                            pltpu.SemaphoreType.DMA, pltpu.SemaphoreType.DMA,
                            pltpu.SemaphoreType.DMA]),
        compiler_params=_dma_params(("arbitrary",)),
        name="moe_combine_ln2",
    )(off, codes, gates, r, y, g2, b2)


def _alibi_slopes(n_heads):
    h = jnp.arange(1, n_heads + 1, dtype=F32)
    return jnp.exp2(-8.0 * h / n_heads)


def _block_size(n, target):
    t = min(n, target)
    while n % t:
        t //= 2
    return t


def kernel(x, p, w_in, b_in, lambda_q1, lambda_k1, lambda_q2, lambda_k2, subln_w, sinks,
           w_br_diff, w_br_swa, w_out, b_out, ln1_g, ln1_b, w_router, b_router,
           w_gate_up, b_gate_up, w_down, b_down, w_ple_gate, w_ple_proj, ln2_g, ln2_b):
    batch, seq, d = x.shape
    depth = w_in.shape[0]
    n = batch * seq
    dn_alpha = (2 * depth) ** 0.25
    n_assign = n * TOP_K
    n_blocks = n_assign // MOE_BLOCK + N_EXPERTS + 1
    n_slots = n_blocks * MOE_BLOCK

    n_in = w_in.shape[2]
    perm = jnp.concatenate([jnp.arange(n_in - 2 * d, n_in), jnp.arange(0, n_in - 2 * d)])
    log2e = math.log2(math.e)
    diff_slopes = _alibi_slopes(DIFF_HEADS) * log2e
    swa_slopes = _alibi_slopes(SWA_HEADS)

    tm_lin = _block_size(n, 1024)
    tq_diff = _block_size(seq, 512)
    tq_swa = _block_size(seq, 256)
    tm_merge = _block_size(n, 512)
    tm_router = _block_size(n, 512)
    rows_moe = _block_size(n, 256)

    xf = x.reshape(n, d)
    xb = xf.astype(BF16)
    for i in range(depth):
        lam_init = 0.8 - 0.6 * math.exp(-0.3 * i)
        col_scale = jnp.ones((n_in,), F32).at[_COL_DQ:_COL_DK].set(HEAD_DIM ** -0.5 * log2e)
        w_in_b = (jnp.take(w_in[i], perm, axis=1) * col_scale).astype(BF16)
        b_in_p = (jnp.take(b_in[i], perm) * col_scale)[None, :]
        proj = _linear(xb, w_in_b, b_in_p, tm=tm_lin, tn=1280)

        od = _diff_attention(proj, diff_slopes, _diff_query_aug(diff_slopes, tq_diff),
                             lambda_q1[i][None, :], lambda_k1[i][None, :],
                             lambda_q2[i][None, :], lambda_k2[i][None, :],
                             subln_w[i][None, :], batch=batch, seq=seq,
                             lam_init=lam_init, tq=tq_diff)
        osw = _swa_attention(proj, swa_slopes, sinks[i].astype(F32), batch=batch, seq=seq, tq=tq_swa)

        x1, r = _merge(xf, od, osw, proj, p[i].reshape(n, -1),
                       w_br_diff[i].astype(BF16), w_br_swa[i].astype(BF16),
                       w_out[i].astype(BF16), b_out[i][None, :],
                       ln1_g[i][None, :], ln1_b[i][None, :],
                       w_ple_gate[i].astype(BF16), w_ple_proj[i].astype(BF16),
                       dn_alpha=dn_alpha, tm=tm_merge)

        wr = jnp.zeros((d, LANES), F32).at[:, :N_EXPERTS].set(w_router[i])
        br = jnp.full((1, LANES), NEG_BIG, F32).at[0, :N_EXPERTS].set(b_router[i])
        code, gates, cnt = _router(x1, wr, br, tm=tm_router)

        counts = cnt[0, :N_EXPERTS].astype(jnp.int32)
        padded = (counts + MOE_BLOCK - 1) // MOE_BLOCK * MOE_BLOCK
        padded_end = jnp.cumsum(padded)
        off = (padded_end - padded).astype(jnp.int32)
        block_start = jnp.arange(n_blocks, dtype=jnp.int32) * MOE_BLOCK
        block_e = jnp.minimum(
            jnp.sum((block_start[:, None] >= padded_end[None, :]).astype(jnp.int32), axis=1),
            N_EXPERTS - 1)
        codes = code[:, :TOP_K].reshape(n_assign)
        n_used = (padded_end[-1:] // MOE_BLOCK).astype(jnp.int32)
        tail = jnp.concatenate([jnp.where(padded > 0, padded_end - MOE_BLOCK, -1).astype(jnp.int32),
                                n_used])

        xs = _dispatch(off, tail, codes, x1, n_slots, rows=rows_moe)
        y = _experts(block_e, n_used, xs, w_gate_up, b_gate_up[:, :, None, :],
                     w_down, b_down[:, :, None, :], layer=i)
        xf, xb = _combine(off, codes, gates, r, y, ln2_g[i][None, :], ln2_b[i][None, :],
                          rows=rows_moe)
    return xf.reshape(batch, seq, d)
```

```python
import functools
import math

import jax
import jax.numpy as jnp
from jax import lax
from jax.experimental import pallas as pl
from jax.experimental.pallas import tpu as pltpu

F32 = jnp.float32
BF16 = jnp.bfloat16

HEAD_DIM = 64
DIFF_HEADS = 8
SWA_HEADS = 16
SWA_KV_HEADS = 2
SWA_GROUP = SWA_HEADS // SWA_KV_HEADS
WINDOW = 128
N_EXPERTS = 32
TOP_K = 4
MOE_BLOCK = 512
SWIGLU_LIMIT = 7.0
SWIGLU_ALPHA = 1.702
LN_EPS = 1e-5
RMS_EPS = 1e-5
NEG_BIG = -1e30

LANES = 128
VMEM_LIMIT = 56 * 1024 * 1024
RANK_BITS = 16

_COL_GA, _COL_GB, _COL_DQ, _COL_DK, _COL_DV, _COL_SQ, _COL_SK, _COL_SV = (
    0, 1024, 2048, 3072, 4096, 5120, 6144, 6272)


def _params(semantics):
    return pltpu.CompilerParams(dimension_semantics=semantics,
                                vmem_limit_bytes=VMEM_LIMIT)


def _dma_params(semantics):
    return pltpu.CompilerParams(dimension_semantics=semantics,
                                vmem_limit_bytes=VMEM_LIMIT,
                                disable_bounds_checks=True)


def _linear_kernel(x_ref, w_ref, b_ref, o_ref):
    acc = jnp.dot(x_ref[...], w_ref[...], preferred_element_type=F32)
    o_ref[...] = (acc + b_ref[...]).astype(o_ref.dtype)


def _linear(x, w, b, *, tm, tn):
    n, k = x.shape
    nout = w.shape[1]
    return pl.pallas_call(
        _linear_kernel,
        out_shape=jax.ShapeDtypeStruct((n, nout), BF16),
        grid=(n // tm, nout // tn),
        in_specs=[pl.BlockSpec((tm, k), lambda i, j: (i, 0)),
                  pl.BlockSpec((k, tn), lambda i, j: (0, j)),
                  pl.BlockSpec((1, tn), lambda i, j: (0, j))],
        out_specs=pl.BlockSpec((tm, tn), lambda i, j: (i, j)),
        compiler_params=_params(("parallel", "arbitrary")),
        name="in_proj",
    )(x, w, b)


N_AUG = 6
STRIP = 256
DV = 2 * HEAD_DIM
ONES_ROWS = 16


def _key_aug(tk, first_lane):
    r = lax.broadcasted_iota(jnp.int32, (tk, LANES), 0)
    lane = lax.broadcasted_iota(jnp.int32, (tk, LANES), 1) - first_lane
    hi = ((r >> 7) << 7).astype(F32)
    lo = (r & 127).astype(F32)
    return jnp.where((lane >= 0) & (lane < 3), hi,
                     jnp.where((lane >= 3) & (lane < N_AUG), lo, 0.0))


def _diff_attn_kernel(slopes_ref, lq1_ref, lk1_ref, lq2_ref, lk2_ref, qaug_ref,
                      q_ref, k_ref, v_ref, w_ref, o_ref,
                      k1a_sc, k2a_sc, vt_sc, a1, a2, st, sa1, sa2, sb1, sb2, *, tq, lam_init):
    h = pl.program_id(1)
    qi = pl.program_id(2)
    slope = slopes_ref[h]
    n_chunks = k1a_sc.shape[0]

    @pl.when(qi == 0)
    def _():
        lane = lax.broadcasted_iota(jnp.int32, (tq, LANES), 1)
        aug1 = _key_aug(tq, HEAD_DIM).astype(BF16)
        aug2 = _key_aug(tq, 0).astype(BF16)

        def build(c, carry):
            rows = pl.ds(pl.multiple_of(c * tq, tq), tq)
            k = k_ref[rows, :]
            k1a_sc[c] = jnp.where(lane < HEAD_DIM, k, aug1)
            k2a_sc[c] = jnp.where(lane >= HEAD_DIM, k, aug2)
            vt_sc[c, 0:DV, :] = v_ref[rows, :].astype(F32).T.astype(BF16)
            vt_sc[c, DV:DV + ONES_ROWS, :] = jnp.where(
                lax.broadcasted_iota(jnp.int32, (ONES_ROWS, tq), 0) == 0, 1.0, 0.0).astype(BF16)
            return carry

        lax.fori_loop(0, n_chunks, build, 0)

    qt = q_ref[...].astype(F32).T
    row = lax.broadcasted_iota(jnp.int32, qt.shape, 0)
    qt1 = jnp.where(row < HEAD_DIM, qt, qaug_ref[0, 0]).astype(BF16)
    qt2 = jnp.where(row >= HEAD_DIM, qt, qaug_ref[0, 1]).astype(BF16)
    a1[...] = jnp.zeros(a1.shape, F32)
    a2[...] = jnp.zeros(a2.shape, F32)

    M1, M2, XA1, XA2 = range(4)
    for r_ in (M1, M2):
        st[r_:r_ + 1, :] = jnp.full((1, tq), NEG_BIG, F32)

    maps = ((k1a_sc, qt1, a1, M1, XA1), (k2a_sc, qt2, a2, M2, XA2))
    tc = min(tq, STRIP)
    pieces = [(mp, slice(h * tc, (h + 1) * tc)) for h in range(tq // tc) for mp in range(2)]

    def scores(j, mp, cols, dst, masked=False):
        s = jnp.dot(maps[mp][0][j], maps[mp][1][:, cols], preferred_element_type=F32)
        if masked:
            krow = lax.broadcasted_iota(jnp.int32, s.shape, 0)
            qcol = lax.broadcasted_iota(jnp.int32, s.shape, 1) + cols.start
            s = jnp.where(krow <= qcol, s, NEG_BIG)
        dst[mp][:, cols] = s
        return jnp.max(s, axis=0, keepdims=True)

    def accumulate(j, mp, cols, src, mx):
        _, _, a_sc, mr, _ = maps[mp]
        c = slope * jnp.full((1, tc), (j - qi) * tq, jnp.int32).astype(F32)
        m = st[mr:mr + 1, cols]
        m_new = jnp.maximum(m, mx + c)
        alpha = jnp.exp2(m - m_new)
        p = jnp.exp2(src[mp][:, cols] - (m_new - c))
        st[mr:mr + 1, cols] = m_new
        a_sc[:, cols] = alpha * a_sc[:, cols] + jnp.dot(vt_sc[j], p.astype(BF16),
                                                        preferred_element_type=F32)

    bufs = ((sa1, sa2), (sb1, sb2))

    def park(maxima):
        for (mp, cols), x in zip(pieces, maxima):
            xr = maps[mp][4]
            st[xr:xr + 1, cols] = x

    def parked():
        return [st[maps[mp][4]:maps[mp][4] + 1, cols] for mp, cols in pieces]

    def step(s_blk, s_dst, p_blk, p_src, p_max, masked=False):
        out = []
        for (mp, cols), x in zip(pieces, p_max):
            out.append(scores(s_blk, mp, cols, s_dst, masked))
            accumulate(p_blk, mp, cols, p_src, x)
        return out

    @pl.when(qi == 0)
    def _():
        xd = [scores(0, mp, cols, bufs[0], masked=True) for mp, cols in pieces]
        for (mp, cols), x in zip(pieces, xd):
            accumulate(0, mp, cols, bufs[0], x)

    @pl.when(qi > 0)
    def _():
        park([scores(0, mp, cols, bufs[0]) for mp, cols in pieces])
        n_loop = qi - 1
        n_quad = n_loop // 4

        def quad(i, carry):
            j = 4 * i
            x = parked()
            for t in range(4):
                x = step(j + t + 1, bufs[(t + 1) % 2], j + t, bufs[t % 2], x)
            park(x)
            return carry

        lax.fori_loop(0, n_quad, quad, 0)
        j = 4 * n_quad
        rem = n_loop - j
        for k in range(4):
            @pl.when(rem == k)
            def _(k=k):
                x = parked()
                for t in range(k):
                    x = step(j + t + 1, bufs[(t + 1) % 2], j + t, bufs[t % 2], x)
                cur, nxt = bufs[k % 2], bufs[(k + 1) % 2]
                xd = step(qi, nxt, qi - 1, cur, x, masked=True)
                for (mp, cols), xk in zip(pieces, xd):
                    accumulate(qi, mp, cols, nxt, xk)

    lam = (jnp.exp(jnp.sum(lq1_ref[...] * lk1_ref[...], axis=1, keepdims=True))
           - jnp.exp(jnp.sum(lq2_ref[...] * lk2_ref[...], axis=1, keepdims=True))
           + lam_init)
    o1 = a1[0:DV, :] / a1[DV:DV + 1, :]
    o2 = a2[0:DV, :] / a2[DV:DV + 1, :]
    o = (o1 - lam * o2).T
    y = o * lax.rsqrt(jnp.mean(jnp.square(o), axis=1, keepdims=True) + RMS_EPS)
    y = (y * w_ref[...]) * (1.0 - lam_init)
    o_ref[...] = y.astype(o_ref.dtype)


def _diff_attention(proj, slopes_l2, qaug, lq1, lk1, lq2, lk2, subln_w, *, batch, seq, lam_init, tq):
    n = proj.shape[0]
    nq = seq // tq
    kern = functools.partial(_diff_attn_kernel, tq=tq, lam_init=lam_init)
    vec = pl.BlockSpec((1, HEAD_DIM), lambda b, h, i: (0, 0))
    cq, ck, cv = _COL_DQ // LANES, _COL_DK // LANES, _COL_DV // LANES
    return pl.pallas_call(
        kern,
        out_shape=jax.ShapeDtypeStruct((n, DIFF_HEADS * 2 * HEAD_DIM), BF16),
        grid=(batch, DIFF_HEADS, nq),
        in_specs=[pl.BlockSpec(memory_space=pltpu.SMEM),
                  vec, vec, vec, vec,
                  pl.BlockSpec((1, 2, LANES, tq), lambda b, h, i: (h, 0, 0, 0)),
                  pl.BlockSpec((tq, LANES), lambda b, h, i: (b * nq + i, cq + h)),
                  pl.BlockSpec((seq, LANES), lambda b, h, i: (b, ck + h)),
                  pl.BlockSpec((seq, LANES), lambda b, h, i: (b, cv + h)),
                  pl.BlockSpec((1, 2 * HEAD_DIM), lambda b, h, i: (0, 0))],
        out_specs=pl.BlockSpec((tq, LANES), lambda b, h, i: (b * nq + i, h)),
        scratch_shapes=[pltpu.VMEM((nq, tq, LANES), BF16), pltpu.VMEM((nq, tq, LANES), BF16),
                        pltpu.VMEM((nq, DV + ONES_ROWS, tq), BF16),
                        pltpu.VMEM((DV + ONES_ROWS, tq), F32), pltpu.VMEM((DV + ONES_ROWS, tq), F32),
                        pltpu.VMEM((8, tq), F32)] + [pltpu.VMEM((tq, tq), F32)] * 4,
        compiler_params=_params(("arbitrary", "arbitrary", "arbitrary")),
        name="diff_attn",
    )(slopes_l2, lq1, lk1, lq2, lk2, qaug, proj, proj, proj, subln_w)


def _diff_query_aug(slopes_l2, tq):
    s0 = slopes_l2.astype(BF16).astype(F32)
    s1 = (slopes_l2 - s0).astype(BF16).astype(F32)
    s2 = (slopes_l2 - s0 - s1).astype(BF16).astype(F32)
    parts = jnp.stack([s0, s1, s2, s0, s1, s2], axis=1)
    n_heads = slopes_l2.shape[0]
    cols = jnp.zeros((n_heads, 2, LANES), F32)
    cols = cols.at[:, 0, HEAD_DIM:HEAD_DIM + N_AUG].set(parts)
    cols = cols.at[:, 1, 0:N_AUG].set(parts)
    return jnp.broadcast_to(cols[:, :, :, None], (n_heads, 2, LANES, tq))


def _swa_kernel(slopes_ref, sinks_ref, q_ref, kp_ref, kc_ref, vp_ref, vc_ref, o_ref,
                bias_sc, sink_sc, ot_sc, *, tq):
    qi = pl.program_id(1)
    wide = SWA_GROUP * WINDOW

    @pl.when((pl.program_id(0) == 0) & (qi == 0))
    def _():
        key = lax.broadcasted_iota(jnp.int32, (2 * WINDOW, wide), 0)
        col = lax.broadcasted_iota(jnp.int32, (2 * WINDOW, wide), 1)
        dist = (col & (WINDOW - 1)) + WINDOW - key
        valid = (dist >= 0) & (dist < WINDOW)
        head = lax.broadcasted_iota(jnp.int32, (1, wide), 1) >> (WINDOW.bit_length() - 1)
        for hk in range(SWA_KV_HEADS):
            slope = jnp.zeros((1, wide), F32)
            sink = jnp.zeros((1, wide), F32)
            for g in range(SWA_GROUP):
                slope = jnp.where(head == g, slopes_ref[hk * SWA_GROUP + g], slope)
                sink = jnp.where(head == g, sinks_ref[hk * SWA_GROUP + g], sink)
            bias_sc[hk] = jnp.where(valid, -slope * dist.astype(F32), NEG_BIG)
            sink_sc[hk] = sink

    kcat = jnp.concatenate([kp_ref[...], kc_ref[...]], axis=0)
    vcat = jnp.concatenate([vp_ref[...], vc_ref[...]], axis=0)
    zeros = jnp.zeros((HEAD_DIM, WINDOW), F32)
    for j in range(tq // WINDOW):
        kj = kcat[j * WINDOW:(j + 2) * WINDOW]
        vt = vcat[j * WINDOW:(j + 2) * WINDOW].astype(F32).T
        qt = q_ref[j * WINDOW:(j + 1) * WINDOW, :].astype(F32).T
        for hk in range(SWA_KV_HEADS):
            cols = []
            for g in range(SWA_GROUP):
                hq = hk * SWA_GROUP + g
                qh = qt[hq * HEAD_DIM:(hq + 1) * HEAD_DIM]
                cols.append(jnp.concatenate([qh, zeros] if hk == 0 else [zeros, qh], axis=0))
            q8 = jnp.concatenate(cols, axis=1).astype(BF16)
            s = jnp.dot(kj, q8, preferred_element_type=F32) + bias_sc[hk]
            if j == 0:
                key = lax.broadcasted_iota(jnp.int32, s.shape, 0)
                s = jnp.where((key < WINDOW) & (qi == 0), NEG_BIG, s)
            sink = sink_sc[hk]
            m = jnp.maximum(jnp.max(s, axis=0, keepdims=True), sink)
            e = jnp.exp2(s - m)
            denom = jnp.sum(e, axis=0, keepdims=True) + jnp.exp2(sink - m)
            v_hk = vt[hk * HEAD_DIM:(hk + 1) * HEAD_DIM].astype(BF16)
            ot = jnp.dot(v_hk, e.astype(BF16), preferred_element_type=F32) / denom
            for g in range(SWA_GROUP):
                hq = hk * SWA_GROUP + g
                ot_sc[hq * HEAD_DIM:(hq + 1) * HEAD_DIM, :] = ot[:, g * WINDOW:(g + 1) * WINDOW]
        o_ref[j * WINDOW:(j + 1) * WINDOW, :] = ot_sc[...].T.astype(o_ref.dtype)


def _swa_attention(proj, slopes, sinks, *, batch, seq, tq):
    n = proj.shape[0]
    nq = seq // tq
    sub = tq // WINDOW
    nwin = seq // WINDOW
    kern = functools.partial(_swa_kernel, tq=tq)
    cq = _COL_SQ // (SWA_HEADS * HEAD_DIM)
    ck, cv = _COL_SK // LANES, _COL_SV // LANES
    prev = lambda c: (lambda b, i: (b * nwin + jnp.maximum(i * sub - 1, 0), c))
    cur = lambda c: (lambda b, i: (b * nq + i, c))
    smem = pl.BlockSpec(memory_space=pltpu.SMEM)
    return pl.pallas_call(
        kern,
        out_shape=jax.ShapeDtypeStruct((n, SWA_HEADS * HEAD_DIM), BF16),
        grid=(batch, nq),
        in_specs=[smem, smem,
                  pl.BlockSpec((tq, SWA_HEADS * HEAD_DIM), cur(cq)),
                  pl.BlockSpec((WINDOW, LANES), prev(ck)),
                  pl.BlockSpec((tq, LANES), cur(ck)),
                  pl.BlockSpec((WINDOW, LANES), prev(cv)),
                  pl.BlockSpec((tq, LANES), cur(cv))],
        out_specs=pl.BlockSpec((tq, SWA_HEADS * HEAD_DIM), lambda b, i: (b * nq + i, 0)),
        scratch_shapes=[pltpu.VMEM((SWA_KV_HEADS, 2 * WINDOW, SWA_GROUP * WINDOW), F32),
                        pltpu.VMEM((SWA_KV_HEADS, 1, SWA_GROUP * WINDOW), F32),
                        pltpu.VMEM((SWA_HEADS * HEAD_DIM, WINDOW), F32)],
        compiler_params=_params(("arbitrary", "arbitrary")),
        name="swa_attn",
    )(slopes, sinks, proj, proj, proj, proj, proj)


def _layer_norm(y, g, b):
    mu = jnp.mean(y, axis=1, keepdims=True)
    var = jnp.mean(jnp.square(y - mu), axis=1, keepdims=True)
    return (y - mu) * lax.rsqrt(var + LN_EPS) * g + b


def _merge_kernel(x_ref, od_ref, os_ref, ga_ref, gb_ref, p_ref,
                  wa_ref, wb_ref, wo_ref, bo_ref, g1_ref, b1_ref, wpg_ref, wpp_ref,
                  x1_ref, r_ref, *, dn_alpha):
    a = jnp.dot(od_ref[...], wa_ref[...], preferred_element_type=F32)
    b = jnp.dot(os_ref[...], wb_ref[...], preferred_element_type=F32)
    merged = (jax.nn.sigmoid(ga_ref[...].astype(F32)) * a
              + jax.nn.sigmoid(gb_ref[...].astype(F32)) * b)
    mix = jnp.dot(merged.astype(BF16), wo_ref[...], preferred_element_type=F32) + bo_ref[...]
    x1 = _layer_norm(dn_alpha * x_ref[...] + mix, g1_ref[...], b1_ref[...])
    x1_ref[...] = x1
    gate = jax.nn.sigmoid(jnp.dot(x1.astype(BF16), wpg_ref[...], preferred_element_type=F32))
    ple = gate * jnp.dot(p_ref[...].astype(BF16), wpp_ref[...], preferred_element_type=F32)
    r_ref[...] = dn_alpha * x1 + ple


def _merge(x, od, osw, proj, p, wa, wb, wo, bo, g1, b1, wpg, wpp, *, dn_alpha, tm):
    n, d = x.shape
    pd = p.shape[1]
    row = lambda c: (lambda i: (i, c))
    full = lambda shape: pl.BlockSpec(shape, lambda i: (0, 0))
    kern = functools.partial(_merge_kernel, dn_alpha=dn_alpha)
    return pl.pallas_call(
        kern,
        out_shape=(jax.ShapeDtypeStruct((n, d), F32), jax.ShapeDtypeStruct((n, d), F32)),
        grid=(n // tm,),
        in_specs=[pl.BlockSpec((tm, d), row(0)),
                  pl.BlockSpec((tm, d), row(0)),
                  pl.BlockSpec((tm, d), row(0)),
                  pl.BlockSpec((tm, d), row(_COL_GA // d)),
                  pl.BlockSpec((tm, d), row(_COL_GB // d)),
                  pl.BlockSpec((tm, pd), row(0)),
                  full((d, d)), full((d, d)), full((d, d)), full((1, d)),
                  full((1, d)), full((1, d)), full((d, d)), full((pd, d))],
        out_specs=(pl.BlockSpec((tm, d), row(0)), pl.BlockSpec((tm, d), row(0))),
        compiler_params=_params(("parallel",)),
        name="merge_ln1",
    )(x, od, osw, proj, proj, p, wa, wb, wo, bo, g1, b1, wpg, wpp)


def _router_kernel(x_ref, w_ref, b_ref, code_ref, gate_ref, cnt_ref, carry, *, tm):
    @pl.when(pl.program_id(0) == 0)
    def _():
        carry[...] = jnp.zeros(carry.shape, F32)

    logits = jnp.dot(x_ref[...], w_ref[...], preferred_element_type=F32,
                     precision=lax.Precision.HIGHEST) + b_ref[...]
    lane = lax.broadcasted_iota(jnp.int32, logits.shape, 1)
    lanef = lane.astype(F32)
    work = logits
    tops, idxs = [], []
    onehot = jnp.zeros(logits.shape, F32)
    for _ in range(TOP_K):
        m = jnp.max(work, axis=1, keepdims=True)
        idx = jnp.min(jnp.where(work == m, lanef, float(LANES)), axis=1, keepdims=True)
        sel = lanef == idx
        onehot = jnp.where(sel, 1.0, onehot)
        work = jnp.where(sel, -jnp.inf, work)
        tops.append(m)
        idxs.append(idx)
    es = [jnp.exp(t - tops[0]) for t in tops]
    denom = es[0] + es[1] + es[2] + es[3]
    r = lax.broadcasted_iota(jnp.int32, (tm, tm), 0)
    c = lax.broadcasted_iota(jnp.int32, (tm, tm), 1)
    tri = jnp.where(c < r, 1.0, 0.0).astype(BF16)
    before = jnp.dot(tri, onehot.astype(BF16), preferred_element_type=F32) + carry[0:1, :]
    code = jnp.zeros(logits.shape, jnp.int32)
    gate = jnp.zeros(logits.shape, F32)
    for k in range(TOP_K):
        rank = jnp.sum(jnp.where(lanef == idxs[k], before, 0.0), axis=1, keepdims=True)
        ck = (idxs[k] * float(1 << RANK_BITS) + rank).astype(jnp.int32)
        code = jnp.where(lane == k, ck, code)
        gate = jnp.where(lane == k, es[k] / denom, gate)
    code_ref[...] = code
    gate_ref[...] = gate
    carry[0:1, :] = carry[0:1, :] + jnp.sum(onehot, axis=0, keepdims=True)
    cnt_ref[...] = carry[...]


def _router(x1, w, b, *, tm):
    n, d = x1.shape
    kern = functools.partial(_router_kernel, tm=tm)
    return pl.pallas_call(
        kern,
        out_shape=(jax.ShapeDtypeStruct((n, LANES), jnp.int32),
                   jax.ShapeDtypeStruct((n, LANES), F32),
                   jax.ShapeDtypeStruct((8, LANES), F32)),
        grid=(n // tm,),
        in_specs=[pl.BlockSpec((tm, d), lambda i: (i, 0)),
                  pl.BlockSpec((d, LANES), lambda i: (0, 0)),
                  pl.BlockSpec((1, LANES), lambda i: (0, 0))],
        out_specs=(pl.BlockSpec((tm, LANES), lambda i: (i, 0)),
                   pl.BlockSpec((tm, LANES), lambda i: (i, 0)),
                   pl.BlockSpec((8, LANES), lambda i: (0, 0))),
        scratch_shapes=[pltpu.VMEM((8, LANES), F32)],
        compiler_params=_params(("arbitrary",)),
        name="router",
    )(x1, w, b)


def _slot(code, off_ref):
    return off_ref[code >> RANK_BITS] + (code & ((1 << RANK_BITS) - 1))


def _dispatch_kernel(off_ref, tail_ref, codes_ref, x_ref, xs_ref, codes_smem, zeros, csem, sem, zsem,
                     *, rows):
    i = pl.program_id(0)
    n_steps = pl.num_programs(0)
    n_assign = rows * TOP_K

    def codes_load(step):
        return pltpu.make_async_copy(codes_ref.at[pl.ds(step * n_assign, n_assign)],
                                     codes_smem, csem)

    @pl.when(i == 0)
    def _():
        codes_load(0).start()
        zeros[...] = jnp.zeros(zeros.shape, zeros.dtype)

        def fill(e):
            first = pl.multiple_of(tail_ref[e], MOE_BLOCK)
            return pltpu.make_async_copy(zeros, xs_ref.at[pl.ds(first, MOE_BLOCK)], zsem)

        def start(e, carry):
            @pl.when(tail_ref[e] >= 0)
            def _():
                fill(e).start()
            return carry

        def finish(e, carry):
            @pl.when(tail_ref[e] >= 0)
            def _():
                fill(e).wait()
            return carry

        def unused(b):
            first = pl.multiple_of(b * MOE_BLOCK, MOE_BLOCK)
            return pltpu.make_async_copy(zeros, xs_ref.at[pl.ds(first, MOE_BLOCK)], zsem)

        n_used = tail_ref[N_EXPERTS]
        n_blocks = xs_ref.shape[0] // MOE_BLOCK
        lax.fori_loop(0, N_EXPERTS, start, 0)
        lax.fori_loop(n_used, n_blocks, lambda b, c: (unused(b).start(), c)[1], 0)
        lax.fori_loop(0, N_EXPERTS, finish, 0)
        lax.fori_loop(n_used, n_blocks, lambda b, c: (unused(b).wait(), c)[1], 0)

    codes_load(i).wait()

    def issue(t, carry):
        src = x_ref.at[pl.ds(t, 1)]
        for k in range(TOP_K):
            slot = _slot(codes_smem[t * TOP_K + k], off_ref)
            pltpu.make_async_copy(src, xs_ref.at[pl.ds(slot, 1)], sem).start()
        return carry

    lax.fori_loop(0, rows, issue, 0, unroll=2)

    @pl.when(i + 1 < n_steps)
    def _():
        codes_load(i + 1).start()

    pltpu.make_async_copy(xs_ref.at[pl.ds(0, n_assign)], xs_ref.at[pl.ds(0, n_assign)], sem).wait()


def _dispatch(off, tail, codes, x1, n_slots, *, rows):
    n, d = x1.shape
    kern = functools.partial(_dispatch_kernel, rows=rows)
    any_spec = pl.BlockSpec(memory_space=pl.ANY)
    return pl.pallas_call(
        kern,
        out_shape=jax.ShapeDtypeStruct((n_slots, d), x1.dtype),
        grid_spec=pltpu.PrefetchScalarGridSpec(
            num_scalar_prefetch=2,
            grid=(n // rows,),
            in_specs=[any_spec, pl.BlockSpec((rows, d), lambda i, off, tail: (i, 0))],
            out_specs=any_spec,
            scratch_shapes=[pltpu.SMEM((rows * TOP_K,), jnp.int32),
                            pltpu.VMEM((MOE_BLOCK, d), x1.dtype),
                            pltpu.SemaphoreType.DMA, pltpu.SemaphoreType.DMA,
                            pltpu.SemaphoreType.DMA]),
        compiler_params=_dma_params(("arbitrary",)),
        name="moe_dispatch",
    )(off, tail, codes, x1)


def _expert_kernel(be_ref, nu_ref, xs_ref, wgu_ref, bgu_ref, wd_ref, bd_ref, y_ref, wgu_b, wd_b):
    j = pl.program_id(0)
    dff = wd_ref.shape[2]

    @pl.when(j < nu_ref[0])
    def _():
        @pl.when((j == 0) | (be_ref[j] != be_ref[jnp.maximum(j - 1, 0)]))
        def _():
            wgu_b[...] = wgu_ref[0, 0].astype(BF16)
            wd_b[...] = wd_ref[0, 0].astype(BF16)

        gu = jnp.dot(xs_ref[...].astype(BF16), wgu_b[...], preferred_element_type=F32) + bgu_ref[0, 0]
        gate = jnp.minimum(gu[:, :dff], SWIGLU_LIMIT)
        up = jnp.clip(gu[:, dff:], -SWIGLU_LIMIT, SWIGLU_LIMIT)
        act = (up + 1.0) * (gate * jax.nn.sigmoid(SWIGLU_ALPHA * gate))
        y_ref[...] = jnp.dot(act.astype(BF16), wd_b[...], preferred_element_type=F32) + bd_ref[0, 0]

    @pl.when(j >= nu_ref[0])
    def _():
        y_ref[...] = jnp.zeros(y_ref.shape, y_ref.dtype)


def _experts(block_e, n_used, xs, wgu, bgu, wd, bd, *, layer):
    n_slots, d = xs.shape
    dff = wd.shape[2]
    nb = n_slots // MOE_BLOCK
    blk = lambda j, be, nu: jnp.minimum(j, nu[0] - 1)
    row = lambda j, be, nu: (blk(j, be, nu), 0)
    wsel = lambda j, be, nu: (layer, be[blk(j, be, nu)], 0, 0)
    return pl.pallas_call(
        _expert_kernel,
        out_shape=jax.ShapeDtypeStruct((n_slots, d), F32),
        grid_spec=pltpu.PrefetchScalarGridSpec(
            num_scalar_prefetch=2,
            grid=(nb,),
            in_specs=[pl.BlockSpec((MOE_BLOCK, d), row),
                      pl.BlockSpec((1, 1, d, 2 * dff), wsel),
                      pl.BlockSpec((1, 1, 1, 2 * dff), wsel),
                      pl.BlockSpec((1, 1, dff, d), wsel),
                      pl.BlockSpec((1, 1, 1, d), wsel)],
            out_specs=pl.BlockSpec((MOE_BLOCK, d), lambda j, be, nu: (j, 0)),
            scratch_shapes=[pltpu.VMEM((d, 2 * dff), BF16), pltpu.VMEM((dff, d), BF16)]),
        compiler_params=_params(("arbitrary",)),
        name="moe_experts",
    )(block_e, n_used, xs, wgu, bgu, wd, bd)


def _combine_kernel(off_ref, codes_ref, gate_ref, r_ref, y_ref, g2_ref, b2_ref,
                    x2_ref, xb_ref, codes_smem, ybuf0, ybuf1, csem, sem0, sem1, *, rows):
    i = pl.program_id(0)
    n_steps = pl.num_programs(0)
    n_assign = rows * TOP_K

    def codes_load(step):
        return pltpu.make_async_copy(codes_ref.at[pl.ds(step * n_assign, n_assign)],
                                     codes_smem, csem)

    def gather(step, ybuf, sem):
        def issue(t, carry):
            for k in range(TOP_K):
                slot = _slot(codes_smem[t * TOP_K + k], off_ref)
                pltpu.make_async_copy(y_ref.at[pl.ds(slot, 1)],
                                      ybuf.at[pl.ds(k * rows + t, 1)], sem).start()
            return carry

        lax.fori_loop(0, rows, issue, 0, unroll=2)

        @pl.when(step + 1 < n_steps)
        def _():
            codes_load(step + 1).start()

    @pl.when(i == 0)
    def _():
        first = codes_load(0)
        first.start()
        first.wait()
        gather(0, ybuf0, sem0)

    def step(ybuf, sem, ybuf_next, sem_next):
        @pl.when(i + 1 < n_steps)
        def _():
            codes_load(i + 1).wait()
            gather(i + 1, ybuf_next, sem_next)

        pltpu.make_async_copy(y_ref.at[pl.ds(0, n_assign)], ybuf, sem).wait()
        gate = gate_ref[...]
        acc = r_ref[...]
        for k in range(TOP_K):
            acc = acc + gate[:, k:k + 1] * ybuf[k * rows:(k + 1) * rows, :]
        x2 = _layer_norm(acc, g2_ref[...], b2_ref[...])
        x2_ref[...] = x2
        xb_ref[...] = x2.astype(BF16)

    @pl.when(i % 2 == 0)
    def _():
        step(ybuf0, sem0, ybuf1, sem1)

    @pl.when(i % 2 == 1)
    def _():
        step(ybuf1, sem1, ybuf0, sem0)


def _combine(off, codes, gates, r, y, g2, b2, *, rows):
    n, d = r.shape
    kern = functools.partial(_combine_kernel, rows=rows)
    any_spec = pl.BlockSpec(memory_space=pl.ANY)
    return pl.pallas_call(
        kern,
        out_shape=(jax.ShapeDtypeStruct((n, d), F32), jax.ShapeDtypeStruct((n, d), BF16)),
        grid_spec=pltpu.PrefetchScalarGridSpec(
            num_scalar_prefetch=1,
            grid=(n // rows,),
            in_specs=[any_spec,
                      pl.BlockSpec((rows, LANES), lambda i, off: (i, 0)),
                      pl.BlockSpec((rows, d), lambda i, off: (i, 0)),
                      any_spec,
                      pl.BlockSpec((1, d), lambda i, off: (0, 0)),
                      pl.BlockSpec((1, d), lambda i, off: (0, 0))],
            out_specs=(pl.BlockSpec((rows, d), lambda i, off: (i, 0)),
                       pl.BlockSpec((rows, d), lambda i, off: (i, 0))),
            scratch_shapes=[pltpu.SMEM((rows * TOP_K,), jnp.int32),
                            pltpu.VMEM((rows * TOP_K, d), F32), pltpu.VMEM((rows * TOP_K, d), F32),
                            pltpu.SemaphoreType.DMA, pltpu.SemaphoreType.DMA,
                            pltpu.SemaphoreType.DMA]),
        compiler_params=_dma_params(("arbitrary",)),
        name="moe_combine_ln2",
    )(off, codes, gates, r, y, g2, b2)


def _alibi_slopes(n_heads):
    h = jnp.arange(1, n_heads + 1, dtype=F32)
    return jnp.exp2(-8.0 * h / n_heads)


def _block_size(n, target):
    t = min(n, target)
    while n % t:
        t //= 2
    return t


def kernel(x, p, w_in, b_in, lambda_q1, lambda_k1, lambda_q2, lambda_k2, subln_w, sinks,
           w_br_diff, w_br_swa, w_out, b_out, ln1_g, ln1_b, w_router, b_router,
           w_gate_up, b_gate_up, w_down, b_down, w_ple_gate, w_ple_proj, ln2_g, ln2_b):
    batch, seq, d = x.shape
    depth = w_in.shape[0]
    n = batch * seq
    dn_alpha = (2 * depth) ** 0.25
    n_assign = n * TOP_K
    n_blocks = n_assign // MOE_BLOCK + N_EXPERTS + 1
    n_slots = n_blocks * MOE_BLOCK

    n_in = w_in.shape[2]
    perm = jnp.concatenate([jnp.arange(n_in - 2 * d, n_in), jnp.arange(0, n_in - 2 * d)])
    log2e = math.log2(math.e)
    diff_slopes = _alibi_slopes(DIFF_HEADS) * log2e
    swa_slopes = _alibi_slopes(SWA_HEADS) * log2e

    tm_lin = _block_size(n, 1024)
    tq_diff = _block_size(seq, 512)
    tq_swa = _block_size(seq, 512)
    tm_merge = _block_size(n, 512)
    tm_router = _block_size(n, 512)
    rows_moe = _block_size(n, 256)

    xf = x.reshape(n, d)
    xb = xf.astype(BF16)
    for i in range(depth):
        lam_init = 0.8 - 0.6 * math.exp(-0.3 * i)
        col_scale = jnp.ones((n_in,), F32).at[_COL_DQ:_COL_DK].set(HEAD_DIM ** -0.5 * log2e)
        col_scale = col_scale.at[_COL_SQ:_COL_SK].set(HEAD_DIM ** -0.5 * log2e)
        w_in_b = (jnp.take(w_in[i], perm, axis=1) * col_scale).astype(BF16)
        b_in_p = (jnp.take(b_in[i], perm) * col_scale)[None, :]
        proj = _linear(xb, w_in_b, b_in_p, tm=tm_lin, tn=1280)

        od = _diff_attention(proj, diff_slopes, _diff_query_aug(diff_slopes, tq_diff),
                             lambda_q1[i][None, :], lambda_k1[i][None, :],
                             lambda_q2[i][None, :], lambda_k2[i][None, :],
                             subln_w[i][None, :], batch=batch, seq=seq,
                             lam_init=lam_init, tq=tq_diff)
        osw = _swa_attention(proj, swa_slopes, sinks[i].astype(F32) * log2e,
                             batch=batch, seq=seq, tq=tq_swa)

        x1, r = _merge(xf, od, osw, proj, p[i].reshape(n, -1),
                       w_br_diff[i].astype(BF16), w_br_swa[i].astype(BF16),
                       w_out[i].astype(BF16), b_out[i][None, :],
                       ln1_g[i][None, :], ln1_b[i][None, :],
                       w_ple_gate[i].astype(BF16), w_ple_proj[i].astype(BF16),
                       dn_alpha=dn_alpha, tm=tm_merge)

        wr = jnp.zeros((d, LANES), F32).at[:, :N_EXPERTS].set(w_router[i])
        br = jnp.full((1, LANES), NEG_BIG, F32).at[0, :N_EXPERTS].set(b_router[i])
        code, gates, cnt = _router(x1, wr, br, tm=tm_router)

        counts = cnt[0, :N_EXPERTS].astype(jnp.int32)
        padded = (counts + MOE_BLOCK - 1) // MOE_BLOCK * MOE_BLOCK
        padded_end = jnp.cumsum(padded)
        off = (padded_end - padded).astype(jnp.int32)
        block_start = jnp.arange(n_blocks, dtype=jnp.int32) * MOE_BLOCK
        block_e = jnp.minimum(
            jnp.sum((block_start[:, None] >= padded_end[None, :]).astype(jnp.int32), axis=1),
            N_EXPERTS - 1)
        codes = code[:, :TOP_K].reshape(n_assign)
        n_used = (padded_end[-1:] // MOE_BLOCK).astype(jnp.int32)
        tail = jnp.concatenate([jnp.where(padded > 0, padded_end - MOE_BLOCK, -1).astype(jnp.int32),
                                n_used])

        xs = _dispatch(off, tail, codes, x1, n_slots, rows=rows_moe)
        y = _experts(block_e, n_used, xs, w_gate_up, b_gate_up[:, :, None, :],
                     w_down, b_down[:, :, None, :], layer=i)
        xf, xb = _combine(off, codes, gates, r, y, ln2_g[i][None, :], ln2_b[i][None, :],
                          rows=rows_moe)
    return xf.reshape(batch, seq, d)
```

```python
import functools
import math

import jax
import jax.numpy as jnp
from jax import lax
from jax.experimental import pallas as pl
from jax.experimental.pallas import tpu as pltpu

F32 = jnp.float32
BF16 = jnp.bfloat16

HEAD_DIM = 64
DIFF_HEADS = 8
SWA_HEADS = 16
SWA_KV_HEADS = 2
SWA_GROUP = SWA_HEADS // SWA_KV_HEADS
WINDOW = 128
N_EXPERTS = 32
TOP_K = 4
MOE_BLOCK = 512
SWIGLU_LIMIT = 7.0
SWIGLU_ALPHA = 1.702
LN_EPS = 1e-5
RMS_EPS = 1e-5
NEG_BIG = -1e30

LANES = 128
VMEM_LIMIT = 56 * 1024 * 1024
RANK_BITS = 16

_COL_GA, _COL_GB, _COL_DQ, _COL_DK, _COL_DV, _COL_SQ, _COL_SK, _COL_SV = (
    0, 1024, 2048, 3072, 4096, 5120, 6144, 6272)


def _params(semantics):
    return pltpu.CompilerParams(dimension_semantics=semantics,
                                vmem_limit_bytes=VMEM_LIMIT)


def _dma_params(semantics):
    return pltpu.CompilerParams(dimension_semantics=semantics,
                                vmem_limit_bytes=VMEM_LIMIT,
                                disable_bounds_checks=True)


def _linear_kernel(x_ref, w_ref, b_ref, o_ref):
    acc = jnp.dot(x_ref[...], w_ref[...], preferred_element_type=F32)
    o_ref[...] = (acc + b_ref[...]).astype(o_ref.dtype)


def _linear(x, w, b, *, tm, tn):
    n, k = x.shape
    nout = w.shape[1]
    return pl.pallas_call(
        _linear_kernel,
        out_shape=jax.ShapeDtypeStruct((n, nout), BF16),
        grid=(n // tm, nout // tn),
        in_specs=[pl.BlockSpec((tm, k), lambda i, j: (i, 0)),
                  pl.BlockSpec((k, tn), lambda i, j: (0, j)),
                  pl.BlockSpec((1, tn), lambda i, j: (0, j))],
        out_specs=pl.BlockSpec((tm, tn), lambda i, j: (i, j)),
        compiler_params=_params(("parallel", "arbitrary")),
        name="in_proj",
    )(x, w, b)


N_AUG = 6
STRIP = 256
DV = 2 * HEAD_DIM
ONES_ROWS = 16


def _key_aug(tk, first_lane):
    r = lax.broadcasted_iota(jnp.int32, (tk, LANES), 0)
    lane = lax.broadcasted_iota(jnp.int32, (tk, LANES), 1) - first_lane
    hi = ((r >> 7) << 7).astype(F32)
    lo = (r & 127).astype(F32)
    return jnp.where((lane >= 0) & (lane < 3), hi,
                     jnp.where((lane >= 3) & (lane < N_AUG), lo, 0.0))


def _diff_attn_kernel(slopes_ref, lq1_ref, lk1_ref, lq2_ref, lk2_ref, qaug_ref,
                      q_ref, k_ref, v_ref, w_ref, o_ref,
                      k1a_sc, k2a_sc, vt_sc, a1, a2, st, sa1, sa2, sb1, sb2, *, tq, tk, lam_init):
    h = pl.program_id(1)
    qi = pl.program_id(2)
    slope = slopes_ref[h]
    n_chunks = k1a_sc.shape[0]

    @pl.when(qi == 0)
    def _():
        lane = lax.broadcasted_iota(jnp.int32, (tk, LANES), 1)
        aug1 = _key_aug(tk, HEAD_DIM).astype(BF16)
        aug2 = _key_aug(tk, 0).astype(BF16)

        def build(c, carry):
            rows = pl.ds(pl.multiple_of(c * tk, tk), tk)
            k = k_ref[rows, :]
            k1a_sc[c] = jnp.where(lane < HEAD_DIM, k, aug1)
            k2a_sc[c] = jnp.where(lane >= HEAD_DIM, k, aug2)
            vt_sc[c, 0:DV, :] = v_ref[rows, :].astype(F32).T.astype(BF16)
            vt_sc[c, DV:DV + ONES_ROWS, :] = jnp.where(
                lax.broadcasted_iota(jnp.int32, (ONES_ROWS, tk), 0) == 0, 1.0, 0.0).astype(BF16)
            return carry

        lax.fori_loop(0, n_chunks, build, 0)

    qt = q_ref[...].astype(F32).T
    row = lax.broadcasted_iota(jnp.int32, qt.shape, 0)
    qt1 = jnp.where(row < HEAD_DIM, qt, qaug_ref[0, 0]).astype(BF16)
    qt2 = jnp.where(row >= HEAD_DIM, qt, qaug_ref[0, 1]).astype(BF16)
    a1[...] = jnp.zeros(a1.shape, F32)
    a2[...] = jnp.zeros(a2.shape, F32)

    M1, M2, XA1, XA2 = range(4)
    for r_ in (M1, M2):
        st[r_:r_ + 1, :] = jnp.full((1, tq), NEG_BIG, F32)

    maps = ((k1a_sc, qt1, a1, M1, XA1), (k2a_sc, qt2, a2, M2, XA2))
    tc = min(tq, STRIP)
    pieces = [(mp, slice(h * tc, (h + 1) * tc)) for h in range(tq // tc) for mp in range(2)]

    def scores(j, mp, cols, dst, key0=None):
        s = jnp.dot(maps[mp][0][j], maps[mp][1][:, cols], preferred_element_type=F32)
        if key0 is not None:
            krow = lax.broadcasted_iota(jnp.int32, s.shape, 0) + key0
            qcol = lax.broadcasted_iota(jnp.int32, s.shape, 1) + cols.start
            s = jnp.where(krow <= qcol, s, NEG_BIG)
        dst[mp][:, cols] = s
        return jnp.max(s, axis=0, keepdims=True)

    def accumulate(j, mp, cols, src, mx):
        _, _, a_sc, mr, _ = maps[mp]
        c = slope * jnp.full((1, tc), j * tk - qi * tq, jnp.int32).astype(F32)
        m = st[mr:mr + 1, cols]
        m_new = jnp.maximum(m, mx + c)
        alpha = jnp.exp2(m - m_new)
        p = jnp.exp2(src[mp][:, cols] - (m_new - c))
        st[mr:mr + 1, cols] = m_new
        a_sc[:, cols] = alpha * a_sc[:, cols] + jnp.dot(vt_sc[j], p.astype(BF16),
                                                        preferred_element_type=F32)

    bufs = ((sa1, sa2), (sb1, sb2))

    def park(maxima):
        for (mp, cols), x in zip(pieces, maxima):
            xr = maps[mp][4]
            st[xr:xr + 1, cols] = x

    def parked():
        return [st[maps[mp][4]:maps[mp][4] + 1, cols] for mp, cols in pieces]

    def step(s_blk, s_dst, p_blk, p_src, p_max, s_pieces=None, p_pieces=None):
        s_pieces = [(mp, cols, None) for mp, cols in pieces] if s_pieces is None else s_pieces
        p_pieces = pieces if p_pieces is None else p_pieces
        out = []
        for k in range(max(len(s_pieces), len(p_pieces))):
            if k < len(s_pieces):
                mp, cols, key0 = s_pieces[k]
                out.append(scores(s_blk, mp, cols, s_dst, key0))
            if k < len(p_pieces):
                mp, cols = p_pieces[k]
                accumulate(p_blk, mp, cols, p_src, p_max[k])
        return out

    def diag_pieces(d):
        out = []
        for mp, cols in pieces:
            if d * tk + tk - 1 <= cols.start:
                out.append((mp, cols, None))
            elif d * tk <= cols.stop - 1:
                out.append((mp, cols, d * tk))
        return out

    d0, d1 = diag_pieces(0), diag_pieces(1)
    d0_cols = [(mp, cols) for mp, cols, _ in d0]
    d1_cols = [(mp, cols) for mp, cols, _ in d1]
    first_diag = 2 * qi

    def finish(cur, nxt, x_prev=None, prev_blk=None):
        if x_prev is None:
            x0 = [scores(first_diag, mp, cols, nxt, key0) for mp, cols, key0 in d0]
        else:
            x0 = step(first_diag, nxt, prev_blk, cur, x_prev, s_pieces=d0)
        x1 = step(first_diag + 1, cur, first_diag, nxt, x0, s_pieces=d1, p_pieces=d0_cols)
        for (mp, cols), x in zip(d1_cols, x1):
            accumulate(first_diag + 1, mp, cols, cur, x)

    @pl.when(qi == 0)
    def _():
        finish(bufs[0], bufs[1])

    @pl.when(qi > 0)
    def _():
        park([scores(0, mp, cols, bufs[0]) for mp, cols in pieces])

        def pair(i, carry):
            j = 2 * i
            x = parked()
            x = step(j + 1, bufs[1], j, bufs[0], x)
            x = step(j + 2, bufs[0], j + 1, bufs[1], x)
            park(x)
            return carry

        lax.fori_loop(0, qi - 1, pair, 0)
        j = 2 * (qi - 1)
        x = step(j + 1, bufs[1], j, bufs[0], parked())
        finish(bufs[1], bufs[0], x_prev=x, prev_blk=j + 1)

    lam = (jnp.exp(jnp.sum(lq1_ref[...] * lk1_ref[...], axis=1, keepdims=True))
           - jnp.exp(jnp.sum(lq2_ref[...] * lk2_ref[...], axis=1, keepdims=True))
           + lam_init)
    o1 = a1[0:DV, :] / a1[DV:DV + 1, :]
    o2 = a2[0:DV, :] / a2[DV:DV + 1, :]
    o = (o1 - lam * o2).T
    y = o * lax.rsqrt(jnp.mean(jnp.square(o), axis=1, keepdims=True) + RMS_EPS)
    y = (y * w_ref[...]) * (1.0 - lam_init)
    o_ref[...] = y.astype(o_ref.dtype)


def _diff_attention(proj, slopes_l2, qaug, lq1, lk1, lq2, lk2, subln_w, *, batch, seq, lam_init, tq):
    n = proj.shape[0]
    nq = seq // tq
    tk = tq // 2
    assert tk % STRIP == 0 and tk % LANES == 0
    nk = seq // tk
    kern = functools.partial(_diff_attn_kernel, tq=tq, tk=tk, lam_init=lam_init)
    vec = pl.BlockSpec((1, HEAD_DIM), lambda b, h, i: (0, 0))
    cq, ck, cv = _COL_DQ // LANES, _COL_DK // LANES, _COL_DV // LANES
    return pl.pallas_call(
        kern,
        out_shape=jax.ShapeDtypeStruct((n, DIFF_HEADS * 2 * HEAD_DIM), BF16),
        grid=(batch, DIFF_HEADS, nq),
        in_specs=[pl.BlockSpec(memory_space=pltpu.SMEM),
                  vec, vec, vec, vec,
                  pl.BlockSpec((1, 2, LANES, tq), lambda b, h, i: (h, 0, 0, 0)),
                  pl.BlockSpec((tq, LANES), lambda b, h, i: (b * nq + i, cq + h)),
                  pl.BlockSpec((seq, LANES), lambda b, h, i: (b, ck + h)),
                  pl.BlockSpec((seq, LANES), lambda b, h, i: (b, cv + h)),
                  pl.BlockSpec((1, 2 * HEAD_DIM), lambda b, h, i: (0, 0))],
        out_specs=pl.BlockSpec((tq, LANES), lambda b, h, i: (b * nq + i, h)),
        scratch_shapes=[pltpu.VMEM((nk, tk, LANES), BF16), pltpu.VMEM((nk, tk, LANES), BF16),
                        pltpu.VMEM((nk, DV + ONES_ROWS, tk), BF16),
                        pltpu.VMEM((DV + ONES_ROWS, tq), F32), pltpu.VMEM((DV + ONES_ROWS, tq), F32),
                        pltpu.VMEM((8, tq), F32)] + [pltpu.VMEM((tk, tq), F32)] * 4,
        compiler_params=_params(("arbitrary", "arbitrary", "arbitrary")),
        name="diff_attn",
    )(slopes_l2, lq1, lk1, lq2, lk2, qaug, proj, proj, proj, subln_w)


def _diff_query_aug(slopes_l2, tq):
    s0 = slopes_l2.astype(BF16).astype(F32)
    s1 = (slopes_l2 - s0).astype(BF16).astype(F32)
    s2 = (slopes_l2 - s0 - s1).astype(BF16).astype(F32)
    parts = jnp.stack([s0, s1, s2, s0, s1, s2], axis=1)
    n_heads = slopes_l2.shape[0]
    cols = jnp.zeros((n_heads, 2, LANES), F32)
    cols = cols.at[:, 0, HEAD_DIM:HEAD_DIM + N_AUG].set(parts)
    cols = cols.at[:, 1, 0:N_AUG].set(parts)
    return jnp.broadcast_to(cols[:, :, :, None], (n_heads, 2, LANES, tq))


def _swa_kernel(slopes_ref, sinks_ref, q_ref, kp_ref, kc_ref, vp_ref, vc_ref, o_ref,
                bias_sc, sink_sc, ot_sc, *, tq):
    qi = pl.program_id(1)
    wide = SWA_GROUP * WINDOW

    @pl.when((pl.program_id(0) == 0) & (qi == 0))
    def _():
        key = lax.broadcasted_iota(jnp.int32, (2 * WINDOW, wide), 0)
        col = lax.broadcasted_iota(jnp.int32, (2 * WINDOW, wide), 1)
        dist = (col & (WINDOW - 1)) + WINDOW - key
        valid = (dist >= 0) & (dist < WINDOW)
        head = lax.broadcasted_iota(jnp.int32, (1, wide), 1) >> (WINDOW.bit_length() - 1)
        for hk in range(SWA_KV_HEADS):
            slope = jnp.zeros((1, wide), F32)
            sink = jnp.zeros((1, wide), F32)
            for g in range(SWA_GROUP):
                slope = jnp.where(head == g, slopes_ref[hk * SWA_GROUP + g], slope)
                sink = jnp.where(head == g, sinks_ref[hk * SWA_GROUP + g], sink)
            bias_sc[hk] = jnp.where(valid, -slope * dist.astype(F32), NEG_BIG)
            sink_sc[hk] = sink

    kcat = jnp.concatenate([kp_ref[...], kc_ref[...]], axis=0)
    vcat = jnp.concatenate([vp_ref[...], vc_ref[...]], axis=0)
    zeros = jnp.zeros((HEAD_DIM, WINDOW), F32)
    for j in range(tq // WINDOW):
        kj = kcat[j * WINDOW:(j + 2) * WINDOW]
        vt = vcat[j * WINDOW:(j + 2) * WINDOW].astype(F32).T
        qt = q_ref[j * WINDOW:(j + 1) * WINDOW, :].astype(F32).T
        for hk in range(SWA_KV_HEADS):
            cols = []
            for g in range(SWA_GROUP):
                hq = hk * SWA_GROUP + g
                qh = qt[hq * HEAD_DIM:(hq + 1) * HEAD_DIM]
                cols.append(jnp.concatenate([qh, zeros] if hk == 0 else [zeros, qh], axis=0))
            q8 = jnp.concatenate(cols, axis=1).astype(BF16)
            s = jnp.dot(kj, q8, preferred_element_type=F32) + bias_sc[hk]
            if j == 0:
                key = lax.broadcasted_iota(jnp.int32, s.shape, 0)
                s = jnp.where((key < WINDOW) & (qi == 0), NEG_BIG, s)
            sink = sink_sc[hk]
            m = jnp.maximum(jnp.max(s, axis=0, keepdims=True), sink)
            e = jnp.exp2(s - m)
            denom = jnp.sum(e, axis=0, keepdims=True) + jnp.exp2(sink - m)
            v_hk = vt[hk * HEAD_DIM:(hk + 1) * HEAD_DIM].astype(BF16)
            ot = jnp.dot(v_hk, e.astype(BF16), preferred_element_type=F32) / denom
            for g in range(SWA_GROUP):
                hq = hk * SWA_GROUP + g
                ot_sc[hq * HEAD_DIM:(hq + 1) * HEAD_DIM, :] = ot[:, g * WINDOW:(g + 1) * WINDOW]
        o_ref[j * WINDOW:(j + 1) * WINDOW, :] = ot_sc[...].T.astype(o_ref.dtype)


def _swa_attention(proj, slopes, sinks, *, batch, seq, tq):
    n = proj.shape[0]
    nq = seq // tq
    sub = tq // WINDOW
    nwin = seq // WINDOW
    kern = functools.partial(_swa_kernel, tq=tq)
    cq = _COL_SQ // (SWA_HEADS * HEAD_DIM)
    ck, cv = _COL_SK // LANES, _COL_SV // LANES
    prev = lambda c: (lambda b, i: (b * nwin + jnp.maximum(i * sub - 1, 0), c))
    cur = lambda c: (lambda b, i: (b * nq + i, c))
    smem = pl.BlockSpec(memory_space=pltpu.SMEM)
    return pl.pallas_call(
        kern,
        out_shape=jax.ShapeDtypeStruct((n, SWA_HEADS * HEAD_DIM), BF16),
        grid=(batch, nq),
        in_specs=[smem, smem,
                  pl.BlockSpec((tq, SWA_HEADS * HEAD_DIM), cur(cq)),
                  pl.BlockSpec((WINDOW, LANES), prev(ck)),
                  pl.BlockSpec((tq, LANES), cur(ck)),
                  pl.BlockSpec((WINDOW, LANES), prev(cv)),
                  pl.BlockSpec((tq, LANES), cur(cv))],
        out_specs=pl.BlockSpec((tq, SWA_HEADS * HEAD_DIM), lambda b, i: (b * nq + i, 0)),
        scratch_shapes=[pltpu.VMEM((SWA_KV_HEADS, 2 * WINDOW, SWA_GROUP * WINDOW), F32),
                        pltpu.VMEM((SWA_KV_HEADS, 1, SWA_GROUP * WINDOW), F32),
                        pltpu.VMEM((SWA_HEADS * HEAD_DIM, WINDOW), F32)],
        compiler_params=_params(("arbitrary", "arbitrary")),
        name="swa_attn",
    )(slopes, sinks, proj, proj, proj, proj, proj)


def _layer_norm(y, g, b):
    mu = jnp.mean(y, axis=1, keepdims=True)
    var = jnp.mean(jnp.square(y - mu), axis=1, keepdims=True)
    return (y - mu) * lax.rsqrt(var + LN_EPS) * g + b


def _merge_kernel(x_ref, od_ref, os_ref, ga_ref, gb_ref, p_ref,
                  wa_ref, wb_ref, wo_ref, bo_ref, g1_ref, b1_ref, wpg_ref, wpp_ref,
                  x1_ref, r_ref, *, dn_alpha):
    a = jnp.dot(od_ref[...], wa_ref[...], preferred_element_type=F32)
    b = jnp.dot(os_ref[...], wb_ref[...], preferred_element_type=F32)
    merged = (jax.nn.sigmoid(ga_ref[...].astype(F32)) * a
              + jax.nn.sigmoid(gb_ref[...].astype(F32)) * b)
    mix = jnp.dot(merged.astype(BF16), wo_ref[...], preferred_element_type=F32) + bo_ref[...]
    x1 = _layer_norm(dn_alpha * x_ref[...] + mix, g1_ref[...], b1_ref[...])
    x1_ref[...] = x1
    gate = jax.nn.sigmoid(jnp.dot(x1.astype(BF16), wpg_ref[...], preferred_element_type=F32))
    ple = gate * jnp.dot(p_ref[...].astype(BF16), wpp_ref[...], preferred_element_type=F32)
    r_ref[...] = dn_alpha * x1 + ple


def _merge(x, od, osw, proj, p, wa, wb, wo, bo, g1, b1, wpg, wpp, *, dn_alpha, tm):
    n, d = x.shape
    pd = p.shape[1]
    row = lambda c: (lambda i: (i, c))
    full = lambda shape: pl.BlockSpec(shape, lambda i: (0, 0))
    kern = functools.partial(_merge_kernel, dn_alpha=dn_alpha)
    return pl.pallas_call(
        kern,
        out_shape=(jax.ShapeDtypeStruct((n, d), F32), jax.ShapeDtypeStruct((n, d), F32)),
        grid=(n // tm,),
        in_specs=[pl.BlockSpec((tm, d), row(0)),
                  pl.BlockSpec((tm, d), row(0)),
                  pl.BlockSpec((tm, d), row(0)),
                  pl.BlockSpec((tm, d), row(_COL_GA // d)),
                  pl.BlockSpec((tm, d), row(_COL_GB // d)),
                  pl.BlockSpec((tm, pd), row(0)),
                  full((d, d)), full((d, d)), full((d, d)), full((1, d)),
                  full((1, d)), full((1, d)), full((d, d)), full((pd, d))],
        out_specs=(pl.BlockSpec((tm, d), row(0)), pl.BlockSpec((tm, d), row(0))),
        compiler_params=_params(("parallel",)),
        name="merge_ln1",
    )(x, od, osw, proj, proj, p, wa, wb, wo, bo, g1, b1, wpg, wpp)


def _router_kernel(x_ref, w_ref, b_ref, code_ref, gate_ref, cnt_ref, carry, *, tm):
    @pl.when(pl.program_id(0) == 0)
    def _():
        carry[...] = jnp.zeros(carry.shape, F32)

    logits = jnp.dot(x_ref[...], w_ref[...], preferred_element_type=F32,
                     precision=lax.Precision.HIGHEST) + b_ref[...]
    lane = lax.broadcasted_iota(jnp.int32, logits.shape, 1)
    lanef = lane.astype(F32)
    work = logits
    tops, idxs = [], []
    onehot = jnp.zeros(logits.shape, F32)
    for _ in range(TOP_K):
        m = jnp.max(work, axis=1, keepdims=True)
        idx = jnp.min(jnp.where(work == m, lanef, float(LANES)), axis=1, keepdims=True)
        sel = lanef == idx
        onehot = jnp.where(sel, 1.0, onehot)
        work = jnp.where(sel, -jnp.inf, work)
        tops.append(m)
        idxs.append(idx)
    es = [jnp.exp(t - tops[0]) for t in tops]
    denom = es[0] + es[1] + es[2] + es[3]
    r = lax.broadcasted_iota(jnp.int32, (tm, tm), 0)
    c = lax.broadcasted_iota(jnp.int32, (tm, tm), 1)
    tri = jnp.where(c < r, 1.0, 0.0).astype(BF16)
    before = jnp.dot(tri, onehot.astype(BF16), preferred_element_type=F32) + carry[0:1, :]
    code = jnp.zeros(logits.shape, jnp.int32)
    gate = jnp.zeros(logits.shape, F32)
    for k in range(TOP_K):
        rank = jnp.sum(jnp.where(lanef == idxs[k], before, 0.0), axis=1, keepdims=True)
        ck = (idxs[k] * float(1 << RANK_BITS) + rank).astype(jnp.int32)
        code = jnp.where(lane == k, ck, code)
        gate = jnp.where(lane == k, es[k] / denom, gate)
    code_ref[...] = code
    gate_ref[...] = gate
    carry[0:1, :] = carry[0:1, :] + jnp.sum(onehot, axis=0, keepdims=True)
    cnt_ref[...] = carry[...]


def _router(x1, w, b, *, tm):
    n, d = x1.shape
    kern = functools.partial(_router_kernel, tm=tm)
    return pl.pallas_call(
        kern,
        out_shape=(jax.ShapeDtypeStruct((n, LANES), jnp.int32),
                   jax.ShapeDtypeStruct((n, LANES), F32),
                   jax.ShapeDtypeStruct((8, LANES), F32)),
        grid=(n // tm,),
        in_specs=[pl.BlockSpec((tm, d), lambda i: (i, 0)),
                  pl.BlockSpec((d, LANES), lambda i: (0, 0)),
                  pl.BlockSpec((1, LANES), lambda i: (0, 0))],
        out_specs=(pl.BlockSpec((tm, LANES), lambda i: (i, 0)),
                   pl.BlockSpec((tm, LANES), lambda i: (i, 0)),
                   pl.BlockSpec((8, LANES), lambda i: (0, 0))),
        scratch_shapes=[pltpu.VMEM((8, LANES), F32)],
        compiler_params=_params(("arbitrary",)),
        name="router",
    )(x1, w, b)


def _slot(code, off_ref):
    return off_ref[code >> RANK_BITS] + (code & ((1 << RANK_BITS) - 1))


def _dispatch_kernel(off_ref, tail_ref, codes_ref, x_ref, xs_ref, codes_smem, zeros, csem, sem, zsem,
                     *, rows):
    i = pl.program_id(0)
    n_steps = pl.num_programs(0)
    n_assign = rows * TOP_K

    def codes_load(step):
        return pltpu.make_async_copy(codes_ref.at[pl.ds(step * n_assign, n_assign)],
                                     codes_smem, csem)

    @pl.when(i == 0)
    def _():
        codes_load(0).start()
        zeros[...] = jnp.zeros(zeros.shape, zeros.dtype)

        def fill(e):
            first = pl.multiple_of(tail_ref[e], MOE_BLOCK)
            return pltpu.make_async_copy(zeros, xs_ref.at[pl.ds(first, MOE_BLOCK)], zsem)

        def start(e, carry):
            @pl.when(tail_ref[e] >= 0)
            def _():
                fill(e).start()
            return carry

        def finish(e, carry):
            @pl.when(tail_ref[e] >= 0)
            def _():
                fill(e).wait()
            return carry

        def unused(b):
            first = pl.multiple_of(b * MOE_BLOCK, MOE_BLOCK)
            return pltpu.make_async_copy(zeros, xs_ref.at[pl.ds(first, MOE_BLOCK)], zsem)

        n_used = tail_ref[N_EXPERTS]
        n_blocks = xs_ref.shape[0] // MOE_BLOCK
        lax.fori_loop(0, N_EXPERTS, start, 0)
        lax.fori_loop(n_used, n_blocks, lambda b, c: (unused(b).start(), c)[1], 0)
        lax.fori_loop(0, N_EXPERTS, finish, 0)
        lax.fori_loop(n_used, n_blocks, lambda b, c: (unused(b).wait(), c)[1], 0)

    codes_load(i).wait()

    def issue(t, carry):
        src = x_ref.at[pl.ds(t, 1)]
        for k in range(TOP_K):
            slot = _slot(codes_smem[t * TOP_K + k], off_ref)
            pltpu.make_async_copy(src, xs_ref.at[pl.ds(slot, 1)], sem).start()
        return carry

    lax.fori_loop(0, rows, issue, 0, unroll=2)

    @pl.when(i + 1 < n_steps)
    def _():
        codes_load(i + 1).start()

    pltpu.make_async_copy(xs_ref.at[pl.ds(0, n_assign)], xs_ref.at[pl.ds(0, n_assign)], sem).wait()


def _dispatch(off, tail, codes, x1, n_slots, *, rows):
    n, d = x1.shape
    kern = functools.partial(_dispatch_kernel, rows=rows)
    any_spec = pl.BlockSpec(memory_space=pl.ANY)
    return pl.pallas_call(
        kern,
        out_shape=jax.ShapeDtypeStruct((n_slots, d), x1.dtype),
        grid_spec=pltpu.PrefetchScalarGridSpec(
            num_scalar_prefetch=2,
            grid=(n // rows,),
            in_specs=[any_spec, pl.BlockSpec((rows, d), lambda i, off, tail: (i, 0))],
            out_specs=any_spec,
            scratch_shapes=[pltpu.SMEM((rows * TOP_K,), jnp.int32),
                            pltpu.VMEM((MOE_BLOCK, d), x1.dtype),
                            pltpu.SemaphoreType.DMA, pltpu.SemaphoreType.DMA,
                            pltpu.SemaphoreType.DMA]),
        compiler_params=_dma_params(("arbitrary",)),
        name="moe_dispatch",
    )(off, tail, codes, x1)


def _expert_kernel(be_ref, nu_ref, xs_ref, wgu_ref, bgu_ref, wd_ref, bd_ref, y_ref, wgu_b, wd_b):
    j = pl.program_id(0)
    dff = wd_ref.shape[2]

    @pl.when(j < nu_ref[0])
    def _():
        @pl.when((j == 0) | (be_ref[j] != be_ref[jnp.maximum(j - 1, 0)]))
        def _():
            wgu_b[...] = wgu_ref[0, 0].astype(BF16)
            wd_b[...] = wd_ref[0, 0].astype(BF16)

        gu = jnp.dot(xs_ref[...].astype(BF16), wgu_b[...], preferred_element_type=F32) + bgu_ref[0, 0]
        gate = jnp.minimum(gu[:, :dff], SWIGLU_LIMIT)
        up = jnp.clip(gu[:, dff:], -SWIGLU_LIMIT, SWIGLU_LIMIT)
        act = (up + 1.0) * (gate * jax.nn.sigmoid(SWIGLU_ALPHA * gate))
        y_ref[...] = jnp.dot(act.astype(BF16), wd_b[...], preferred_element_type=F32) + bd_ref[0, 0]

    @pl.when(j >= nu_ref[0])
    def _():
        y_ref[...] = jnp.zeros(y_ref.shape, y_ref.dtype)


def _experts(block_e, n_used, xs, wgu, bgu, wd, bd, *, layer):
    n_slots, d = xs.shape
    dff = wd.shape[2]
    nb = n_slots // MOE_BLOCK
    blk = lambda j, be, nu: jnp.minimum(j, nu[0] - 1)
    row = lambda j, be, nu: (blk(j, be, nu), 0)
    wsel = lambda j, be, nu: (layer, be[blk(j, be, nu)], 0, 0)
    return pl.pallas_call(
        _expert_kernel,
        out_shape=jax.ShapeDtypeStruct((n_slots, d), F32),
        grid_spec=pltpu.PrefetchScalarGridSpec(
            num_scalar_prefetch=2,
            grid=(nb,),
            in_specs=[pl.BlockSpec((MOE_BLOCK, d), row),
                      pl.BlockSpec((1, 1, d, 2 * dff), wsel),
                      pl.BlockSpec((1, 1, 1, 2 * dff), wsel),
                      pl.BlockSpec((1, 1, dff, d), wsel),
                      pl.BlockSpec((1, 1, 1, d), wsel)],
            out_specs=pl.BlockSpec((MOE_BLOCK, d), lambda j, be, nu: (j, 0)),
            scratch_shapes=[pltpu.VMEM((d, 2 * dff), BF16), pltpu.VMEM((dff, d), BF16)]),
        compiler_params=_params(("arbitrary",)),
        name="moe_experts",
    )(block_e, n_used, xs, wgu, bgu, wd, bd)


def _combine_kernel(off_ref, codes_ref, gate_ref, r_ref, y_ref, g2_ref, b2_ref,
                    x2_ref, xb_ref, codes_smem, ybuf0, ybuf1, csem, sem0, sem1, *, rows):
    i = pl.program_id(0)
    n_steps = pl.num_programs(0)
    n_assign = rows * TOP_K

    def codes_load(step):
        return pltpu.make_async_copy(codes_ref.at[pl.ds(step * n_assign, n_assign)],
                                     codes_smem, csem)

    def gather(step, ybuf, sem):
        def issue(t, carry):
            for k in range(TOP_K):
                slot = _slot(codes_smem[t * TOP_K + k], off_ref)
                pltpu.make_async_copy(y_ref.at[pl.ds(slot, 1)],
                                      ybuf.at[pl.ds(k * rows + t, 1)], sem).start()
            return carry

        lax.fori_loop(0, rows, issue, 0, unroll=2)

        @pl.when(step + 1 < n_steps)
        def _():
            codes_load(step + 1).start()

    @pl.when(i == 0)
    def _():
        first = codes_load(0)
        first.start()
        first.wait()
        gather(0, ybuf0, sem0)

    def step(ybuf, sem, ybuf_next, sem_next):
        @pl.when(i + 1 < n_steps)
        def _():
            codes_load(i + 1).wait()
            gather(i + 1, ybuf_next, sem_next)

        pltpu.make_async_copy(y_ref.at[pl.ds(0, n_assign)], ybuf, sem).wait()
        gate = gate_ref[...]
        acc = r_ref[...]
        for k in range(TOP_K):
            acc = acc + gate[:, k:k + 1] * ybuf[k * rows:(k + 1) * rows, :]
        x2 = _layer_norm(acc, g2_ref[...], b2_ref[...])
        x2_ref[...] = x2
        xb_ref[...] = x2.astype(BF16)

    @pl.when(i % 2 == 0)
    def _():
        step(ybuf0, sem0, ybuf1, sem1)

    @pl.when(i % 2 == 1)
    def _():
        step(ybuf1, sem1, ybuf0, sem0)


def _combine(off, codes, gates, r, y, g2, b2, *, rows):
    n, d = r.shape
    kern = functools.partial(_combine_kernel, rows=rows)
    any_spec = pl.BlockSpec(memory_space=pl.ANY)
    return pl.pallas_call(
        kern,
        out_shape=(jax.ShapeDtypeStruct((n, d), F32), jax.ShapeDtypeStruct((n, d), BF16)),
        grid_spec=pltpu.PrefetchScalarGridSpec(
            num_scalar_prefetch=1,
            grid=(n // rows,),
            in_specs=[any_spec,
                      pl.BlockSpec((rows, LANES), lambda i, off: (i, 0)),
                      pl.BlockSpec((rows, d), lambda i, off: (i, 0)),
                      any_spec,
                      pl.BlockSpec((1, d), lambda i, off: (0, 0)),
                      pl.BlockSpec((1, d), lambda i, off: (0, 0))],
            out_specs=(pl.BlockSpec((rows, d), lambda i, off: (i, 0)),
                       pl.BlockSpec((rows, d), lambda i, off: (i, 0))),
            scratch_shapes=[pltpu.SMEM((rows * TOP_K,), jnp.int32),
                            pltpu.VMEM((rows * TOP_K, d), F32), pltpu.VMEM((rows * TOP_K, d), F32),
                            pltpu.SemaphoreType.DMA, pltpu.SemaphoreType.DMA,
                            pltpu.SemaphoreType.DMA]),
        compiler_params=_dma_params(("arbitrary",)),
        name="moe_combine_ln2",
    )(off, codes, gates, r, y, g2, b2)


def _alibi_slopes(n_heads):
    h = jnp.arange(1, n_heads + 1, dtype=F32)
    return jnp.exp2(-8.0 * h / n_heads)


def _block_size(n, target):
    t = min(n, target)
    while n % t:
        t //= 2
    return t


def kernel(x, p, w_in, b_in, lambda_q1, lambda_k1, lambda_q2, lambda_k2, subln_w, sinks,
           w_br_diff, w_br_swa, w_out, b_out, ln1_g, ln1_b, w_router, b_router,
           w_gate_up, b_gate_up, w_down, b_down, w_ple_gate, w_ple_proj, ln2_g, ln2_b):
    batch, seq, d = x.shape
    depth = w_in.shape[0]
    n = batch * seq
    dn_alpha = (2 * depth) ** 0.25
    n_assign = n * TOP_K
    n_blocks = n_assign // MOE_BLOCK + N_EXPERTS + 1
    n_slots = n_blocks * MOE_BLOCK

    n_in = w_in.shape[2]
    perm = jnp.concatenate([jnp.arange(n_in - 2 * d, n_in), jnp.arange(0, n_in - 2 * d)])
    log2e = math.log2(math.e)
    diff_slopes = _alibi_slopes(DIFF_HEADS) * log2e
    swa_slopes = _alibi_slopes(SWA_HEADS) * log2e

    tm_lin = _block_size(n, 1024)
    tq_diff = _block_size(seq, 1024)
    tq_swa = _block_size(seq, 512)
    tm_merge = _block_size(n, 512)
    tm_router = _block_size(n, 512)
    rows_moe = _block_size(n, 256)

    xf = x.reshape(n, d)
    xb = xf.astype(BF16)
    for i in range(depth):
        lam_init = 0.8 - 0.6 * math.exp(-0.3 * i)
        col_scale = jnp.ones((n_in,), F32).at[_COL_DQ:_COL_DK].set(HEAD_DIM ** -0.5 * log2e)
        col_scale = col_scale.at[_COL_SQ:_COL_SK].set(HEAD_DIM ** -0.5 * log2e)
        w_in_b = (jnp.take(w_in[i], perm, axis=1) * col_scale).astype(BF16)
        b_in_p = (jnp.take(b_in[i], perm) * col_scale)[None, :]
        proj = _linear(xb, w_in_b, b_in_p, tm=tm_lin, tn=1280)

        od = _diff_attention(proj, diff_slopes, _diff_query_aug(diff_slopes, tq_diff),
                             lambda_q1[i][None, :], lambda_k1[i][None, :],
                             lambda_q2[i][None, :], lambda_k2[i][None, :],
                             subln_w[i][None, :], batch=batch, seq=seq,
                             lam_init=lam_init, tq=tq_diff)
        osw = _swa_attention(proj, swa_slopes, sinks[i].astype(F32) * log2e,
                             batch=batch, seq=seq, tq=tq_swa)

        x1, r = _merge(xf, od, osw, proj, p[i].reshape(n, -1),
                       w_br_diff[i].astype(BF16), w_br_swa[i].astype(BF16),
                       w_out[i].astype(BF16), b_out[i][None, :],
                       ln1_g[i][None, :], ln1_b[i][None, :],
                       w_ple_gate[i].astype(BF16), w_ple_proj[i].astype(BF16),
                       dn_alpha=dn_alpha, tm=tm_merge)

        wr = jnp.zeros((d, LANES), F32).at[:, :N_EXPERTS].set(w_router[i])
        br = jnp.full((1, LANES), NEG_BIG, F32).at[0, :N_EXPERTS].set(b_router[i])
        code, gates, cnt = _router(x1, wr, br, tm=tm_router)

        counts = cnt[0, :N_EXPERTS].astype(jnp.int32)
        padded = (counts + MOE_BLOCK - 1) // MOE_BLOCK * MOE_BLOCK
        padded_end = jnp.cumsum(padded)
        off = (padded_end - padded).astype(jnp.int32)
        block_start = jnp.arange(n_blocks, dtype=jnp.int32) * MOE_BLOCK
        block_e = jnp.minimum(
            jnp.sum((block_start[:, None] >= padded_end[None, :]).astype(jnp.int32), axis=1),
            N_EXPERTS - 1)
        codes = code[:, :TOP_K].reshape(n_assign)
        n_used = (padded_end[-1:] // MOE_BLOCK).astype(jnp.int32)
        tail = jnp.concatenate([jnp.where(padded > 0, padded_end - MOE_BLOCK, -1).astype(jnp.int32),
                                n_used])

        xs = _dispatch(off, tail, codes, x1, n_slots, rows=rows_moe)
        y = _experts(block_e, n_used, xs, w_gate_up, b_gate_up[:, :, None, :],
                     w_down, b_down[:, :, None, :], layer=i)
        xf, xb = _combine(off, codes, gates, r, y, ln2_g[i][None, :], ln2_b[i][None, :],
                          rows=rows_moe)
    return xf.reshape(batch, seq, d)
```

```python
import functools
import math

import jax
import jax.numpy as jnp
from jax import lax
from jax.experimental import pallas as pl
from jax.experimental.pallas import tpu as pltpu

F32 = jnp.float32
BF16 = jnp.bfloat16

HEAD_DIM = 64
DIFF_HEADS = 8
SWA_HEADS = 16
SWA_KV_HEADS = 2
SWA_GROUP = SWA_HEADS // SWA_KV_HEADS
WINDOW = 128
N_EXPERTS = 32
TOP_K = 4
MOE_BLOCK = 512
SWIGLU_LIMIT = 7.0
SWIGLU_ALPHA = 1.702
LN_EPS = 1e-5
RMS_EPS = 1e-5
NEG_BIG = -1e30

LANES = 128
VMEM_LIMIT = 56 * 1024 * 1024
RANK_BITS = 16

_COL_GA, _COL_GB, _COL_DQ, _COL_DK, _COL_DV, _COL_SQ, _COL_SK, _COL_SV = (
    0, 1024, 2048, 3072, 4096, 5120, 6144, 6272)


def _params(semantics):
    return pltpu.CompilerParams(dimension_semantics=semantics,
                                vmem_limit_bytes=VMEM_LIMIT)


def _dma_params(semantics):
    return pltpu.CompilerParams(dimension_semantics=semantics,
                                vmem_limit_bytes=VMEM_LIMIT,
                                disable_bounds_checks=True)


def _linear_kernel(x_ref, w_ref, b_ref, o_ref):
    acc = jnp.dot(x_ref[...], w_ref[...], preferred_element_type=F32)
    o_ref[...] = (acc + b_ref[...]).astype(o_ref.dtype)


def _linear(x, w, b, *, tm, tn):
    n, k = x.shape
    nout = w.shape[1]
    return pl.pallas_call(
        _linear_kernel,
        out_shape=jax.ShapeDtypeStruct((n, nout), BF16),
        grid=(n // tm, nout // tn),
        in_specs=[pl.BlockSpec((tm, k), lambda i, j: (i, 0)),
                  pl.BlockSpec((k, tn), lambda i, j: (0, j)),
                  pl.BlockSpec((1, tn), lambda i, j: (0, j))],
        out_specs=pl.BlockSpec((tm, tn), lambda i, j: (i, j)),
        compiler_params=_params(("parallel", "arbitrary")),
        name="in_proj",
    )(x, w, b)


N_AUG = 6
STRIP = 256
DV = 2 * HEAD_DIM
ONES_ROWS = 16


def _key_aug(tk, first_lane):
    r = lax.broadcasted_iota(jnp.int32, (tk, LANES), 0)
    lane = lax.broadcasted_iota(jnp.int32, (tk, LANES), 1) - first_lane
    hi = ((r >> 7) << 7).astype(F32)
    lo = (r & 127).astype(F32)
    return jnp.where((lane >= 0) & (lane < 3), hi,
                     jnp.where((lane >= 3) & (lane < N_AUG), lo, 0.0))


def _diff_attn_kernel(slopes_ref, lq1_ref, lk1_ref, lq2_ref, lk2_ref, qaug_ref,
                      q_ref, k_ref, v_ref, w_ref, o_ref,
                      k1a_sc, k2a_sc, vt_sc, a1, a2, st, sa1, sa2, sb1, sb2, *, tq, tk, lam_init):
    h = pl.program_id(1)
    qi = pl.program_id(2)
    slope = slopes_ref[h]
    n_chunks = k1a_sc.shape[0]

    @pl.when(qi == 0)
    def _():
        lane = lax.broadcasted_iota(jnp.int32, (tk, LANES), 1)
        aug1 = _key_aug(tk, HEAD_DIM).astype(BF16)
        aug2 = _key_aug(tk, 0).astype(BF16)

        def build(c, carry):
            rows = pl.ds(pl.multiple_of(c * tk, tk), tk)
            k = k_ref[rows, :]
            k1a_sc[c] = jnp.where(lane < HEAD_DIM, k, aug1)
            k2a_sc[c] = jnp.where(lane >= HEAD_DIM, k, aug2)
            vt_sc[c, 0:DV, :] = v_ref[rows, :].astype(F32).T.astype(BF16)
            vt_sc[c, DV:DV + ONES_ROWS, :] = jnp.where(
                lax.broadcasted_iota(jnp.int32, (ONES_ROWS, tk), 0) == 0, 1.0, 0.0).astype(BF16)
            return carry

        lax.fori_loop(0, n_chunks, build, 0)

    qt = q_ref[...].astype(F32).T
    row = lax.broadcasted_iota(jnp.int32, qt.shape, 0)
    qt1 = jnp.where(row < HEAD_DIM, qt, qaug_ref[0, 0]).astype(BF16)
    qt2 = jnp.where(row >= HEAD_DIM, qt, qaug_ref[0, 1]).astype(BF16)
    a1[...] = jnp.zeros(a1.shape, F32)
    a2[...] = jnp.zeros(a2.shape, F32)

    M1, M2, XA1, XA2 = range(4)
    for r_ in (M1, M2):
        st[r_:r_ + 1, :] = jnp.full((1, tq), NEG_BIG, F32)

    maps = ((k1a_sc, qt1, a1, M1, XA1), (k2a_sc, qt2, a2, M2, XA2))
    tc = min(tq, STRIP)
    pieces = [(mp, slice(h * tc, (h + 1) * tc)) for h in range(tq // tc) for mp in range(2)]

    def scores(j, mp, cols, dst, key0=None):
        s = jnp.dot(maps[mp][0][j], maps[mp][1][:, cols], preferred_element_type=F32)
        if key0 is not None:
            krow = lax.broadcasted_iota(jnp.int32, s.shape, 0) + key0
            qcol = lax.broadcasted_iota(jnp.int32, s.shape, 1) + cols.start
            s = jnp.where(krow <= qcol, s, NEG_BIG)
        dst[mp][:, cols] = s
        return jnp.max(s, axis=0, keepdims=True)

    def accumulate(j, mp, cols, src, mx):
        _, _, a_sc, mr, _ = maps[mp]
        c = slope * jnp.full((1, tc), j * tk - qi * tq, jnp.int32).astype(F32)
        m = st[mr:mr + 1, cols]
        m_new = jnp.maximum(m, mx + c)
        alpha = jnp.exp2(m - m_new)
        p = jnp.exp2(src[mp][:, cols] - (m_new - c))
        st[mr:mr + 1, cols] = m_new
        a_sc[:, cols] = alpha * a_sc[:, cols] + jnp.dot(vt_sc[j], p.astype(BF16),
                                                        preferred_element_type=F32)

    bufs = ((sa1, sa2), (sb1, sb2))

    def park(maxima):
        for (mp, cols), x in zip(pieces, maxima):
            xr = maps[mp][4]
            st[xr:xr + 1, cols] = x

    def parked():
        return [st[maps[mp][4]:maps[mp][4] + 1, cols] for mp, cols in pieces]

    def step(s_blk, s_dst, p_blk, p_src, p_max, s_pieces=None, p_pieces=None):
        s_pieces = [(mp, cols, None) for mp, cols in pieces] if s_pieces is None else s_pieces
        p_pieces = pieces if p_pieces is None else p_pieces
        out = []
        for k in range(max(len(s_pieces), len(p_pieces))):
            if k < len(s_pieces):
                mp, cols, key0 = s_pieces[k]
                out.append(scores(s_blk, mp, cols, s_dst, key0))
            if k < len(p_pieces):
                mp, cols = p_pieces[k]
                accumulate(p_blk, mp, cols, p_src, p_max[k])
        return out

    def diag_pieces(d):
        out = []
        for mp, cols in pieces:
            if d * tk + tk - 1 <= cols.start:
                out.append((mp, cols, None))
            elif d * tk <= cols.stop - 1:
                out.append((mp, cols, d * tk))
        return out

    d0, d1 = diag_pieces(0), diag_pieces(1)
    d0_cols = [(mp, cols) for mp, cols, _ in d0]
    d1_cols = [(mp, cols) for mp, cols, _ in d1]
    first_diag = 2 * qi

    def finish(cur, nxt, x_prev=None, prev_blk=None):
        if x_prev is None:
            x0 = [scores(first_diag, mp, cols, nxt, key0) for mp, cols, key0 in d0]
        else:
            x0 = step(first_diag, nxt, prev_blk, cur, x_prev, s_pieces=d0)
        x1 = step(first_diag + 1, cur, first_diag, nxt, x0, s_pieces=d1, p_pieces=d0_cols)
        for (mp, cols), x in zip(d1_cols, x1):
            accumulate(first_diag + 1, mp, cols, cur, x)

    @pl.when(qi == 0)
    def _():
        finish(bufs[0], bufs[1])

    @pl.when(qi > 0)
    def _():
        park([scores(0, mp, cols, bufs[0]) for mp, cols in pieces])

        def pair(i, carry):
            j = 2 * i
            x = parked()
            x = step(j + 1, bufs[1], j, bufs[0], x)
            x = step(j + 2, bufs[0], j + 1, bufs[1], x)
            park(x)
            return carry

        lax.fori_loop(0, qi - 1, pair, 0)
        j = 2 * (qi - 1)
        x = step(j + 1, bufs[1], j, bufs[0], parked())
        finish(bufs[1], bufs[0], x_prev=x, prev_blk=j + 1)

    lam = (jnp.exp(jnp.sum(lq1_ref[...] * lk1_ref[...], axis=1, keepdims=True))
           - jnp.exp(jnp.sum(lq2_ref[...] * lk2_ref[...], axis=1, keepdims=True))
           + lam_init)
    o1 = a1[0:DV, :] / a1[DV:DV + 1, :]
    o2 = a2[0:DV, :] / a2[DV:DV + 1, :]
    o = (o1 - lam * o2).T
    y = o * lax.rsqrt(jnp.mean(jnp.square(o), axis=1, keepdims=True) + RMS_EPS)
    y = (y * w_ref[...]) * (1.0 - lam_init)
    o_ref[...] = y.astype(o_ref.dtype)


def _diff_attention(proj, slopes_l2, qaug, lq1, lk1, lq2, lk2, subln_w, *, batch, seq, lam_init, tq):
    n = proj.shape[0]
    nq = seq // tq
    tk = tq // 2
    assert tk % STRIP == 0 and tk % LANES == 0
    nk = seq // tk
    kern = functools.partial(_diff_attn_kernel, tq=tq, tk=tk, lam_init=lam_init)
    vec = pl.BlockSpec((1, HEAD_DIM), lambda b, h, i: (0, 0))
    cq, ck, cv = _COL_DQ // LANES, _COL_DK // LANES, _COL_DV // LANES
    return pl.pallas_call(
        kern,
        out_shape=jax.ShapeDtypeStruct((n, DIFF_HEADS * 2 * HEAD_DIM), BF16),
        grid=(batch, DIFF_HEADS, nq),
        in_specs=[pl.BlockSpec(memory_space=pltpu.SMEM),
                  vec, vec, vec, vec,
                  pl.BlockSpec((1, 2, LANES, tq), lambda b, h, i: (h, 0, 0, 0)),
                  pl.BlockSpec((tq, LANES), lambda b, h, i: (b * nq + i, cq + h)),
                  pl.BlockSpec((seq, LANES), lambda b, h, i: (b, ck + h)),
                  pl.BlockSpec((seq, LANES), lambda b, h, i: (b, cv + h)),
                  pl.BlockSpec((1, 2 * HEAD_DIM), lambda b, h, i: (0, 0))],
        out_specs=pl.BlockSpec((tq, LANES), lambda b, h, i: (b * nq + i, h)),
        scratch_shapes=[pltpu.VMEM((nk, tk, LANES), BF16), pltpu.VMEM((nk, tk, LANES), BF16),
                        pltpu.VMEM((nk, DV + ONES_ROWS, tk), BF16),
                        pltpu.VMEM((DV + ONES_ROWS, tq), F32), pltpu.VMEM((DV + ONES_ROWS, tq), F32),
                        pltpu.VMEM((8, tq), F32)] + [pltpu.VMEM((tk, tq), F32)] * 4,
        compiler_params=_params(("arbitrary", "arbitrary", "arbitrary")),
        name="diff_attn",
    )(slopes_l2, lq1, lk1, lq2, lk2, qaug, proj, proj, proj, subln_w)


def _diff_query_aug(slopes_l2, tq):
    s0 = slopes_l2.astype(BF16).astype(F32)
    s1 = (slopes_l2 - s0).astype(BF16).astype(F32)
    s2 = (slopes_l2 - s0 - s1).astype(BF16).astype(F32)
    parts = jnp.stack([s0, s1, s2, s0, s1, s2], axis=1)
    n_heads = slopes_l2.shape[0]
    cols = jnp.zeros((n_heads, 2, LANES), F32)
    cols = cols.at[:, 0, HEAD_DIM:HEAD_DIM + N_AUG].set(parts)
    cols = cols.at[:, 1, 0:N_AUG].set(parts)
    return jnp.broadcast_to(cols[:, :, :, None], (n_heads, 2, LANES, tq))


def _swa_kernel(slopes_ref, sinks_ref, q_ref, kp_ref, kc_ref, vp_ref, vc_ref, o_ref,
                bias_sc, sink_sc, ot_sc, *, tq):
    qi = pl.program_id(1)
    wide = SWA_GROUP * WINDOW

    @pl.when((pl.program_id(0) == 0) & (qi == 0))
    def _():
        key = lax.broadcasted_iota(jnp.int32, (2 * WINDOW, wide), 0)
        col = lax.broadcasted_iota(jnp.int32, (2 * WINDOW, wide), 1)
        dist = (col & (WINDOW - 1)) + WINDOW - key
        valid = (dist >= 0) & (dist < WINDOW)
        head = lax.broadcasted_iota(jnp.int32, (1, wide), 1) >> (WINDOW.bit_length() - 1)
        for hk in range(SWA_KV_HEADS):
            slope = jnp.zeros((1, wide), F32)
            sink = jnp.zeros((1, wide), F32)
            for g in range(SWA_GROUP):
                slope = jnp.where(head == g, slopes_ref[hk * SWA_GROUP + g], slope)
                sink = jnp.where(head == g, sinks_ref[hk * SWA_GROUP + g], sink)
            bias_sc[hk] = jnp.where(valid, -slope * dist.astype(F32), NEG_BIG)
            sink_sc[hk] = sink

    kcat = jnp.concatenate([kp_ref[...], kc_ref[...]], axis=0)
    vcat = jnp.concatenate([vp_ref[...], vc_ref[...]], axis=0)
    zeros = jnp.zeros((HEAD_DIM, WINDOW), F32)
    for j in range(tq // WINDOW):
        kj = kcat[j * WINDOW:(j + 2) * WINDOW]
        vt = vcat[j * WINDOW:(j + 2) * WINDOW].astype(F32).T
        qt = q_ref[j * WINDOW:(j + 1) * WINDOW, :].astype(F32).T
        for hk in range(SWA_KV_HEADS):
            cols = []
            for g in range(SWA_GROUP):
                hq = hk * SWA_GROUP + g
                qh = qt[hq * HEAD_DIM:(hq + 1) * HEAD_DIM]
                cols.append(jnp.concatenate([qh, zeros] if hk == 0 else [zeros, qh], axis=0))
            q8 = jnp.concatenate(cols, axis=1).astype(BF16)
            s = jnp.dot(kj, q8, preferred_element_type=F32) + bias_sc[hk]
            if j == 0:
                key = lax.broadcasted_iota(jnp.int32, s.shape, 0)
                s = jnp.where((key < WINDOW) & (qi == 0), NEG_BIG, s)
            sink = sink_sc[hk]
            m = jnp.maximum(jnp.max(s, axis=0, keepdims=True), sink)
            e = jnp.exp2(s - m)
            denom = jnp.sum(e, axis=0, keepdims=True) + jnp.exp2(sink - m)
            v_hk = vt[hk * HEAD_DIM:(hk + 1) * HEAD_DIM].astype(BF16)
            ot = jnp.dot(v_hk, e.astype(BF16), preferred_element_type=F32) / denom
            for g in range(SWA_GROUP):
                hq = hk * SWA_GROUP + g
                ot_sc[hq * HEAD_DIM:(hq + 1) * HEAD_DIM, :] = ot[:, g * WINDOW:(g + 1) * WINDOW]
        o_ref[j * WINDOW:(j + 1) * WINDOW, :] = ot_sc[...].T.astype(o_ref.dtype)


def _swa_attention(proj, slopes, sinks, *, batch, seq, tq):
    n = proj.shape[0]
    nq = seq // tq
    sub = tq // WINDOW
    nwin = seq // WINDOW
    kern = functools.partial(_swa_kernel, tq=tq)
    cq = _COL_SQ // (SWA_HEADS * HEAD_DIM)
    ck, cv = _COL_SK // LANES, _COL_SV // LANES
    prev = lambda c: (lambda b, i: (b * nwin + jnp.maximum(i * sub - 1, 0), c))
    cur = lambda c: (lambda b, i: (b * nq + i, c))
    smem = pl.BlockSpec(memory_space=pltpu.SMEM)
    return pl.pallas_call(
        kern,
        out_shape=jax.ShapeDtypeStruct((n, SWA_HEADS * HEAD_DIM), BF16),
        grid=(batch, nq),
        in_specs=[smem, smem,
                  pl.BlockSpec((tq, SWA_HEADS * HEAD_DIM), cur(cq)),
                  pl.BlockSpec((WINDOW, LANES), prev(ck)),
                  pl.BlockSpec((tq, LANES), cur(ck)),
                  pl.BlockSpec((WINDOW, LANES), prev(cv)),
                  pl.BlockSpec((tq, LANES), cur(cv))],
        out_specs=pl.BlockSpec((tq, SWA_HEADS * HEAD_DIM), lambda b, i: (b * nq + i, 0)),
        scratch_shapes=[pltpu.VMEM((SWA_KV_HEADS, 2 * WINDOW, SWA_GROUP * WINDOW), F32),
                        pltpu.VMEM((SWA_KV_HEADS, 1, SWA_GROUP * WINDOW), F32),
                        pltpu.VMEM((SWA_HEADS * HEAD_DIM, WINDOW), F32)],
        compiler_params=_params(("arbitrary", "arbitrary")),
        name="swa_attn",
    )(slopes, sinks, proj, proj, proj, proj, proj)


def _layer_norm(y, g, b):
    mu = jnp.mean(y, axis=1, keepdims=True)
    var = jnp.mean(jnp.square(y - mu), axis=1, keepdims=True)
    return (y - mu) * lax.rsqrt(var + LN_EPS) * g + b


def _merge_kernel(x_ref, od_ref, os_ref, ga_ref, gb_ref, p_ref,
                  wa_ref, wb_ref, wo_ref, bo_ref, g1_ref, b1_ref, wpg_ref, wpp_ref,
                  x1_ref, r_ref, *, dn_alpha):
    a = jnp.dot(od_ref[...], wa_ref[...], preferred_element_type=F32)
    b = jnp.dot(os_ref[...], wb_ref[...], preferred_element_type=F32)
    merged = (jax.nn.sigmoid(ga_ref[...].astype(F32)) * a
              + jax.nn.sigmoid(gb_ref[...].astype(F32)) * b)
    mix = jnp.dot(merged.astype(BF16), wo_ref[...], preferred_element_type=F32) + bo_ref[...]
    x1 = _layer_norm(dn_alpha * x_ref[...] + mix, g1_ref[...], b1_ref[...])
    x1_ref[...] = x1
    gate = jax.nn.sigmoid(jnp.dot(x1.astype(BF16), wpg_ref[...], preferred_element_type=F32))
    ple = gate * jnp.dot(p_ref[...].astype(BF16), wpp_ref[...], preferred_element_type=F32)
    r_ref[...] = dn_alpha * x1 + ple


def _merge(x, od, osw, proj, p, wa, wb, wo, bo, g1, b1, wpg, wpp, *, dn_alpha, tm):
    n, d = x.shape
    pd = p.shape[1]
    row = lambda c: (lambda i: (i, c))
    full = lambda shape: pl.BlockSpec(shape, lambda i: (0, 0))
    kern = functools.partial(_merge_kernel, dn_alpha=dn_alpha)
    return pl.pallas_call(
        kern,
        out_shape=(jax.ShapeDtypeStruct((n, d), F32), jax.ShapeDtypeStruct((n, d), F32)),
        grid=(n // tm,),
        in_specs=[pl.BlockSpec((tm, d), row(0)),
                  pl.BlockSpec((tm, d), row(0)),
                  pl.BlockSpec((tm, d), row(0)),
                  pl.BlockSpec((tm, d), row(_COL_GA // d)),
                  pl.BlockSpec((tm, d), row(_COL_GB // d)),
                  pl.BlockSpec((tm, pd), row(0)),
                  full((d, d)), full((d, d)), full((d, d)), full((1, d)),
                  full((1, d)), full((1, d)), full((d, d)), full((pd, d))],
        out_specs=(pl.BlockSpec((tm, d), row(0)), pl.BlockSpec((tm, d), row(0))),
        compiler_params=_params(("parallel",)),
        name="merge_ln1",
    )(x, od, osw, proj, proj, p, wa, wb, wo, bo, g1, b1, wpg, wpp)


def _router_kernel(x_ref, w_ref, b_ref, code_ref, gate_ref, cnt_ref, carry, *, tm):
    @pl.when(pl.program_id(0) == 0)
    def _():
        carry[...] = jnp.zeros(carry.shape, F32)

    logits = jnp.dot(x_ref[...], w_ref[...], preferred_element_type=F32,
                     precision=lax.Precision.HIGHEST) + b_ref[...]
    lane = lax.broadcasted_iota(jnp.int32, logits.shape, 1)
    lanef = lane.astype(F32)
    work = logits
    tops, idxs = [], []
    onehot = jnp.zeros(logits.shape, F32)
    for _ in range(TOP_K):
        m = jnp.max(work, axis=1, keepdims=True)
        idx = jnp.min(jnp.where(work == m, lanef, float(LANES)), axis=1, keepdims=True)
        sel = lanef == idx
        onehot = jnp.where(sel, 1.0, onehot)
        work = jnp.where(sel, -jnp.inf, work)
        tops.append(m)
        idxs.append(idx)
    es = [jnp.exp(t - tops[0]) for t in tops]
    denom = es[0] + es[1] + es[2] + es[3]
    r = lax.broadcasted_iota(jnp.int32, (tm, tm), 0)
    c = lax.broadcasted_iota(jnp.int32, (tm, tm), 1)
    tri = jnp.where(c < r, 1.0, 0.0).astype(BF16)
    before = jnp.dot(tri, onehot.astype(BF16), preferred_element_type=F32) + carry[0:1, :]
    code = jnp.zeros(logits.shape, jnp.int32)
    gate = jnp.zeros(logits.shape, F32)
    for k in range(TOP_K):
        rank = jnp.sum(jnp.where(lanef == idxs[k], before, 0.0), axis=1, keepdims=True)
        ck = (idxs[k] * float(1 << RANK_BITS) + rank).astype(jnp.int32)
        code = jnp.where(lane == k, ck, code)
        gate = jnp.where(lane == k, es[k] / denom, gate)
    code_ref[...] = code
    gate_ref[...] = gate
    carry[0:1, :] = carry[0:1, :] + jnp.sum(onehot, axis=0, keepdims=True)
    cnt_ref[...] = carry[...]


def _router(x1, w, b, *, tm):
    n, d = x1.shape
    kern = functools.partial(_router_kernel, tm=tm)
    return pl.pallas_call(
        kern,
        out_shape=(jax.ShapeDtypeStruct((n, LANES), jnp.int32),
                   jax.ShapeDtypeStruct((n, LANES), F32),
                   jax.ShapeDtypeStruct((8, LANES), F32)),
        grid=(n // tm,),
        in_specs=[pl.BlockSpec((tm, d), lambda i: (i, 0)),
                  pl.BlockSpec((d, LANES), lambda i: (0, 0)),
                  pl.BlockSpec((1, LANES), lambda i: (0, 0))],
        out_specs=(pl.BlockSpec((tm, LANES), lambda i: (i, 0)),
                   pl.BlockSpec((tm, LANES), lambda i: (i, 0)),
                   pl.BlockSpec((8, LANES), lambda i: (0, 0))),
        scratch_shapes=[pltpu.VMEM((8, LANES), F32)],
        compiler_params=_params(("arbitrary",)),
        name="router",
    )(x1, w, b)


def _slot(code, off_ref):
    return off_ref[code >> RANK_BITS] + (code & ((1 << RANK_BITS) - 1))


def _dispatch_kernel(off_ref, tail_ref, codes_ref, x_ref, xs_ref, codes_smem, zeros, csem, sem, zsem,
                     *, rows):
    i = pl.program_id(0)
    n_steps = pl.num_programs(0)
    n_assign = rows * TOP_K

    def codes_load(step):
        return pltpu.make_async_copy(codes_ref.at[pl.ds(step * n_assign, n_assign)],
                                     codes_smem, csem)

    @pl.when(i == 0)
    def _():
        codes_load(0).start()
        zeros[...] = jnp.zeros(zeros.shape, zeros.dtype)

        def fill(e):
            first = pl.multiple_of(tail_ref[e], MOE_BLOCK)
            return pltpu.make_async_copy(zeros, xs_ref.at[pl.ds(first, MOE_BLOCK)], zsem)

        def start(e, carry):
            @pl.when(tail_ref[e] >= 0)
            def _():
                fill(e).start()
            return carry

        def finish(e, carry):
            @pl.when(tail_ref[e] >= 0)
            def _():
                fill(e).wait()
            return carry

        def unused(b):
            first = pl.multiple_of(b * MOE_BLOCK, MOE_BLOCK)
            return pltpu.make_async_copy(zeros, xs_ref.at[pl.ds(first, MOE_BLOCK)], zsem)

        n_used = tail_ref[N_EXPERTS]
        n_blocks = xs_ref.shape[0] // MOE_BLOCK
        lax.fori_loop(0, N_EXPERTS, start, 0)
        lax.fori_loop(n_used, n_blocks, lambda b, c: (unused(b).start(), c)[1], 0)
        lax.fori_loop(0, N_EXPERTS, finish, 0)
        lax.fori_loop(n_used, n_blocks, lambda b, c: (unused(b).wait(), c)[1], 0)

    codes_load(i).wait()

    def issue(t, carry):
        src = x_ref.at[pl.ds(t, 1)]
        for k in range(TOP_K):
            slot = _slot(codes_smem[t * TOP_K + k], off_ref)
            pltpu.make_async_copy(src, xs_ref.at[pl.ds(slot, 1)], sem).start()
        return carry

    lax.fori_loop(0, rows, issue, 0, unroll=2)

    @pl.when(i + 1 < n_steps)
    def _():
        codes_load(i + 1).start()

    pltpu.make_async_copy(xs_ref.at[pl.ds(0, n_assign)], xs_ref.at[pl.ds(0, n_assign)], sem).wait()


def _dispatch(off, tail, codes, x1, n_slots, *, rows):
    n, d = x1.shape
    kern = functools.partial(_dispatch_kernel, rows=rows)
    any_spec = pl.BlockSpec(memory_space=pl.ANY)
    return pl.pallas_call(
        kern,
        out_shape=jax.ShapeDtypeStruct((n_slots, d), x1.dtype),
        grid_spec=pltpu.PrefetchScalarGridSpec(
            num_scalar_prefetch=2,
            grid=(n // rows,),
            in_specs=[any_spec, pl.BlockSpec((rows, d), lambda i, off, tail: (i, 0))],
            out_specs=any_spec,
            scratch_shapes=[pltpu.SMEM((rows * TOP_K,), jnp.int32),
                            pltpu.VMEM((MOE_BLOCK, d), x1.dtype),
                            pltpu.SemaphoreType.DMA, pltpu.SemaphoreType.DMA,
                            pltpu.SemaphoreType.DMA]),
        compiler_params=_dma_params(("arbitrary",)),
        name="moe_dispatch",
    )(off, tail, codes, x1)


def _expert_kernel(be_ref, nu_ref, tok_ref, x_ref, wgu_ref, bgu_ref, wd_ref, bd_ref, y_ref,
                   wgu_b, wd_b, tok0, tok1, xbuf0, xbuf1, tsem, gsem0, gsem1):
    j = pl.program_id(0)
    nb = pl.num_programs(0)
    nu = nu_ref[0]
    dff = wd_ref.shape[2]
    toks, xbufs, gsems = (tok0, tok1), (xbuf0, xbuf1), (gsem0, gsem1)

    def tok_load(block, dst):
        first = pl.multiple_of(jnp.minimum(block, nb - 1) * MOE_BLOCK, MOE_BLOCK)
        return pltpu.make_async_copy(tok_ref.at[pl.ds(first, MOE_BLOCK)], dst, tsem)

    def row_copy(tok, s, xbuf, sem):
        return pltpu.make_async_copy(x_ref.at[pl.ds(tok, 1)], xbuf.at[pl.ds(s, 1)], sem)

    def rows_done(xbuf, sem):
        return pltpu.make_async_copy(x_ref.at[pl.ds(0, MOE_BLOCK)], xbuf, sem)

    @pl.when(j == 0)
    def _():
        first = tok_load(0, tok0)
        first.start()
        first.wait()
        lax.fori_loop(0, MOE_BLOCK,
                      lambda s, c: (row_copy(tok0[s], s, xbuf0, gsem0).start(), c)[1], 0)
        tok_load(1, tok1).start()

    for par in (0, 1):
        cur, nxt = par, 1 - par

        @pl.when((j < nu) & (j % 2 == par))
        def _(cur=cur, nxt=nxt):
            tok_load(j + 1, toks[nxt]).wait()
            rows_done(xbufs[cur], gsems[cur]).wait()

            @pl.when((j == 0) | (be_ref[j] != be_ref[jnp.maximum(j - 1, 0)]))
            def _():
                wgu_b[...] = wgu_ref[0, 0].astype(BF16)
                wd_b[...] = wd_ref[0, 0].astype(BF16)

            for s in range(MOE_BLOCK):
                row_copy(toks[nxt][s], s, xbufs[nxt], gsems[nxt]).start()
            xb = xbufs[cur][...].astype(BF16)
            gu = jnp.dot(xb, wgu_b[...], preferred_element_type=F32) + bgu_ref[0, 0]
            gate = jnp.minimum(gu[:, :dff], SWIGLU_LIMIT)
            up = jnp.clip(gu[:, dff:], -SWIGLU_LIMIT, SWIGLU_LIMIT)
            act = (up + 1.0) * (gate * jax.nn.sigmoid(SWIGLU_ALPHA * gate))
            y_ref[...] = jnp.dot(act.astype(BF16), wd_b[...], preferred_element_type=F32) + bd_ref[0, 0]
            tok_load(j + 2, toks[cur]).start()

        @pl.when((j == nu) & (j % 2 == par))
        def _(cur=cur, nxt=nxt):
            tok_load(j + 1, toks[nxt]).wait()
            rows_done(xbufs[cur], gsems[cur]).wait()

    @pl.when(j >= nu)
    def _():
        y_ref[...] = jnp.zeros(y_ref.shape, y_ref.dtype)


def _experts(block_e, n_used, slot_tok, x1, wgu, bgu, wd, bd, *, layer):
    n_slots = slot_tok.shape[0]
    d = x1.shape[1]
    dff = wd.shape[2]
    nb = n_slots // MOE_BLOCK
    blk = lambda j, be, nu: jnp.minimum(j, nu[0] - 1)
    wsel = lambda j, be, nu: (layer, be[blk(j, be, nu)], 0, 0)
    any_spec = pl.BlockSpec(memory_space=pl.ANY)
    return pl.pallas_call(
        _expert_kernel,
        out_shape=jax.ShapeDtypeStruct((n_slots, d), F32),
        grid_spec=pltpu.PrefetchScalarGridSpec(
            num_scalar_prefetch=2,
            grid=(nb,),
            in_specs=[any_spec, any_spec,
                      pl.BlockSpec((1, 1, d, 2 * dff), wsel),
                      pl.BlockSpec((1, 1, 1, 2 * dff), wsel),
                      pl.BlockSpec((1, 1, dff, d), wsel),
                      pl.BlockSpec((1, 1, 1, d), wsel)],
            out_specs=pl.BlockSpec((MOE_BLOCK, d), lambda j, be, nu: (j, 0)),
            scratch_shapes=[pltpu.VMEM((d, 2 * dff), BF16), pltpu.VMEM((dff, d), BF16),
                            pltpu.SMEM((MOE_BLOCK,), jnp.int32), pltpu.SMEM((MOE_BLOCK,), jnp.int32),
                            pltpu.VMEM((MOE_BLOCK, d), F32), pltpu.VMEM((MOE_BLOCK, d), F32),
                            pltpu.SemaphoreType.DMA, pltpu.SemaphoreType.DMA,
                            pltpu.SemaphoreType.DMA]),
        compiler_params=_dma_params(("arbitrary",)),
        name="moe_experts",
    )(block_e, n_used, slot_tok, x1, wgu, bgu, wd, bd)


def _combine_kernel(off_ref, codes_ref, gate_ref, r_ref, y_ref, g2_ref, b2_ref,
                    x2_ref, xb_ref, codes_smem, ybuf0, ybuf1, csem, sem0, sem1, *, rows):
    i = pl.program_id(0)
    n_steps = pl.num_programs(0)
    n_assign = rows * TOP_K

    def codes_load(step):
        return pltpu.make_async_copy(codes_ref.at[pl.ds(step * n_assign, n_assign)],
                                     codes_smem, csem)

    def gather(step, ybuf, sem):
        def issue(t, carry):
            for k in range(TOP_K):
                slot = _slot(codes_smem[t * TOP_K + k], off_ref)
                pltpu.make_async_copy(y_ref.at[pl.ds(slot, 1)],
                                      ybuf.at[pl.ds(k * rows + t, 1)], sem).start()
            return carry

        lax.fori_loop(0, rows, issue, 0, unroll=2)

        @pl.when(step + 1 < n_steps)
        def _():
            codes_load(step + 1).start()

    @pl.when(i == 0)
    def _():
        first = codes_load(0)
        first.start()
        first.wait()
        gather(0, ybuf0, sem0)

    def step(ybuf, sem, ybuf_next, sem_next):
        @pl.when(i + 1 < n_steps)
        def _():
            codes_load(i + 1).wait()
            gather(i + 1, ybuf_next, sem_next)

        pltpu.make_async_copy(y_ref.at[pl.ds(0, n_assign)], ybuf, sem).wait()
        gate = gate_ref[...]
        acc = r_ref[...]
        for k in range(TOP_K):
            acc = acc + gate[:, k:k + 1] * ybuf[k * rows:(k + 1) * rows, :]
        x2 = _layer_norm(acc, g2_ref[...], b2_ref[...])
        x2_ref[...] = x2
        xb_ref[...] = x2.astype(BF16)

    @pl.when(i % 2 == 0)
    def _():
        step(ybuf0, sem0, ybuf1, sem1)

    @pl.when(i % 2 == 1)
    def _():
        step(ybuf1, sem1, ybuf0, sem0)


def _combine(off, codes, gates, r, y, g2, b2, *, rows):
    n, d = r.shape
    kern = functools.partial(_combine_kernel, rows=rows)
    any_spec = pl.BlockSpec(memory_space=pl.ANY)
    return pl.pallas_call(
        kern,
        out_shape=(jax.ShapeDtypeStruct((n, d), F32), jax.ShapeDtypeStruct((n, d), BF16)),
        grid_spec=pltpu.PrefetchScalarGridSpec(
            num_scalar_prefetch=1,
            grid=(n // rows,),
            in_specs=[any_spec,
                      pl.BlockSpec((rows, LANES), lambda i, off: (i, 0)),
                      pl.BlockSpec((rows, d), lambda i, off: (i, 0)),
                      any_spec,
                      pl.BlockSpec((1, d), lambda i, off: (0, 0)),
                      pl.BlockSpec((1, d), lambda i, off: (0, 0))],
            out_specs=(pl.BlockSpec((rows, d), lambda i, off: (i, 0)),
                       pl.BlockSpec((rows, d), lambda i, off: (i, 0))),
            scratch_shapes=[pltpu.SMEM((rows * TOP_K,), jnp.int32),
                            pltpu.VMEM((rows * TOP_K, d), F32), pltpu.VMEM((rows * TOP_K, d), F32),
                            pltpu.SemaphoreType.DMA, pltpu.SemaphoreType.DMA,
                            pltpu.SemaphoreType.DMA]),
        compiler_params=_dma_params(("arbitrary",)),
        name="moe_combine_ln2",
    )(off, codes, gates, r, y, g2, b2)


def _alibi_slopes(n_heads):
    h = jnp.arange(1, n_heads + 1, dtype=F32)
    return jnp.exp2(-8.0 * h / n_heads)


def _block_size(n, target):
    t = min(n, target)
    while n % t:
        t //= 2
    return t


def kernel(x, p, w_in, b_in, lambda_q1, lambda_k1, lambda_q2, lambda_k2, subln_w, sinks,
           w_br_diff, w_br_swa, w_out, b_out, ln1_g, ln1_b, w_router, b_router,
           w_gate_up, b_gate_up, w_down, b_down, w_ple_gate, w_ple_proj, ln2_g, ln2_b):
    batch, seq, d = x.shape
    depth = w_in.shape[0]
    n = batch * seq
    dn_alpha = (2 * depth) ** 0.25
    n_assign = n * TOP_K
    n_blocks = n_assign // MOE_BLOCK + N_EXPERTS + 1
    n_slots = n_blocks * MOE_BLOCK

    n_in = w_in.shape[2]
    perm = jnp.concatenate([jnp.arange(n_in - 2 * d, n_in), jnp.arange(0, n_in - 2 * d)])
    log2e = math.log2(math.e)
    diff_slopes = _alibi_slopes(DIFF_HEADS) * log2e
    swa_slopes = _alibi_slopes(SWA_HEADS) * log2e

    tm_lin = _block_size(n, 1024)
    tq_diff = _block_size(seq, 1024)
    tq_swa = _block_size(seq, 512)
    tm_merge = _block_size(n, 512)
    tm_router = _block_size(n, 512)
    rows_moe = _block_size(n, 256)

    xf = x.reshape(n, d)
    xb = xf.astype(BF16)
    for i in range(depth):
        lam_init = 0.8 - 0.6 * math.exp(-0.3 * i)
        col_scale = jnp.ones((n_in,), F32).at[_COL_DQ:_COL_DK].set(HEAD_DIM ** -0.5 * log2e)
        col_scale = col_scale.at[_COL_SQ:_COL_SK].set(HEAD_DIM ** -0.5 * log2e)
        w_in_b = (jnp.take(w_in[i], perm, axis=1) * col_scale).astype(BF16)
        b_in_p = (jnp.take(b_in[i], perm) * col_scale)[None, :]
        proj = _linear(xb, w_in_b, b_in_p, tm=tm_lin, tn=1280)

        od = _diff_attention(proj, diff_slopes, _diff_query_aug(diff_slopes, tq_diff),
                             lambda_q1[i][None, :], lambda_k1[i][None, :],
                             lambda_q2[i][None, :], lambda_k2[i][None, :],
                             subln_w[i][None, :], batch=batch, seq=seq,
                             lam_init=lam_init, tq=tq_diff)
        osw = _swa_attention(proj, swa_slopes, sinks[i].astype(F32) * log2e,
                             batch=batch, seq=seq, tq=tq_swa)

        x1, r = _merge(xf, od, osw, proj, p[i].reshape(n, -1),
                       w_br_diff[i].astype(BF16), w_br_swa[i].astype(BF16),
                       w_out[i].astype(BF16), b_out[i][None, :],
                       ln1_g[i][None, :], ln1_b[i][None, :],
                       w_ple_gate[i].astype(BF16), w_ple_proj[i].astype(BF16),
                       dn_alpha=dn_alpha, tm=tm_merge)

        wr = jnp.zeros((d, LANES), F32).at[:, :N_EXPERTS].set(w_router[i])
        br = jnp.full((1, LANES), NEG_BIG, F32).at[0, :N_EXPERTS].set(b_router[i])
        code, gates, cnt = _router(x1, wr, br, tm=tm_router)

        counts = cnt[0, :N_EXPERTS].astype(jnp.int32)
        padded = (counts + MOE_BLOCK - 1) // MOE_BLOCK * MOE_BLOCK
        padded_end = jnp.cumsum(padded)
        off = (padded_end - padded).astype(jnp.int32)
        block_start = jnp.arange(n_blocks, dtype=jnp.int32) * MOE_BLOCK
        block_e = jnp.minimum(
            jnp.sum((block_start[:, None] >= padded_end[None, :]).astype(jnp.int32), axis=1),
            N_EXPERTS - 1)
        codes = code[:, :TOP_K].reshape(n_assign)
        n_used = (padded_end[-1:] // MOE_BLOCK).astype(jnp.int32)
        tail = jnp.concatenate([jnp.where(padded > 0, padded_end - MOE_BLOCK, -1).astype(jnp.int32),
                                n_used])

        dest = off[codes >> RANK_BITS] + (codes & ((1 << RANK_BITS) - 1))
        slot_tok = jnp.zeros((n_slots,), jnp.int32).at[dest].set(
            jnp.arange(n_assign, dtype=jnp.int32) // TOP_K)
        y = _experts(block_e, n_used, slot_tok, x1, w_gate_up, b_gate_up[:, :, None, :],
                     w_down, b_down[:, :, None, :], layer=i)
        xf, xb = _combine(off, codes, gates, r, y, ln2_g[i][None, :], ln2_b[i][None, :],
                          rows=rows_moe)
    return xf.reshape(batch, seq, d)
```

```python
import functools
import math

import jax
import jax.numpy as jnp
from jax import lax
from jax.experimental import pallas as pl
from jax.experimental.pallas import tpu as pltpu

F32 = jnp.float32
BF16 = jnp.bfloat16

HEAD_DIM = 64
DIFF_HEADS = 8
SWA_HEADS = 16
SWA_KV_HEADS = 2
SWA_GROUP = SWA_HEADS // SWA_KV_HEADS
WINDOW = 128
N_EXPERTS = 32
TOP_K = 4
MOE_BLOCK = 512
SWIGLU_LIMIT = 7.0
SWIGLU_ALPHA = 1.702
LN_EPS = 1e-5
RMS_EPS = 1e-5
NEG_BIG = -1e30

LANES = 128
VMEM_LIMIT = 56 * 1024 * 1024
RANK_BITS = 16

_COL_GA, _COL_GB, _COL_DQ, _COL_DK, _COL_DV, _COL_SQ, _COL_SK, _COL_SV = (
    0, 1024, 2048, 3072, 4096, 5120, 6144, 6272)


def _params(semantics):
    return pltpu.CompilerParams(dimension_semantics=semantics,
                                vmem_limit_bytes=VMEM_LIMIT)


def _dma_params(semantics):
    return pltpu.CompilerParams(dimension_semantics=semantics,
                                vmem_limit_bytes=VMEM_LIMIT,
                                disable_bounds_checks=True)


def _linear_kernel(x_ref, w_ref, b_ref, o_ref):
    acc = jnp.dot(x_ref[...].astype(BF16), w_ref[...], preferred_element_type=F32)
    o_ref[...] = (acc + b_ref[...]).astype(o_ref.dtype)


def _linear(x, w, b, *, tm, tn):
    n, k = x.shape
    nout = w.shape[1]
    return pl.pallas_call(
        _linear_kernel,
        out_shape=jax.ShapeDtypeStruct((n, nout), BF16),
        grid=(n // tm, nout // tn),
        in_specs=[pl.BlockSpec((tm, k), lambda i, j: (i, 0)),
                  pl.BlockSpec((k, tn), lambda i, j: (0, j)),
                  pl.BlockSpec((1, tn), lambda i, j: (0, j))],
        out_specs=pl.BlockSpec((tm, tn), lambda i, j: (i, j)),
        compiler_params=_params(("parallel", "arbitrary")),
        name="in_proj",
    )(x, w, b)


N_AUG = 6
STRIP = 256
DV = 2 * HEAD_DIM
ONES_ROWS = 16


def _key_aug(tk, first_lane):
    r = lax.broadcasted_iota(jnp.int32, (tk, LANES), 0)
    lane = lax.broadcasted_iota(jnp.int32, (tk, LANES), 1) - first_lane
    hi = ((r >> 7) << 7).astype(F32)
    lo = (r & 127).astype(F32)
    return jnp.where((lane >= 0) & (lane < 3), hi,
                     jnp.where((lane >= 3) & (lane < N_AUG), lo, 0.0))


def _diff_attn_kernel(slopes_ref, lq1_ref, lk1_ref, lq2_ref, lk2_ref, qaug_ref,
                      q_ref, k_ref, v_ref, w_ref, o_ref,
                      k1a_sc, k2a_sc, vt_sc, a1, a2, st, sa1, sa2, sb1, sb2, *, tq, tk, lam_init):
    h = pl.program_id(1)
    qi = pl.program_id(2)
    slope = slopes_ref[h]
    n_chunks = k1a_sc.shape[0]

    @pl.when(qi == 0)
    def _():
        lane = lax.broadcasted_iota(jnp.int32, (tk, LANES), 1)
        aug1 = _key_aug(tk, HEAD_DIM).astype(BF16)
        aug2 = _key_aug(tk, 0).astype(BF16)

        def build(c, carry):
            rows = pl.ds(pl.multiple_of(c * tk, tk), tk)
            k = k_ref[rows, :]
            k1a_sc[c] = jnp.where(lane < HEAD_DIM, k, aug1)
            k2a_sc[c] = jnp.where(lane >= HEAD_DIM, k, aug2)
            vt_sc[c, 0:DV, :] = v_ref[rows, :].astype(F32).T.astype(BF16)
            vt_sc[c, DV:DV + ONES_ROWS, :] = jnp.where(
                lax.broadcasted_iota(jnp.int32, (ONES_ROWS, tk), 0) == 0, 1.0, 0.0).astype(BF16)
            return carry

        lax.fori_loop(0, n_chunks, build, 0)

    qt = q_ref[...].astype(F32).T
    row = lax.broadcasted_iota(jnp.int32, qt.shape, 0)
    qt1 = jnp.where(row < HEAD_DIM, qt, qaug_ref[0, 0]).astype(BF16)
    qt2 = jnp.where(row >= HEAD_DIM, qt, qaug_ref[0, 1]).astype(BF16)
    a1[...] = jnp.zeros(a1.shape, F32)
    a2[...] = jnp.zeros(a2.shape, F32)

    M1, M2, XA1, XA2 = range(4)
    for r_ in (M1, M2):
        st[r_:r_ + 1, :] = jnp.full((1, tq), NEG_BIG, F32)

    maps = ((k1a_sc, qt1, a1, M1, XA1), (k2a_sc, qt2, a2, M2, XA2))
    tc = min(tq, STRIP)
    pieces = [(mp, slice(h * tc, (h + 1) * tc)) for h in range(tq // tc) for mp in range(2)]

    def scores(j, mp, cols, dst, key0=None):
        s = jnp.dot(maps[mp][0][j], maps[mp][1][:, cols], preferred_element_type=F32)
        if key0 is not None:
            krow = lax.broadcasted_iota(jnp.int32, s.shape, 0) + key0
            qcol = lax.broadcasted_iota(jnp.int32, s.shape, 1) + cols.start
            s = jnp.where(krow <= qcol, s, NEG_BIG)
        dst[mp][:, cols] = s
        return jnp.max(s, axis=0, keepdims=True)

    def accumulate(j, mp, cols, src, mx):
        _, _, a_sc, mr, _ = maps[mp]
        c = slope * jnp.full((1, tc), j * tk - qi * tq, jnp.int32).astype(F32)
        m = st[mr:mr + 1, cols]
        m_new = jnp.maximum(m, mx + c)
        alpha = jnp.exp2(m - m_new)
        p = jnp.exp2(src[mp][:, cols] - (m_new - c))
        st[mr:mr + 1, cols] = m_new
        a_sc[:, cols] = alpha * a_sc[:, cols] + jnp.dot(vt_sc[j], p.astype(BF16),
                                                        preferred_element_type=F32)

    bufs = ((sa1, sa2), (sb1, sb2))

    def park(maxima):
        for (mp, cols), x in zip(pieces, maxima):
            xr = maps[mp][4]
            st[xr:xr + 1, cols] = x

    def parked():
        return [st[maps[mp][4]:maps[mp][4] + 1, cols] for mp, cols in pieces]

    def step(s_blk, s_dst, p_blk, p_src, p_max, s_pieces=None, p_pieces=None):
        s_pieces = [(mp, cols, None) for mp, cols in pieces] if s_pieces is None else s_pieces
        p_pieces = pieces if p_pieces is None else p_pieces
        out = []
        for k in range(max(len(s_pieces), len(p_pieces))):
            if k < len(s_pieces):
                mp, cols, key0 = s_pieces[k]
                out.append(scores(s_blk, mp, cols, s_dst, key0))
            if k < len(p_pieces):
                mp, cols = p_pieces[k]
                accumulate(p_blk, mp, cols, p_src, p_max[k])
        return out

    def diag_pieces(d):
        out = []
        for mp, cols in pieces:
            if d * tk + tk - 1 <= cols.start:
                out.append((mp, cols, None))
            elif d * tk <= cols.stop - 1:
                out.append((mp, cols, d * tk))
        return out

    d0, d1 = diag_pieces(0), diag_pieces(1)
    d0_cols = [(mp, cols) for mp, cols, _ in d0]
    d1_cols = [(mp, cols) for mp, cols, _ in d1]
    first_diag = 2 * qi

    def finish(cur, nxt, x_prev=None, prev_blk=None):
        if x_prev is None:
            x0 = [scores(first_diag, mp, cols, nxt, key0) for mp, cols, key0 in d0]
        else:
            x0 = step(first_diag, nxt, prev_blk, cur, x_prev, s_pieces=d0)
        x1 = step(first_diag + 1, cur, first_diag, nxt, x0, s_pieces=d1, p_pieces=d0_cols)
        for (mp, cols), x in zip(d1_cols, x1):
            accumulate(first_diag + 1, mp, cols, cur, x)

    @pl.when(qi == 0)
    def _():
        finish(bufs[0], bufs[1])

    @pl.when(qi > 0)
    def _():
        park([scores(0, mp, cols, bufs[0]) for mp, cols in pieces])

        def pair(i, carry):
            j = 2 * i
            x = parked()
            x = step(j + 1, bufs[1], j, bufs[0], x)
            x = step(j + 2, bufs[0], j + 1, bufs[1], x)
            park(x)
            return carry

        lax.fori_loop(0, qi - 1, pair, 0)
        j = 2 * (qi - 1)
        x = step(j + 1, bufs[1], j, bufs[0], parked())
        finish(bufs[1], bufs[0], x_prev=x, prev_blk=j + 1)

    lam = (jnp.exp(jnp.sum(lq1_ref[...] * lk1_ref[...], axis=1, keepdims=True))
           - jnp.exp(jnp.sum(lq2_ref[...] * lk2_ref[...], axis=1, keepdims=True))
           + lam_init)
    o1 = a1[0:DV, :] / a1[DV:DV + 1, :]
    o2 = a2[0:DV, :] / a2[DV:DV + 1, :]
    o = (o1 - lam * o2).T
    y = o * lax.rsqrt(jnp.mean(jnp.square(o), axis=1, keepdims=True) + RMS_EPS)
    y = (y * w_ref[...]) * (1.0 - lam_init)
    o_ref[...] = y.astype(o_ref.dtype)


def _diff_attention(proj, slopes_l2, qaug, lq1, lk1, lq2, lk2, subln_w, *, batch, seq, lam_init, tq):
    n = proj.shape[0]
    nq = seq // tq
    tk = tq // 2
    assert tk % STRIP == 0 and tk % LANES == 0
    nk = seq // tk
    kern = functools.partial(_diff_attn_kernel, tq=tq, tk=tk, lam_init=lam_init)
    vec = pl.BlockSpec((1, HEAD_DIM), lambda b, h, i: (0, 0))
    cq, ck, cv = _COL_DQ // LANES, _COL_DK // LANES, _COL_DV // LANES
    return pl.pallas_call(
        kern,
        out_shape=jax.ShapeDtypeStruct((n, DIFF_HEADS * 2 * HEAD_DIM), BF16),
        grid=(batch, DIFF_HEADS, nq),
        in_specs=[pl.BlockSpec(memory_space=pltpu.SMEM),
                  vec, vec, vec, vec,
                  pl.BlockSpec((1, 2, LANES, tq), lambda b, h, i: (h, 0, 0, 0)),
                  pl.BlockSpec((tq, LANES), lambda b, h, i: (b * nq + i, cq + h)),
                  pl.BlockSpec((seq, LANES), lambda b, h, i: (b, ck + h)),
                  pl.BlockSpec((seq, LANES), lambda b, h, i: (b, cv + h)),
                  pl.BlockSpec((1, 2 * HEAD_DIM), lambda b, h, i: (0, 0))],
        out_specs=pl.BlockSpec((tq, LANES), lambda b, h, i: (b * nq + i, h)),
        scratch_shapes=[pltpu.VMEM((nk, tk, LANES), BF16), pltpu.VMEM((nk, tk, LANES), BF16),
                        pltpu.VMEM((nk, DV + ONES_ROWS, tk), BF16),
                        pltpu.VMEM((DV + ONES_ROWS, tq), F32), pltpu.VMEM((DV + ONES_ROWS, tq), F32),
                        pltpu.VMEM((8, tq), F32)] + [pltpu.VMEM((tk, tq), F32)] * 4,
        compiler_params=_params(("arbitrary", "arbitrary", "arbitrary")),
        name="diff_attn",
    )(slopes_l2, lq1, lk1, lq2, lk2, qaug, proj, proj, proj, subln_w)


def _diff_query_aug(slopes_l2, tq):
    s0 = slopes_l2.astype(BF16).astype(F32)
    s1 = (slopes_l2 - s0).astype(BF16).astype(F32)
    s2 = (slopes_l2 - s0 - s1).astype(BF16).astype(F32)
    parts = jnp.stack([s0, s1, s2, s0, s1, s2], axis=1)
    n_heads = slopes_l2.shape[0]
    cols = jnp.zeros((n_heads, 2, LANES), F32)
    cols = cols.at[:, 0, HEAD_DIM:HEAD_DIM + N_AUG].set(parts)
    cols = cols.at[:, 1, 0:N_AUG].set(parts)
    return jnp.broadcast_to(cols[:, :, :, None], (n_heads, 2, LANES, tq))


def _swa_kernel(slopes_ref, sinks_ref, q_ref, kp_ref, kc_ref, vp_ref, vc_ref, o_ref,
                bias_sc, sink_sc, ot_sc, *, tq):
    qi = pl.program_id(1)
    wide = SWA_GROUP * WINDOW

    @pl.when((pl.program_id(0) == 0) & (qi == 0))
    def _():
        key = lax.broadcasted_iota(jnp.int32, (2 * WINDOW, wide), 0)
        col = lax.broadcasted_iota(jnp.int32, (2 * WINDOW, wide), 1)
        dist = (col & (WINDOW - 1)) + WINDOW - key
        valid = (dist >= 0) & (dist < WINDOW)
        head = lax.broadcasted_iota(jnp.int32, (1, wide), 1) >> (WINDOW.bit_length() - 1)
        for hk in range(SWA_KV_HEADS):
            slope = jnp.zeros((1, wide), F32)
            sink = jnp.zeros((1, wide), F32)
            for g in range(SWA_GROUP):
                slope = jnp.where(head == g, slopes_ref[hk * SWA_GROUP + g], slope)
                sink = jnp.where(head == g, sinks_ref[hk * SWA_GROUP + g], sink)
            bias_sc[hk] = jnp.where(valid, -slope * dist.astype(F32), NEG_BIG)
            sink_sc[hk] = sink

    kcat = jnp.concatenate([kp_ref[...], kc_ref[...]], axis=0)
    vcat = jnp.concatenate([vp_ref[...], vc_ref[...]], axis=0)
    zeros = jnp.zeros((HEAD_DIM, WINDOW), F32)
    for j in range(tq // WINDOW):
        kj = kcat[j * WINDOW:(j + 2) * WINDOW]
        vt = vcat[j * WINDOW:(j + 2) * WINDOW].astype(F32).T
        qt = q_ref[j * WINDOW:(j + 1) * WINDOW, :].astype(F32).T
        for hk in range(SWA_KV_HEADS):
            cols = []
            for g in range(SWA_GROUP):
                hq = hk * SWA_GROUP + g
                qh = qt[hq * HEAD_DIM:(hq + 1) * HEAD_DIM]
                cols.append(jnp.concatenate([qh, zeros] if hk == 0 else [zeros, qh], axis=0))
            q8 = jnp.concatenate(cols, axis=1).astype(BF16)
            s = jnp.dot(kj, q8, preferred_element_type=F32) + bias_sc[hk]
            if j == 0:
                key = lax.broadcasted_iota(jnp.int32, s.shape, 0)
                s = jnp.where((key < WINDOW) & (qi == 0), NEG_BIG, s)
            sink = sink_sc[hk]
            m = jnp.maximum(jnp.max(s, axis=0, keepdims=True), sink)
            e = jnp.exp2(s - m)
            denom = jnp.sum(e, axis=0, keepdims=True) + jnp.exp2(sink - m)
            v_hk = vt[hk * HEAD_DIM:(hk + 1) * HEAD_DIM].astype(BF16)
            ot = jnp.dot(v_hk, e.astype(BF16), preferred_element_type=F32) / denom
            for g in range(SWA_GROUP):
                hq = hk * SWA_GROUP + g
                ot_sc[hq * HEAD_DIM:(hq + 1) * HEAD_DIM, :] = ot[:, g * WINDOW:(g + 1) * WINDOW]
        o_ref[j * WINDOW:(j + 1) * WINDOW, :] = ot_sc[...].T.astype(o_ref.dtype)


def _swa_attention(proj, slopes, sinks, *, batch, seq, tq):
    n = proj.shape[0]
    nq = seq // tq
    sub = tq // WINDOW
    nwin = seq // WINDOW
    kern = functools.partial(_swa_kernel, tq=tq)
    cq = _COL_SQ // (SWA_HEADS * HEAD_DIM)
    ck, cv = _COL_SK // LANES, _COL_SV // LANES
    prev = lambda c: (lambda b, i: (b * nwin + jnp.maximum(i * sub - 1, 0), c))
    cur = lambda c: (lambda b, i: (b * nq + i, c))
    smem = pl.BlockSpec(memory_space=pltpu.SMEM)
    return pl.pallas_call(
        kern,
        out_shape=jax.ShapeDtypeStruct((n, SWA_HEADS * HEAD_DIM), BF16),
        grid=(batch, nq),
        in_specs=[smem, smem,
                  pl.BlockSpec((tq, SWA_HEADS * HEAD_DIM), cur(cq)),
                  pl.BlockSpec((WINDOW, LANES), prev(ck)),
                  pl.BlockSpec((tq, LANES), cur(ck)),
                  pl.BlockSpec((WINDOW, LANES), prev(cv)),
                  pl.BlockSpec((tq, LANES), cur(cv))],
        out_specs=pl.BlockSpec((tq, SWA_HEADS * HEAD_DIM), lambda b, i: (b * nq + i, 0)),
        scratch_shapes=[pltpu.VMEM((SWA_KV_HEADS, 2 * WINDOW, SWA_GROUP * WINDOW), F32),
                        pltpu.VMEM((SWA_KV_HEADS, 1, SWA_GROUP * WINDOW), F32),
                        pltpu.VMEM((SWA_HEADS * HEAD_DIM, WINDOW), F32)],
        compiler_params=_params(("arbitrary", "arbitrary")),
        name="swa_attn",
    )(slopes, sinks, proj, proj, proj, proj, proj)


def _layer_norm(y, g, b):
    mu = jnp.mean(y, axis=1, keepdims=True)
    var = jnp.mean(jnp.square(y - mu), axis=1, keepdims=True)
    return (y - mu) * lax.rsqrt(var + LN_EPS) * g + b


def _merge_kernel(x_ref, od_ref, os_ref, ga_ref, gb_ref, p_ref,
                  wa_ref, wb_ref, wo_ref, bo_ref, g1_ref, b1_ref, wpg_ref, wpp_ref,
                  x1_ref, r_ref, *, dn_alpha):
    a = jnp.dot(od_ref[...], wa_ref[...], preferred_element_type=F32)
    b = jnp.dot(os_ref[...], wb_ref[...], preferred_element_type=F32)
    merged = (jax.nn.sigmoid(ga_ref[...].astype(F32)) * a
              + jax.nn.sigmoid(gb_ref[...].astype(F32)) * b)
    mix = jnp.dot(merged.astype(BF16), wo_ref[...], preferred_element_type=F32) + bo_ref[...]
    x1 = _layer_norm(dn_alpha * x_ref[...] + mix, g1_ref[...], b1_ref[...])
    x1_ref[...] = x1
    gate = jax.nn.sigmoid(jnp.dot(x1.astype(BF16), wpg_ref[...], preferred_element_type=F32))
    ple = gate * jnp.dot(p_ref[...].astype(BF16), wpp_ref[...], preferred_element_type=F32)
    r_ref[...] = dn_alpha * x1 + ple


def _merge(x, od, osw, proj, p, wa, wb, wo, bo, g1, b1, wpg, wpp, *, dn_alpha, tm):
    n, d = x.shape
    pd = p.shape[1]
    row = lambda c: (lambda i: (i, c))
    full = lambda shape: pl.BlockSpec(shape, lambda i: (0, 0))
    kern = functools.partial(_merge_kernel, dn_alpha=dn_alpha)
    return pl.pallas_call(
        kern,
        out_shape=(jax.ShapeDtypeStruct((n, d), F32), jax.ShapeDtypeStruct((n, d), F32)),
        grid=(n // tm,),
        in_specs=[pl.BlockSpec((tm, d), row(0)),
                  pl.BlockSpec((tm, d), row(0)),
                  pl.BlockSpec((tm, d), row(0)),
                  pl.BlockSpec((tm, d), row(_COL_GA // d)),
                  pl.BlockSpec((tm, d), row(_COL_GB // d)),
                  pl.BlockSpec((tm, pd), row(0)),
                  full((d, d)), full((d, d)), full((d, d)), full((1, d)),
                  full((1, d)), full((1, d)), full((d, d)), full((pd, d))],
        out_specs=(pl.BlockSpec((tm, d), row(0)), pl.BlockSpec((tm, d), row(0))),
        compiler_params=_params(("parallel",)),
        name="merge_ln1",
    )(x, od, osw, proj, proj, p, wa, wb, wo, bo, g1, b1, wpg, wpp)


def _router_kernel(x_ref, w_ref, b_ref, code_ref, gate_ref, cnt_ref, carry, *, tm):
    @pl.when(pl.program_id(0) == 0)
    def _():
        carry[...] = jnp.zeros(carry.shape, F32)

    logits = jnp.dot(x_ref[...], w_ref[...], preferred_element_type=F32,
                     precision=lax.Precision.HIGHEST) + b_ref[...]
    lane = lax.broadcasted_iota(jnp.int32, logits.shape, 1)
    lanef = lane.astype(F32)
    work = logits
    tops, idxs = [], []
    onehot = jnp.zeros(logits.shape, F32)
    for _ in range(TOP_K):
        m = jnp.max(work, axis=1, keepdims=True)
        idx = jnp.min(jnp.where(work == m, lanef, float(LANES)), axis=1, keepdims=True)
        sel = lanef == idx
        onehot = jnp.where(sel, 1.0, onehot)
        work = jnp.where(sel, -jnp.inf, work)
        tops.append(m)
        idxs.append(idx)
    es = [jnp.exp(t - tops[0]) for t in tops]
    denom = es[0] + es[1] + es[2] + es[3]
    r = lax.broadcasted_iota(jnp.int32, (tm, tm), 0)
    c = lax.broadcasted_iota(jnp.int32, (tm, tm), 1)
    tri = jnp.where(c < r, 1.0, 0.0).astype(BF16)
    before = jnp.dot(tri, onehot.astype(BF16), preferred_element_type=F32) + carry[0:1, :]
    code = jnp.zeros(logits.shape, jnp.int32)
    gate = jnp.zeros(logits.shape, F32)
    for k in range(TOP_K):
        rank = jnp.sum(jnp.where(lanef == idxs[k], before, 0.0), axis=1, keepdims=True)
        ck = (idxs[k] * float(1 << RANK_BITS) + rank).astype(jnp.int32)
        code = jnp.where(lane == k, ck, code)
        gate = jnp.where(lane == k, es[k] / denom, gate)
    code_ref[...] = code
    gate_ref[...] = gate
    carry[0:1, :] = carry[0:1, :] + jnp.sum(onehot, axis=0, keepdims=True)
    cnt_ref[...] = carry[...]


def _router(x1, w, b, *, tm):
    n, d = x1.shape
    kern = functools.partial(_router_kernel, tm=tm)
    return pl.pallas_call(
        kern,
        out_shape=(jax.ShapeDtypeStruct((n, LANES), jnp.int32),
                   jax.ShapeDtypeStruct((n, LANES), F32),
                   jax.ShapeDtypeStruct((8, LANES), F32)),
        grid=(n // tm,),
        in_specs=[pl.BlockSpec((tm, d), lambda i: (i, 0)),
                  pl.BlockSpec((d, LANES), lambda i: (0, 0)),
                  pl.BlockSpec((1, LANES), lambda i: (0, 0))],
        out_specs=(pl.BlockSpec((tm, LANES), lambda i: (i, 0)),
                   pl.BlockSpec((tm, LANES), lambda i: (i, 0)),
                   pl.BlockSpec((8, LANES), lambda i: (0, 0))),
        scratch_shapes=[pltpu.VMEM((8, LANES), F32)],
        compiler_params=_params(("arbitrary",)),
        name="router",
    )(x1, w, b)


def _slot(code, off_ref):
    return off_ref[code >> RANK_BITS] + (code & ((1 << RANK_BITS) - 1))


def _dispatch_kernel(off_ref, tail_ref, codes_ref, x_ref, xs_ref, codes_smem, zeros, csem, sem, zsem,
                     *, rows):
    i = pl.program_id(0)
    n_steps = pl.num_programs(0)
    n_assign = rows * TOP_K

    def codes_load(step):
        return pltpu.make_async_copy(codes_ref.at[pl.ds(step * n_assign, n_assign)],
                                     codes_smem, csem)

    @pl.when(i == 0)
    def _():
        codes_load(0).start()
        zeros[...] = jnp.zeros(zeros.shape, zeros.dtype)

        def fill(e):
            first = pl.multiple_of(tail_ref[e], MOE_BLOCK)
            return pltpu.make_async_copy(zeros, xs_ref.at[pl.ds(first, MOE_BLOCK)], zsem)

        def start(e, carry):
            @pl.when(tail_ref[e] >= 0)
            def _():
                fill(e).start()
            return carry

        def finish(e, carry):
            @pl.when(tail_ref[e] >= 0)
            def _():
                fill(e).wait()
            return carry

        def unused(b):
            first = pl.multiple_of(b * MOE_BLOCK, MOE_BLOCK)
            return pltpu.make_async_copy(zeros, xs_ref.at[pl.ds(first, MOE_BLOCK)], zsem)

        n_used = tail_ref[N_EXPERTS]
        n_blocks = xs_ref.shape[0] // MOE_BLOCK
        lax.fori_loop(0, N_EXPERTS, start, 0)
        lax.fori_loop(n_used, n_blocks, lambda b, c: (unused(b).start(), c)[1], 0)
        lax.fori_loop(0, N_EXPERTS, finish, 0)
        lax.fori_loop(n_used, n_blocks, lambda b, c: (unused(b).wait(), c)[1], 0)

    codes_load(i).wait()

    def issue(t, carry):
        src = x_ref.at[pl.ds(t, 1)]
        for k in range(TOP_K):
            slot = _slot(codes_smem[t * TOP_K + k], off_ref)
            pltpu.make_async_copy(src, xs_ref.at[pl.ds(slot, 1)], sem).start(priority=k % 2)
        return carry

    lax.fori_loop(0, rows, issue, 0, unroll=2)

    @pl.when(i + 1 < n_steps)
    def _():
        codes_load(i + 1).start()

    pltpu.make_async_copy(xs_ref.at[pl.ds(0, n_assign)], xs_ref.at[pl.ds(0, n_assign)], sem).wait()


def _dispatch(off, tail, codes, x1, n_slots, *, rows):
    n, d = x1.shape
    kern = functools.partial(_dispatch_kernel, rows=rows)
    any_spec = pl.BlockSpec(memory_space=pl.ANY)
    return pl.pallas_call(
        kern,
        out_shape=jax.ShapeDtypeStruct((n_slots, d), x1.dtype),
        grid_spec=pltpu.PrefetchScalarGridSpec(
            num_scalar_prefetch=2,
            grid=(n // rows,),
            in_specs=[any_spec, pl.BlockSpec((rows, d), lambda i, off, tail: (i, 0))],
            out_specs=any_spec,
            scratch_shapes=[pltpu.SMEM((rows * TOP_K,), jnp.int32),
                            pltpu.VMEM((MOE_BLOCK, d), x1.dtype),
                            pltpu.SemaphoreType.DMA, pltpu.SemaphoreType.DMA,
                            pltpu.SemaphoreType.DMA]),
        compiler_params=_dma_params(("arbitrary",)),
        name="moe_dispatch",
    )(off, tail, codes, x1)


def _expert_kernel(be_ref, nu_ref, xs_ref, wgu_ref, bgu_ref, wd_ref, bd_ref, y_ref, wgu_b, wd_b):
    j = pl.program_id(0)
    dff = wd_ref.shape[2]

    @pl.when(j < nu_ref[0])
    def _():
        @pl.when((j == 0) | (be_ref[j] != be_ref[jnp.maximum(j - 1, 0)]))
        def _():
            wgu_b[...] = wgu_ref[0, 0].astype(BF16)
            wd_b[...] = wd_ref[0, 0].astype(BF16)

        gu = jnp.dot(xs_ref[...].astype(BF16), wgu_b[...], preferred_element_type=F32) + bgu_ref[0, 0]
        gate = jnp.minimum(gu[:, :dff], SWIGLU_LIMIT)
        up = jnp.clip(gu[:, dff:], -SWIGLU_LIMIT, SWIGLU_LIMIT)
        act = (up + 1.0) * (gate * jax.nn.sigmoid(SWIGLU_ALPHA * gate))
        y_ref[...] = jnp.dot(act.astype(BF16), wd_b[...], preferred_element_type=F32) + bd_ref[0, 0]

    @pl.when(j >= nu_ref[0])
    def _():
        y_ref[...] = jnp.zeros(y_ref.shape, y_ref.dtype)


def _experts(block_e, n_used, xs, wgu, bgu, wd, bd, *, layer):
    n_slots, d = xs.shape
    dff = wd.shape[2]
    nb = n_slots // MOE_BLOCK
    blk = lambda j, be, nu: jnp.minimum(j, nu[0] - 1)
    row = lambda j, be, nu: (blk(j, be, nu), 0)
    wsel = lambda j, be, nu: (layer, be[blk(j, be, nu)], 0, 0)
    return pl.pallas_call(
        _expert_kernel,
        out_shape=jax.ShapeDtypeStruct((n_slots, d), F32),
        grid_spec=pltpu.PrefetchScalarGridSpec(
            num_scalar_prefetch=2,
            grid=(nb,),
            in_specs=[pl.BlockSpec((MOE_BLOCK, d), row),
                      pl.BlockSpec((1, 1, d, 2 * dff), wsel),
                      pl.BlockSpec((1, 1, 1, 2 * dff), wsel),
                      pl.BlockSpec((1, 1, dff, d), wsel),
                      pl.BlockSpec((1, 1, 1, d), wsel)],
            out_specs=pl.BlockSpec((MOE_BLOCK, d), lambda j, be, nu: (j, 0)),
            scratch_shapes=[pltpu.VMEM((d, 2 * dff), BF16), pltpu.VMEM((dff, d), BF16)]),
        compiler_params=_params(("arbitrary",)),
        name="moe_experts",
    )(block_e, n_used, xs, wgu, bgu, wd, bd)


def _combine_kernel(off_ref, codes_ref, gate_ref, r_ref, y_ref, g2_ref, b2_ref,
                    x2_ref, xb_ref, codes_smem, ybuf0, ybuf1, csem, sem0, sem1, *, rows):
    i = pl.program_id(0)
    n_steps = pl.num_programs(0)
    n_assign = rows * TOP_K

    def codes_load(step):
        return pltpu.make_async_copy(codes_ref.at[pl.ds(step * n_assign, n_assign)],
                                     codes_smem, csem)

    def gather(step, ybuf, sem):
        def issue(t, carry):
            for k in range(TOP_K):
                slot = _slot(codes_smem[t * TOP_K + k], off_ref)
                pltpu.make_async_copy(y_ref.at[pl.ds(slot, 1)],
                                      ybuf.at[pl.ds(k * rows + t, 1)], sem).start(priority=k % 2)
            return carry

        lax.fori_loop(0, rows, issue, 0, unroll=2)

        @pl.when(step + 1 < n_steps)
        def _():
            codes_load(step + 1).start()

    @pl.when(i == 0)
    def _():
        first = codes_load(0)
        first.start()
        first.wait()
        gather(0, ybuf0, sem0)

    def step(ybuf, sem, ybuf_next, sem_next):
        @pl.when(i + 1 < n_steps)
        def _():
            codes_load(i + 1).wait()
            gather(i + 1, ybuf_next, sem_next)

        pltpu.make_async_copy(y_ref.at[pl.ds(0, n_assign)], ybuf, sem).wait()
        gate = gate_ref[...]
        acc = r_ref[...]
        for k in range(TOP_K):
            acc = acc + gate[:, k:k + 1] * ybuf[k * rows:(k + 1) * rows, :]
        x2 = _layer_norm(acc, g2_ref[...], b2_ref[...])
        x2_ref[...] = x2
        xb_ref[...] = x2.astype(BF16)

    @pl.when(i % 2 == 0)
    def _():
        step(ybuf0, sem0, ybuf1, sem1)

    @pl.when(i % 2 == 1)
    def _():
        step(ybuf1, sem1, ybuf0, sem0)


def _combine(off, codes, gates, r, y, g2, b2, *, rows):
    n, d = r.shape
    kern = functools.partial(_combine_kernel, rows=rows)
    any_spec = pl.BlockSpec(memory_space=pl.ANY)
    return pl.pallas_call(
        kern,
        out_shape=(jax.ShapeDtypeStruct((n, d), F32), jax.ShapeDtypeStruct((n, d), BF16)),
        grid_spec=pltpu.PrefetchScalarGridSpec(
            num_scalar_prefetch=1,
            grid=(n // rows,),
            in_specs=[any_spec,
                      pl.BlockSpec((rows, LANES), lambda i, off: (i, 0)),
                      pl.BlockSpec((rows, d), lambda i, off: (i, 0)),
                      any_spec,
                      pl.BlockSpec((1, d), lambda i, off: (0, 0)),
                      pl.BlockSpec((1, d), lambda i, off: (0, 0))],
            out_specs=(pl.BlockSpec((rows, d), lambda i, off: (i, 0)),
                       pl.BlockSpec((rows, d), lambda i, off: (i, 0))),
            scratch_shapes=[pltpu.SMEM((rows * TOP_K,), jnp.int32),
                            pltpu.VMEM((rows * TOP_K, d), F32), pltpu.VMEM((rows * TOP_K, d), F32),
                            pltpu.SemaphoreType.DMA, pltpu.SemaphoreType.DMA,
                            pltpu.SemaphoreType.DMA]),
        compiler_params=_dma_params(("arbitrary",)),
        name="moe_combine_ln2",
    )(off, codes, gates, r, y, g2, b2)


def _alibi_slopes(n_heads):
    h = jnp.arange(1, n_heads + 1, dtype=F32)
    return jnp.exp2(-8.0 * h / n_heads)


def _block_size(n, target):
    t = min(n, target)
    while n % t:
        t //= 2
    return t


def kernel(x, p, w_in, b_in, lambda_q1, lambda_k1, lambda_q2, lambda_k2, subln_w, sinks,
           w_br_diff, w_br_swa, w_out, b_out, ln1_g, ln1_b, w_router, b_router,
           w_gate_up, b_gate_up, w_down, b_down, w_ple_gate, w_ple_proj, ln2_g, ln2_b):
    batch, seq, d = x.shape
    depth = w_in.shape[0]
    n = batch * seq
    dn_alpha = (2 * depth) ** 0.25
    n_assign = n * TOP_K
    n_blocks = n_assign // MOE_BLOCK + N_EXPERTS + 1
    n_slots = n_blocks * MOE_BLOCK

    n_in = w_in.shape[2]
    permute = lambda a: jnp.concatenate([a[..., n_in - 2 * d:], a[..., :n_in - 2 * d]], axis=-1)
    log2e = math.log2(math.e)
    diff_slopes = _alibi_slopes(DIFF_HEADS) * log2e
    swa_slopes = _alibi_slopes(SWA_HEADS) * log2e

    tm_lin = _block_size(n, 1024)
    tq_diff = _block_size(seq, 1024)
    tq_swa = _block_size(seq, 512)
    tm_merge = _block_size(n, 512)
    tm_router = _block_size(n, 512)
    rows_moe = _block_size(n, 256)

    xf = x.reshape(n, d)
    xb = xf
    for i in range(depth):
        lam_init = 0.8 - 0.6 * math.exp(-0.3 * i)
        col_scale = jnp.ones((n_in,), F32).at[_COL_DQ:_COL_DK].set(HEAD_DIM ** -0.5 * log2e)
        col_scale = col_scale.at[_COL_SQ:_COL_SK].set(HEAD_DIM ** -0.5 * log2e)
        w_in_b = (permute(w_in[i]) * col_scale).astype(BF16)
        b_in_p = (permute(b_in[i]) * col_scale)[None, :]
        proj = _linear(xb, w_in_b, b_in_p, tm=tm_lin, tn=1280)

        od = _diff_attention(proj, diff_slopes, _diff_query_aug(diff_slopes, tq_diff),
                             lambda_q1[i][None, :], lambda_k1[i][None, :],
                             lambda_q2[i][None, :], lambda_k2[i][None, :],
                             subln_w[i][None, :], batch=batch, seq=seq,
                             lam_init=lam_init, tq=tq_diff)
        osw = _swa_attention(proj, swa_slopes, sinks[i].astype(F32) * log2e,
                             batch=batch, seq=seq, tq=tq_swa)

        x1, r = _merge(xf, od, osw, proj, p[i].reshape(n, -1),
                       w_br_diff[i].astype(BF16), w_br_swa[i].astype(BF16),
                       w_out[i].astype(BF16), b_out[i][None, :],
                       ln1_g[i][None, :], ln1_b[i][None, :],
                       w_ple_gate[i].astype(BF16), w_ple_proj[i].astype(BF16),
                       dn_alpha=dn_alpha, tm=tm_merge)

        wr = jnp.zeros((d, LANES), F32).at[:, :N_EXPERTS].set(w_router[i])
        br = jnp.full((1, LANES), NEG_BIG, F32).at[0, :N_EXPERTS].set(b_router[i])
        code, gates, cnt = _router(x1, wr, br, tm=tm_router)

        counts = cnt[0, :N_EXPERTS].astype(jnp.int32)
        padded = (counts + MOE_BLOCK - 1) // MOE_BLOCK * MOE_BLOCK
        padded_end = jnp.cumsum(padded)
        off = (padded_end - padded).astype(jnp.int32)
        block_start = jnp.arange(n_blocks, dtype=jnp.int32) * MOE_BLOCK
        block_e = jnp.minimum(
            jnp.sum((block_start[:, None] >= padded_end[None, :]).astype(jnp.int32), axis=1),
            N_EXPERTS - 1)
        codes = code[:, :TOP_K].reshape(n_assign)
        n_used = (padded_end[-1:] // MOE_BLOCK).astype(jnp.int32)
        tail = jnp.concatenate([jnp.where(padded > 0, padded_end - MOE_BLOCK, -1).astype(jnp.int32),
                                n_used])

        xs = _dispatch(off, tail, codes, x1, n_slots, rows=rows_moe)
        y = _experts(block_e, n_used, xs, w_gate_up, b_gate_up[:, :, None, :],
                     w_down, b_down[:, :, None, :], layer=i)
        xf, xb = _combine(off, codes, gates, r, y, ln2_g[i][None, :], ln2_b[i][None, :],
                          rows=rows_moe)
    return xf.reshape(batch, seq, d)
```

```python
import functools
import math

import jax
import jax.numpy as jnp
from jax import lax
from jax.experimental import pallas as pl
from jax.experimental.pallas import tpu as pltpu

F32 = jnp.float32
BF16 = jnp.bfloat16

HEAD_DIM = 64
DIFF_HEADS = 8
SWA_HEADS = 16
SWA_KV_HEADS = 2
SWA_GROUP = SWA_HEADS // SWA_KV_HEADS
WINDOW = 128
N_EXPERTS = 32
TOP_K = 4
MOE_BLOCK = 512
SWIGLU_LIMIT = 7.0
SWIGLU_ALPHA = 1.702
LN_EPS = 1e-5
RMS_EPS = 1e-5
NEG_BIG = -1e30

LANES = 128
VMEM_LIMIT = 56 * 1024 * 1024
RANK_BITS = 16

_COL_GA, _COL_GB, _COL_DQ, _COL_DK, _COL_DV, _COL_SQ, _COL_SK, _COL_SV = (
    0, 1024, 2048, 3072, 4096, 5120, 6144, 6272)


def _params(semantics):
    return pltpu.CompilerParams(dimension_semantics=semantics,
                                vmem_limit_bytes=VMEM_LIMIT)


def _dma_params(semantics):
    return pltpu.CompilerParams(dimension_semantics=semantics,
                                vmem_limit_bytes=VMEM_LIMIT,
                                disable_bounds_checks=True)


def _linear_kernel(x_ref, w_ref, b_ref, o_ref):
    acc = jnp.dot(x_ref[...].astype(BF16), w_ref[...], preferred_element_type=F32)
    o_ref[...] = (acc + b_ref[...]).astype(o_ref.dtype)


def _linear(x, w, b, *, tm, tn):
    n, k = x.shape
    nout = w.shape[1]
    return pl.pallas_call(
        _linear_kernel,
        out_shape=jax.ShapeDtypeStruct((n, nout), BF16),
        grid=(n // tm, nout // tn),
        in_specs=[pl.BlockSpec((tm, k), lambda i, j: (i, 0)),
                  pl.BlockSpec((k, tn), lambda i, j: (0, j)),
                  pl.BlockSpec((1, tn), lambda i, j: (0, j))],
        out_specs=pl.BlockSpec((tm, tn), lambda i, j: (i, j)),
        compiler_params=_params(("parallel", "arbitrary")),
        name="in_proj",
    )(x, w, b)


N_AUG = 6
STRIP = 256
DV = 2 * HEAD_DIM
ONES_ROWS = 16


def _key_aug(tk, first_lane):
    r = lax.broadcasted_iota(jnp.int32, (tk, LANES), 0)
    lane = lax.broadcasted_iota(jnp.int32, (tk, LANES), 1) - first_lane
    hi = ((r >> 7) << 7).astype(F32)
    lo = (r & 127).astype(F32)
    return jnp.where((lane >= 0) & (lane < 3), hi,
                     jnp.where((lane >= 3) & (lane < N_AUG), lo, 0.0))


def _diff_attn_kernel(slopes_ref, lq1_ref, lk1_ref, lq2_ref, lk2_ref, qaug_ref,
                      q_ref, k_ref, v_ref, w_ref, o_ref,
                      k1a_sc, k2a_sc, vt_sc, a1, a2, st, sa1, sa2, sb1, sb2, *, tq, tk, lam_init):
    h = pl.program_id(1)
    qi = pl.program_id(2)
    slope = slopes_ref[h]
    n_chunks = k1a_sc.shape[0]

    @pl.when(qi == 0)
    def _():
        lane = lax.broadcasted_iota(jnp.int32, (tk, LANES), 1)
        aug1 = _key_aug(tk, HEAD_DIM).astype(BF16)
        aug2 = _key_aug(tk, 0).astype(BF16)

        def build(c, carry):
            rows = pl.ds(pl.multiple_of(c * tk, tk), tk)
            k = k_ref[rows, :]
            k1a_sc[c] = jnp.where(lane < HEAD_DIM, k, aug1)
            k2a_sc[c] = jnp.where(lane >= HEAD_DIM, k, aug2)
            vt_sc[c, 0:DV, :] = v_ref[rows, :].astype(F32).T.astype(BF16)
            vt_sc[c, DV:DV + ONES_ROWS, :] = jnp.where(
                lax.broadcasted_iota(jnp.int32, (ONES_ROWS, tk), 0) == 0, 1.0, 0.0).astype(BF16)
            return carry

        lax.fori_loop(0, n_chunks, build, 0)

    qt = q_ref[...].astype(F32).T
    row = lax.broadcasted_iota(jnp.int32, qt.shape, 0)
    qt1 = jnp.where(row < HEAD_DIM, qt, qaug_ref[0, 0]).astype(BF16)
    qt2 = jnp.where(row >= HEAD_DIM, qt, qaug_ref[0, 1]).astype(BF16)
    a1[...] = jnp.zeros(a1.shape, F32)
    a2[...] = jnp.zeros(a2.shape, F32)

    M1, M2, XA1, XA2 = range(4)
    for r_ in (M1, M2):
        st[r_:r_ + 1, :] = jnp.full((1, tq), NEG_BIG, F32)

    maps = ((k1a_sc, qt1, a1, M1, XA1), (k2a_sc, qt2, a2, M2, XA2))
    tc = min(tq, STRIP)
    pieces = [(mp, slice(h * tc, (h + 1) * tc)) for h in range(tq // tc) for mp in range(2)]

    def scores(j, mp, cols, dst, key0=None):
        s = jnp.dot(maps[mp][0][j], maps[mp][1][:, cols], preferred_element_type=F32)
        if key0 is not None:
            krow = lax.broadcasted_iota(jnp.int32, s.shape, 0) + key0
            qcol = lax.broadcasted_iota(jnp.int32, s.shape, 1) + cols.start
            s = jnp.where(krow <= qcol, s, NEG_BIG)
        dst[mp][:, cols] = s
        return jnp.max(s, axis=0, keepdims=True)

    def accumulate(j, mp, cols, src, mx):
        _, _, a_sc, mr, _ = maps[mp]
        c = slope * jnp.full((1, tc), j * tk - qi * tq, jnp.int32).astype(F32)
        m = st[mr:mr + 1, cols]
        m_new = jnp.maximum(m, mx + c)
        alpha = jnp.exp2(m - m_new)
        p = jnp.exp2(src[mp][:, cols] - (m_new - c))
        st[mr:mr + 1, cols] = m_new
        a_sc[:, cols] = alpha * a_sc[:, cols] + jnp.dot(vt_sc[j], p.astype(BF16),
                                                        preferred_element_type=F32)

    bufs = ((sa1, sa2), (sb1, sb2))

    def park(maxima):
        for (mp, cols), x in zip(pieces, maxima):
            xr = maps[mp][4]
            st[xr:xr + 1, cols] = x

    def parked():
        return [st[maps[mp][4]:maps[mp][4] + 1, cols] for mp, cols in pieces]

    def step(s_blk, s_dst, p_blk, p_src, p_max, s_pieces=None, p_pieces=None):
        s_pieces = [(mp, cols, None) for mp, cols in pieces] if s_pieces is None else s_pieces
        p_pieces = pieces if p_pieces is None else p_pieces
        out = []
        for k in range(max(len(s_pieces), len(p_pieces))):
            if k < len(s_pieces):
                mp, cols, key0 = s_pieces[k]
                out.append(scores(s_blk, mp, cols, s_dst, key0))
            if k < len(p_pieces):
                mp, cols = p_pieces[k]
                accumulate(p_blk, mp, cols, p_src, p_max[k])
        return out

    def diag_pieces(d):
        out = []
        for mp, cols in pieces:
            if d * tk + tk - 1 <= cols.start:
                out.append((mp, cols, None))
            elif d * tk <= cols.stop - 1:
                out.append((mp, cols, d * tk))
        return out

    d0, d1 = diag_pieces(0), diag_pieces(1)
    d0_cols = [(mp, cols) for mp, cols, _ in d0]
    d1_cols = [(mp, cols) for mp, cols, _ in d1]
    first_diag = 2 * qi

    def finish(cur, nxt, x_prev=None, prev_blk=None):
        if x_prev is None:
            x0 = [scores(first_diag, mp, cols, nxt, key0) for mp, cols, key0 in d0]
        else:
            x0 = step(first_diag, nxt, prev_blk, cur, x_prev, s_pieces=d0)
        x1 = step(first_diag + 1, cur, first_diag, nxt, x0, s_pieces=d1, p_pieces=d0_cols)
        for (mp, cols), x in zip(d1_cols, x1):
            accumulate(first_diag + 1, mp, cols, cur, x)

    @pl.when(qi == 0)
    def _():
        finish(bufs[0], bufs[1])

    @pl.when(qi > 0)
    def _():
        park([scores(0, mp, cols, bufs[0]) for mp, cols in pieces])

        def pair(i, carry):
            j = 2 * i
            x = parked()
            x = step(j + 1, bufs[1], j, bufs[0], x)
            x = step(j + 2, bufs[0], j + 1, bufs[1], x)
            park(x)
            return carry

        lax.fori_loop(0, qi - 1, pair, 0)
        j = 2 * (qi - 1)
        x = step(j + 1, bufs[1], j, bufs[0], parked())
        finish(bufs[1], bufs[0], x_prev=x, prev_blk=j + 1)

    lam = (jnp.exp(jnp.sum(lq1_ref[...] * lk1_ref[...], axis=1, keepdims=True))
           - jnp.exp(jnp.sum(lq2_ref[...] * lk2_ref[...], axis=1, keepdims=True))
           + lam_init)
    o1 = a1[0:DV, :] / a1[DV:DV + 1, :]
    o2 = a2[0:DV, :] / a2[DV:DV + 1, :]
    o = (o1 - lam * o2).T
    y = o * lax.rsqrt(jnp.mean(jnp.square(o), axis=1, keepdims=True) + RMS_EPS)
    y = (y * w_ref[...]) * (1.0 - lam_init)
    o_ref[...] = y.astype(o_ref.dtype)


def _diff_attention(proj, slopes_l2, qaug, lq1, lk1, lq2, lk2, subln_w, *, batch, seq, lam_init, tq):
    n = proj.shape[0]
    nq = seq // tq
    tk = tq // 2
    assert tk % STRIP == 0 and tk % LANES == 0
    nk = seq // tk
    kern = functools.partial(_diff_attn_kernel, tq=tq, tk=tk, lam_init=lam_init)
    vec = pl.BlockSpec((1, HEAD_DIM), lambda b, h, i: (0, 0))
    cq, ck, cv = _COL_DQ // LANES, _COL_DK // LANES, _COL_DV // LANES
    return pl.pallas_call(
        kern,
        out_shape=jax.ShapeDtypeStruct((n, DIFF_HEADS * 2 * HEAD_DIM), BF16),
        grid=(batch, DIFF_HEADS, nq),
        in_specs=[pl.BlockSpec(memory_space=pltpu.SMEM),
                  vec, vec, vec, vec,
                  pl.BlockSpec((1, 2, LANES, tq), lambda b, h, i: (h, 0, 0, 0)),
                  pl.BlockSpec((tq, LANES), lambda b, h, i: (b * nq + i, cq + h)),
                  pl.BlockSpec((seq, LANES), lambda b, h, i: (b, ck + h)),
                  pl.BlockSpec((seq, LANES), lambda b, h, i: (b, cv + h)),
                  pl.BlockSpec((1, 2 * HEAD_DIM), lambda b, h, i: (0, 0))],
        out_specs=pl.BlockSpec((tq, LANES), lambda b, h, i: (b * nq + i, h)),
        scratch_shapes=[pltpu.VMEM((nk, tk, LANES), BF16), pltpu.VMEM((nk, tk, LANES), BF16),
                        pltpu.VMEM((nk, DV + ONES_ROWS, tk), BF16),
                        pltpu.VMEM((DV + ONES_ROWS, tq), F32), pltpu.VMEM((DV + ONES_ROWS, tq), F32),
                        pltpu.VMEM((8, tq), F32)] + [pltpu.VMEM((tk, tq), F32)] * 4,
        compiler_params=_params(("arbitrary", "arbitrary", "arbitrary")),
        name="diff_attn",
    )(slopes_l2, lq1, lk1, lq2, lk2, qaug, proj, proj, proj, subln_w)


def _diff_query_aug(slopes_l2, tq):
    s0 = slopes_l2.astype(BF16).astype(F32)
    s1 = (slopes_l2 - s0).astype(BF16).astype(F32)
    s2 = (slopes_l2 - s0 - s1).astype(BF16).astype(F32)
    parts = jnp.stack([s0, s1, s2, s0, s1, s2], axis=1)
    n_heads = slopes_l2.shape[0]
    cols = jnp.zeros((n_heads, 2, LANES), F32)
    cols = cols.at[:, 0, HEAD_DIM:HEAD_DIM + N_AUG].set(parts)
    cols = cols.at[:, 1, 0:N_AUG].set(parts)
    return jnp.broadcast_to(cols[:, :, :, None], (n_heads, 2, LANES, tq))


def _swa_kernel(slopes_ref, sinks_ref, q_ref, kp_ref, kc_ref, vp_ref, vc_ref, o_ref,
                bias_sc, sink_sc, ot_sc, *, tq):
    qi = pl.program_id(1)
    wide = SWA_GROUP * WINDOW

    @pl.when((pl.program_id(0) == 0) & (qi == 0))
    def _():
        key = lax.broadcasted_iota(jnp.int32, (2 * WINDOW, wide), 0)
        col = lax.broadcasted_iota(jnp.int32, (2 * WINDOW, wide), 1)
        dist = (col & (WINDOW - 1)) + WINDOW - key
        valid = (dist >= 0) & (dist < WINDOW)
        head = lax.broadcasted_iota(jnp.int32, (1, wide), 1) >> (WINDOW.bit_length() - 1)
        for hk in range(SWA_KV_HEADS):
            slope = jnp.zeros((1, wide), F32)
            sink = jnp.zeros((1, wide), F32)
            for g in range(SWA_GROUP):
                slope = jnp.where(head == g, slopes_ref[hk * SWA_GROUP + g], slope)
                sink = jnp.where(head == g, sinks_ref[hk * SWA_GROUP + g], sink)
            bias_sc[hk] = jnp.where(valid, -slope * dist.astype(F32), NEG_BIG)
            sink_sc[hk] = sink

    kcat = jnp.concatenate([kp_ref[...], kc_ref[...]], axis=0)
    vcat = jnp.concatenate([vp_ref[...], vc_ref[...]], axis=0)
    zeros = jnp.zeros((HEAD_DIM, WINDOW), F32)
    for j in range(tq // WINDOW):
        kj = kcat[j * WINDOW:(j + 2) * WINDOW]
        vt = vcat[j * WINDOW:(j + 2) * WINDOW].astype(F32).T
        qt = q_ref[j * WINDOW:(j + 1) * WINDOW, :].astype(F32).T
        for hk in range(SWA_KV_HEADS):
            cols = []
            for g in range(SWA_GROUP):
                hq = hk * SWA_GROUP + g
                qh = qt[hq * HEAD_DIM:(hq + 1) * HEAD_DIM]
                cols.append(jnp.concatenate([qh, zeros] if hk == 0 else [zeros, qh], axis=0))
            q8 = jnp.concatenate(cols, axis=1).astype(BF16)
            s = jnp.dot(kj, q8, preferred_element_type=F32) + bias_sc[hk]
            if j == 0:
                key = lax.broadcasted_iota(jnp.int32, s.shape, 0)
                s = jnp.where((key < WINDOW) & (qi == 0), NEG_BIG, s)
            sink = sink_sc[hk]
            m = jnp.maximum(jnp.max(s, axis=0, keepdims=True), sink)
            e = jnp.exp2(s - m)
            denom = jnp.sum(e, axis=0, keepdims=True) + jnp.exp2(sink - m)
            v_hk = vt[hk * HEAD_DIM:(hk + 1) * HEAD_DIM].astype(BF16)
            ot = jnp.dot(v_hk, e.astype(BF16), preferred_element_type=F32) / denom
            for g in range(SWA_GROUP):
                hq = hk * SWA_GROUP + g
                ot_sc[hq * HEAD_DIM:(hq + 1) * HEAD_DIM, :] = ot[:, g * WINDOW:(g + 1) * WINDOW]
        o_ref[j * WINDOW:(j + 1) * WINDOW, :] = ot_sc[...].T.astype(o_ref.dtype)


def _swa_attention(proj, slopes, sinks, *, batch, seq, tq):
    n = proj.shape[0]
    nq = seq // tq
    sub = tq // WINDOW
    nwin = seq // WINDOW
    kern = functools.partial(_swa_kernel, tq=tq)
    cq = _COL_SQ // (SWA_HEADS * HEAD_DIM)
    ck, cv = _COL_SK // LANES, _COL_SV // LANES
    prev = lambda c: (lambda b, i: (b * nwin + jnp.maximum(i * sub - 1, 0), c))
    cur = lambda c: (lambda b, i: (b * nq + i, c))
    smem = pl.BlockSpec(memory_space=pltpu.SMEM)
    return pl.pallas_call(
        kern,
        out_shape=jax.ShapeDtypeStruct((n, SWA_HEADS * HEAD_DIM), BF16),
        grid=(batch, nq),
        in_specs=[smem, smem,
                  pl.BlockSpec((tq, SWA_HEADS * HEAD_DIM), cur(cq)),
                  pl.BlockSpec((WINDOW, LANES), prev(ck)),
                  pl.BlockSpec((tq, LANES), cur(ck)),
                  pl.BlockSpec((WINDOW, LANES), prev(cv)),
                  pl.BlockSpec((tq, LANES), cur(cv))],
        out_specs=pl.BlockSpec((tq, SWA_HEADS * HEAD_DIM), lambda b, i: (b * nq + i, 0)),
        scratch_shapes=[pltpu.VMEM((SWA_KV_HEADS, 2 * WINDOW, SWA_GROUP * WINDOW), F32),
                        pltpu.VMEM((SWA_KV_HEADS, 1, SWA_GROUP * WINDOW), F32),
                        pltpu.VMEM((SWA_HEADS * HEAD_DIM, WINDOW), F32)],
        compiler_params=_params(("arbitrary", "arbitrary")),
        name="swa_attn",
    )(slopes, sinks, proj, proj, proj, proj, proj)


def _layer_norm(y, g, b):
    mu = jnp.mean(y, axis=1, keepdims=True)
    var = jnp.mean(jnp.square(y - mu), axis=1, keepdims=True)
    return (y - mu) * lax.rsqrt(var + LN_EPS) * g + b


def _merge_kernel(x_ref, od_ref, os_ref, ga_ref, gb_ref, p_ref,
                  wa_ref, wb_ref, wo_ref, bo_ref, g1_ref, b1_ref, wpg_ref, wpp_ref,
                  x1_ref, r_ref, *, dn_alpha):
    a = jnp.dot(od_ref[...], wa_ref[...], preferred_element_type=F32)
    b = jnp.dot(os_ref[...], wb_ref[...], preferred_element_type=F32)
    merged = (jax.nn.sigmoid(ga_ref[...].astype(F32)) * a
              + jax.nn.sigmoid(gb_ref[...].astype(F32)) * b)
    mix = jnp.dot(merged.astype(BF16), wo_ref[...], preferred_element_type=F32) + bo_ref[...]
    x1 = _layer_norm(dn_alpha * x_ref[...] + mix, g1_ref[...], b1_ref[...])
    x1_ref[...] = x1
    gate = jax.nn.sigmoid(jnp.dot(x1.astype(BF16), wpg_ref[...], preferred_element_type=F32))
    ple = gate * jnp.dot(p_ref[...].astype(BF16), wpp_ref[...], preferred_element_type=F32)
    r_ref[...] = dn_alpha * x1 + ple


def _merge(x, od, osw, proj, p, wa, wb, wo, bo, g1, b1, wpg, wpp, *, dn_alpha, tm):
    n, d = x.shape
    pd = p.shape[1]
    row = lambda c: (lambda i: (i, c))
    full = lambda shape: pl.BlockSpec(shape, lambda i: (0, 0))
    kern = functools.partial(_merge_kernel, dn_alpha=dn_alpha)
    return pl.pallas_call(
        kern,
        out_shape=(jax.ShapeDtypeStruct((n, d), F32), jax.ShapeDtypeStruct((n, d), F32)),
        grid=(n // tm,),
        in_specs=[pl.BlockSpec((tm, d), row(0)),
                  pl.BlockSpec((tm, d), row(0)),
                  pl.BlockSpec((tm, d), row(0)),
                  pl.BlockSpec((tm, d), row(_COL_GA // d)),
                  pl.BlockSpec((tm, d), row(_COL_GB // d)),
                  pl.BlockSpec((tm, pd), row(0)),
                  full((d, d)), full((d, d)), full((d, d)), full((1, d)),
                  full((1, d)), full((1, d)), full((d, d)), full((pd, d))],
        out_specs=(pl.BlockSpec((tm, d), row(0)), pl.BlockSpec((tm, d), row(0))),
        compiler_params=_params(("parallel",)),
        name="merge_ln1",
    )(x, od, osw, proj, proj, p, wa, wb, wo, bo, g1, b1, wpg, wpp)


def _router_kernel(x_ref, w_ref, b_ref, code_ref, gate_ref, cnt_ref, carry, *, tm):
    @pl.when(pl.program_id(0) == 0)
    def _():
        carry[...] = jnp.zeros(carry.shape, F32)

    logits = jnp.dot(x_ref[...], w_ref[...], preferred_element_type=F32,
                     precision=lax.Precision.HIGHEST) + b_ref[...]
    lane = lax.broadcasted_iota(jnp.int32, logits.shape, 1)
    lanef = lane.astype(F32)
    work = logits
    tops, idxs = [], []
    onehot = jnp.zeros(logits.shape, F32)
    for _ in range(TOP_K):
        m = jnp.max(work, axis=1, keepdims=True)
        idx = jnp.min(jnp.where(work == m, lanef, float(LANES)), axis=1, keepdims=True)
        sel = lanef == idx
        onehot = jnp.where(sel, 1.0, onehot)
        work = jnp.where(sel, -jnp.inf, work)
        tops.append(m)
        idxs.append(idx)
    es = [jnp.exp(t - tops[0]) for t in tops]
    denom = es[0] + es[1] + es[2] + es[3]
    r = lax.broadcasted_iota(jnp.int32, (tm, tm), 0)
    c = lax.broadcasted_iota(jnp.int32, (tm, tm), 1)
    tri = jnp.where(c < r, 1.0, 0.0).astype(BF16)
    before = jnp.dot(tri, onehot.astype(BF16), preferred_element_type=F32) + carry[0:1, :]
    code = jnp.zeros(logits.shape, jnp.int32)
    gate = jnp.zeros(logits.shape, F32)
    for k in range(TOP_K):
        rank = jnp.sum(jnp.where(lanef == idxs[k], before, 0.0), axis=1, keepdims=True)
        ck = (idxs[k] * float(1 << RANK_BITS) + rank).astype(jnp.int32)
        code = jnp.where(lane == k, ck, code)
        gate = jnp.where(lane == k, es[k] / denom, gate)
    code_ref[...] = code
    gate_ref[...] = gate
    carry[0:1, :] = carry[0:1, :] + jnp.sum(onehot, axis=0, keepdims=True)
    cnt_ref[...] = carry[...]


def _router(x1, w, b, *, tm):
    n, d = x1.shape
    kern = functools.partial(_router_kernel, tm=tm)
    return pl.pallas_call(
        kern,
        out_shape=(jax.ShapeDtypeStruct((n, LANES), jnp.int32),
                   jax.ShapeDtypeStruct((n, LANES), F32),
                   jax.ShapeDtypeStruct((8, LANES), F32)),
        grid=(n // tm,),
        in_specs=[pl.BlockSpec((tm, d), lambda i: (i, 0)),
                  pl.BlockSpec((d, LANES), lambda i: (0, 0)),
                  pl.BlockSpec((1, LANES), lambda i: (0, 0))],
        out_specs=(pl.BlockSpec((tm, LANES), lambda i: (i, 0)),
                   pl.BlockSpec((tm, LANES), lambda i: (i, 0)),
                   pl.BlockSpec((8, LANES), lambda i: (0, 0))),
        scratch_shapes=[pltpu.VMEM((8, LANES), F32)],
        compiler_params=_params(("arbitrary",)),
        name="router",
    )(x1, w, b)


def _dispatch_kernel(tail_ref, codes_ref, x_ref, xs_ref, codes_smem, zeros, csem, sem, zsem,
                     *, rows):
    i = pl.program_id(0)
    n_steps = pl.num_programs(0)
    n_assign = rows * TOP_K

    def codes_load(step):
        return pltpu.make_async_copy(codes_ref.at[pl.ds(step * n_assign, n_assign)],
                                     codes_smem, csem)

    @pl.when(i == 0)
    def _():
        codes_load(0).start()
        zeros[...] = jnp.zeros(zeros.shape, zeros.dtype)

        def fill(e):
            first = pl.multiple_of(tail_ref[e], MOE_BLOCK)
            return pltpu.make_async_copy(zeros, xs_ref.at[pl.ds(first, MOE_BLOCK)], zsem)

        def start(e, carry):
            @pl.when(tail_ref[e] >= 0)
            def _():
                fill(e).start()
            return carry

        def finish(e, carry):
            @pl.when(tail_ref[e] >= 0)
            def _():
                fill(e).wait()
            return carry

        def unused(b):
            first = pl.multiple_of(b * MOE_BLOCK, MOE_BLOCK)
            return pltpu.make_async_copy(zeros, xs_ref.at[pl.ds(first, MOE_BLOCK)], zsem)

        n_used = tail_ref[N_EXPERTS]
        n_blocks = xs_ref.shape[0] // MOE_BLOCK
        lax.fori_loop(0, N_EXPERTS, start, 0)
        lax.fori_loop(n_used, n_blocks, lambda b, c: (unused(b).start(), c)[1], 0)
        lax.fori_loop(0, N_EXPERTS, finish, 0)
        lax.fori_loop(n_used, n_blocks, lambda b, c: (unused(b).wait(), c)[1], 0)

    codes_load(i).wait()

    def issue(t, carry):
        src = x_ref.at[pl.ds(t, 1)]
        for k in range(TOP_K):
            slot = codes_smem[t * TOP_K + k]
            pltpu.make_async_copy(src, xs_ref.at[pl.ds(slot, 1)], sem).start(priority=k % 2)
        return carry

    lax.fori_loop(0, rows, issue, 0, unroll=2)

    @pl.when(i + 1 < n_steps)
    def _():
        codes_load(i + 1).start()

    pltpu.make_async_copy(xs_ref.at[pl.ds(0, n_assign)], xs_ref.at[pl.ds(0, n_assign)], sem).wait()


def _dispatch(tail, codes, x1, n_slots, *, rows):
    n, d = x1.shape
    kern = functools.partial(_dispatch_kernel, rows=rows)
    any_spec = pl.BlockSpec(memory_space=pl.ANY)
    return pl.pallas_call(
        kern,
        out_shape=jax.ShapeDtypeStruct((n_slots, d), x1.dtype),
        grid_spec=pltpu.PrefetchScalarGridSpec(
            num_scalar_prefetch=1,
            grid=(n // rows,),
            in_specs=[any_spec, pl.BlockSpec((rows, d), lambda i, tail: (i, 0))],
            out_specs=any_spec,
            scratch_shapes=[pltpu.SMEM((rows * TOP_K,), jnp.int32),
                            pltpu.VMEM((MOE_BLOCK, d), x1.dtype),
                            pltpu.SemaphoreType.DMA, pltpu.SemaphoreType.DMA,
                            pltpu.SemaphoreType.DMA]),
        compiler_params=_dma_params(("arbitrary",)),
        name="moe_dispatch",
    )(tail, codes, x1)


def _expert_kernel(be_ref, nu_ref, xs_ref, wgu_ref, bgu_ref, wd_ref, bd_ref, y_ref, wgu_b, wd_b):
    j = pl.program_id(0)
    dff = wd_ref.shape[2]

    @pl.when(j < nu_ref[0])
    def _():
        @pl.when((j == 0) | (be_ref[j] != be_ref[jnp.maximum(j - 1, 0)]))
        def _():
            wgu_b[...] = wgu_ref[0, 0].astype(BF16)
            wd_b[...] = wd_ref[0, 0].astype(BF16)

        gu = jnp.dot(xs_ref[...].astype(BF16), wgu_b[...], preferred_element_type=F32) + bgu_ref[0, 0]
        gate = jnp.minimum(gu[:, :dff], SWIGLU_LIMIT)
        up = jnp.clip(gu[:, dff:], -SWIGLU_LIMIT, SWIGLU_LIMIT)
        act = (up + 1.0) * (gate * jax.nn.sigmoid(SWIGLU_ALPHA * gate))
        y_ref[...] = jnp.dot(act.astype(BF16), wd_b[...], preferred_element_type=F32) + bd_ref[0, 0]

    @pl.when(j >= nu_ref[0])
    def _():
        y_ref[...] = jnp.zeros(y_ref.shape, y_ref.dtype)


def _experts(block_e, n_used, xs, wgu, bgu, wd, bd, *, layer):
    n_slots, d = xs.shape
    dff = wd.shape[2]
    nb = n_slots // MOE_BLOCK
    blk = lambda j, be, nu: jnp.minimum(j, nu[0] - 1)
    row = lambda j, be, nu: (blk(j, be, nu), 0)
    wsel = lambda j, be, nu: (layer, be[blk(j, be, nu)], 0, 0)
    return pl.pallas_call(
        _expert_kernel,
        out_shape=jax.ShapeDtypeStruct((n_slots, d), F32),
        grid_spec=pltpu.PrefetchScalarGridSpec(
            num_scalar_prefetch=2,
            grid=(nb,),
            in_specs=[pl.BlockSpec((MOE_BLOCK, d), row),
                      pl.BlockSpec((1, 1, d, 2 * dff), wsel),
                      pl.BlockSpec((1, 1, 1, 2 * dff), wsel),
                      pl.BlockSpec((1, 1, dff, d), wsel),
                      pl.BlockSpec((1, 1, 1, d), wsel)],
            out_specs=pl.BlockSpec((MOE_BLOCK, d), lambda j, be, nu: (j, 0)),
            scratch_shapes=[pltpu.VMEM((d, 2 * dff), BF16), pltpu.VMEM((dff, d), BF16)]),
        compiler_params=_params(("arbitrary",)),
        name="moe_experts",
    )(block_e, n_used, xs, wgu, bgu, wd, bd)


def _combine_kernel(codes_ref, gate_ref, r_ref, y_ref, g2_ref, b2_ref,
                    x2_ref, xb_ref, codes_smem, ybuf0, ybuf1, csem, sem0, sem1, *, rows):
    i = pl.program_id(0)
    n_steps = pl.num_programs(0)
    n_assign = rows * TOP_K

    def codes_load(step):
        return pltpu.make_async_copy(codes_ref.at[pl.ds(step * n_assign, n_assign)],
                                     codes_smem, csem)

    def gather(step, ybuf, sem):
        def issue(t, carry):
            for k in range(TOP_K):
                slot = codes_smem[t * TOP_K + k]
                pltpu.make_async_copy(y_ref.at[pl.ds(slot, 1)],
                                      ybuf.at[pl.ds(k * rows + t, 1)], sem).start(priority=k % 2)
            return carry

        lax.fori_loop(0, rows, issue, 0, unroll=2)

        @pl.when(step + 1 < n_steps)
        def _():
            codes_load(step + 1).start()

    @pl.when(i == 0)
    def _():
        first = codes_load(0)
        first.start()
        first.wait()
        gather(0, ybuf0, sem0)

    def step(ybuf, sem, ybuf_next, sem_next):
        @pl.when(i + 1 < n_steps)
        def _():
            codes_load(i + 1).wait()
            gather(i + 1, ybuf_next, sem_next)

        pltpu.make_async_copy(y_ref.at[pl.ds(0, n_assign)], ybuf, sem).wait()
        gate = gate_ref[...]
        acc = r_ref[...]
        for k in range(TOP_K):
            acc = acc + gate[:, k:k + 1] * ybuf[k * rows:(k + 1) * rows, :]
        x2 = _layer_norm(acc, g2_ref[...], b2_ref[...])
        x2_ref[...] = x2
        xb_ref[...] = x2.astype(BF16)

    @pl.when(i % 2 == 0)
    def _():
        step(ybuf0, sem0, ybuf1, sem1)

    @pl.when(i % 2 == 1)
    def _():
        step(ybuf1, sem1, ybuf0, sem0)


def _combine(codes, gates, r, y, g2, b2, *, rows):
    n, d = r.shape
    kern = functools.partial(_combine_kernel, rows=rows)
    any_spec = pl.BlockSpec(memory_space=pl.ANY)
    return pl.pallas_call(
        kern,
        out_shape=(jax.ShapeDtypeStruct((n, d), F32), jax.ShapeDtypeStruct((n, d), BF16)),
        grid=(n // rows,),
        in_specs=[any_spec,
                  pl.BlockSpec((rows, LANES), lambda i: (i, 0)),
                  pl.BlockSpec((rows, d), lambda i: (i, 0)),
                  any_spec,
                  pl.BlockSpec((1, d), lambda i: (0, 0)),
                  pl.BlockSpec((1, d), lambda i: (0, 0))],
        out_specs=(pl.BlockSpec((rows, d), lambda i: (i, 0)),
                   pl.BlockSpec((rows, d), lambda i: (i, 0))),
        scratch_shapes=[pltpu.SMEM((rows * TOP_K,), jnp.int32),
                        pltpu.VMEM((rows * TOP_K, d), F32), pltpu.VMEM((rows * TOP_K, d), F32),
                        pltpu.SemaphoreType.DMA, pltpu.SemaphoreType.DMA,
                        pltpu.SemaphoreType.DMA],
        compiler_params=_dma_params(("arbitrary",)),
        name="moe_combine_ln2",
    )(codes, gates, r, y, g2, b2)


def _alibi_slopes(n_heads):
    h = jnp.arange(1, n_heads + 1, dtype=F32)
    return jnp.exp2(-8.0 * h / n_heads)


def _block_size(n, target):
    t = min(n, target)
    while n % t:
        t //= 2
    return t


def kernel(x, p, w_in, b_in, lambda_q1, lambda_k1, lambda_q2, lambda_k2, subln_w, sinks,
           w_br_diff, w_br_swa, w_out, b_out, ln1_g, ln1_b, w_router, b_router,
           w_gate_up, b_gate_up, w_down, b_down, w_ple_gate, w_ple_proj, ln2_g, ln2_b):
    batch, seq, d = x.shape
    depth = w_in.shape[0]
    n = batch * seq
    dn_alpha = (2 * depth) ** 0.25
    n_assign = n * TOP_K
    n_blocks = n_assign // MOE_BLOCK + N_EXPERTS + 1
    n_slots = n_blocks * MOE_BLOCK

    n_in = w_in.shape[2]
    permute = lambda a: jnp.concatenate([a[..., n_in - 2 * d:], a[..., :n_in - 2 * d]], axis=-1)
    log2e = math.log2(math.e)
    diff_slopes = _alibi_slopes(DIFF_HEADS) * log2e
    swa_slopes = _alibi_slopes(SWA_HEADS) * log2e

    tm_lin = _block_size(n, 1024)
    tq_diff = _block_size(seq, 1024)
    tq_swa = _block_size(seq, 512)
    tm_merge = _block_size(n, 512)
    tm_router = _block_size(n, 512)
    rows_moe = _block_size(n, 256)

    xf = x.reshape(n, d)
    xb = xf
    for i in range(depth):
        lam_init = 0.8 - 0.6 * math.exp(-0.3 * i)
        col_scale = jnp.ones((n_in,), F32).at[_COL_DQ:_COL_DK].set(HEAD_DIM ** -0.5 * log2e)
        col_scale = col_scale.at[_COL_SQ:_COL_SK].set(HEAD_DIM ** -0.5 * log2e)
        w_in_b = (permute(w_in[i]) * col_scale).astype(BF16)
        b_in_p = (permute(b_in[i]) * col_scale)[None, :]
        proj = _linear(xb, w_in_b, b_in_p, tm=tm_lin, tn=1280)

        od = _diff_attention(proj, diff_slopes, _diff_query_aug(diff_slopes, tq_diff),
                             lambda_q1[i][None, :], lambda_k1[i][None, :],
                             lambda_q2[i][None, :], lambda_k2[i][None, :],
                             subln_w[i][None, :], batch=batch, seq=seq,
                             lam_init=lam_init, tq=tq_diff)
        osw = _swa_attention(proj, swa_slopes, sinks[i].astype(F32) * log2e,
                             batch=batch, seq=seq, tq=tq_swa)

        x1, r = _merge(xf, od, osw, proj, p[i].reshape(n, -1),
                       w_br_diff[i].astype(BF16), w_br_swa[i].astype(BF16),
                       w_out[i].astype(BF16), b_out[i][None, :],
                       ln1_g[i][None, :], ln1_b[i][None, :],
                       w_ple_gate[i].astype(BF16), w_ple_proj[i].astype(BF16),
                       dn_alpha=dn_alpha, tm=tm_merge)

        wr = jnp.zeros((d, LANES), F32).at[:, :N_EXPERTS].set(w_router[i])
        br = jnp.full((1, LANES), NEG_BIG, F32).at[0, :N_EXPERTS].set(b_router[i])
        code, gates, cnt = _router(x1, wr, br, tm=tm_router)

        counts = cnt[0, :N_EXPERTS].astype(jnp.int32)
        padded = (counts + MOE_BLOCK - 1) // MOE_BLOCK * MOE_BLOCK
        padded_end = jnp.cumsum(padded)
        off = (padded_end - padded).astype(jnp.int32)
        block_start = jnp.arange(n_blocks, dtype=jnp.int32) * MOE_BLOCK
        block_e = jnp.minimum(
            jnp.sum((block_start[:, None] >= padded_end[None, :]).astype(jnp.int32), axis=1),
            N_EXPERTS - 1)
        codes = code[:, :TOP_K].reshape(n_assign)
        codes = off[codes >> RANK_BITS] + (codes & ((1 << RANK_BITS) - 1))
        n_used = (padded_end[-1:] // MOE_BLOCK).astype(jnp.int32)
        tail = jnp.concatenate([jnp.where(padded > 0, padded_end - MOE_BLOCK, -1).astype(jnp.int32),
                                n_used])

        xs = _dispatch(tail, codes, x1, n_slots, rows=rows_moe)
        y = _experts(block_e, n_used, xs, w_gate_up, b_gate_up[:, :, None, :],
                     w_down, b_down[:, :, None, :], layer=i)
        xf, xb = _combine(codes, gates, r, y, ln2_g[i][None, :], ln2_b[i][None, :],
                          rows=rows_moe)
    return xf.reshape(batch, seq, d)
```

```python
import functools
import math

import jax
import jax.numpy as jnp
from jax import lax
from jax.experimental import pallas as pl
from jax.experimental.pallas import tpu as pltpu

F32 = jnp.float32
BF16 = jnp.bfloat16

HEAD_DIM = 64
DIFF_HEADS = 8
SWA_HEADS = 16
SWA_KV_HEADS = 2
SWA_GROUP = SWA_HEADS // SWA_KV_HEADS
WINDOW = 128
N_EXPERTS = 32
TOP_K = 4
MOE_BLOCK = 512
SWIGLU_LIMIT = 7.0
SWIGLU_ALPHA = 1.702
LN_EPS = 1e-5
RMS_EPS = 1e-5
NEG_BIG = -1e30

LANES = 128
VMEM_LIMIT = 56 * 1024 * 1024
RANK_BITS = 16

_COL_GA, _COL_GB, _COL_DQ, _COL_DK, _COL_DV, _COL_SQ, _COL_SK, _COL_SV = (
    0, 1024, 2048, 3072, 4096, 5120, 6144, 6272)


def _params(semantics):
    return pltpu.CompilerParams(dimension_semantics=semantics,
                                vmem_limit_bytes=VMEM_LIMIT)


def _dma_params(semantics):
    return pltpu.CompilerParams(dimension_semantics=semantics,
                                vmem_limit_bytes=VMEM_LIMIT,
                                disable_bounds_checks=True)


def _linear_kernel(x_ref, w_ref, b_ref, o_ref):
    acc = jnp.dot(x_ref[...].astype(BF16), w_ref[...], preferred_element_type=F32)
    o_ref[...] = (acc + b_ref[...]).astype(o_ref.dtype)


def _linear(x, w, b, *, tm, tn):
    n, k = x.shape
    nout = w.shape[1]
    return pl.pallas_call(
        _linear_kernel,
        out_shape=jax.ShapeDtypeStruct((n, nout), BF16),
        grid=(n // tm, nout // tn),
        in_specs=[pl.BlockSpec((tm, k), lambda i, j: (i, 0)),
                  pl.BlockSpec((k, tn), lambda i, j: (0, j)),
                  pl.BlockSpec((1, tn), lambda i, j: (0, j))],
        out_specs=pl.BlockSpec((tm, tn), lambda i, j: (i, j)),
        compiler_params=_params(("parallel", "arbitrary")),
        name="in_proj",
    )(x, w, b)


N_AUG = 6
STRIP = 256
DV = 2 * HEAD_DIM
ONES_ROWS = 16


def _key_aug(tk, first_lane):
    r = lax.broadcasted_iota(jnp.int32, (tk, LANES), 0)
    lane = lax.broadcasted_iota(jnp.int32, (tk, LANES), 1) - first_lane
    lane_bits = LANES.bit_length() - 1
    hi = ((r >> lane_bits) << lane_bits).astype(F32)
    lo = (r & (LANES - 1)).astype(F32)
    return jnp.where((lane >= 0) & (lane < 3), hi,
                     jnp.where((lane >= 3) & (lane < N_AUG), lo, 0.0))


def _diff_attn_kernel(slopes_ref, lq1_ref, lk1_ref, lq2_ref, lk2_ref, qaug_ref,
                      q_ref, k_ref, v_ref, w_ref, o_ref,
                      k1a_sc, k2a_sc, vt_sc, a1, a2, st, sa1, sa2, sb1, sb2, *, tq, tk, lam_init):
    h = pl.program_id(1)
    qi = pl.program_id(2)
    slope = slopes_ref[h]
    n_chunks = k1a_sc.shape[0]

    @pl.when(qi == 0)
    def _():
        lane = lax.broadcasted_iota(jnp.int32, (tk, LANES), 1)
        aug1 = _key_aug(tk, HEAD_DIM).astype(BF16)
        aug2 = _key_aug(tk, 0).astype(BF16)

        def build(c, carry):
            rows = pl.ds(pl.multiple_of(c * tk, tk), tk)
            k = k_ref[rows, :]
            k1a_sc[c] = jnp.where(lane < HEAD_DIM, k, aug1)
            k2a_sc[c] = jnp.where(lane >= HEAD_DIM, k, aug2)
            vt_sc[c, 0:DV, :] = v_ref[rows, :].astype(F32).T.astype(BF16)
            vt_sc[c, DV:DV + ONES_ROWS, :] = jnp.where(
                lax.broadcasted_iota(jnp.int32, (ONES_ROWS, tk), 0) == 0, 1.0, 0.0).astype(BF16)
            return carry

        lax.fori_loop(0, n_chunks, build, 0)

    qt = q_ref[...].astype(F32).T
    row = lax.broadcasted_iota(jnp.int32, qt.shape, 0)
    qt1 = jnp.where(row < HEAD_DIM, qt, qaug_ref[0, 0]).astype(BF16)
    qt2 = jnp.where(row >= HEAD_DIM, qt, qaug_ref[0, 1]).astype(BF16)
    a1[...] = jnp.zeros(a1.shape, F32)
    a2[...] = jnp.zeros(a2.shape, F32)

    M1, M2, XA1, XA2 = range(4)
    for r_ in (M1, M2):
        st[r_:r_ + 1, :] = jnp.full((1, tq), NEG_BIG, F32)

    maps = ((k1a_sc, qt1, a1, M1, XA1), (k2a_sc, qt2, a2, M2, XA2))
    tc = min(tq, STRIP)
    pieces = [(mp, slice(h * tc, (h + 1) * tc)) for h in range(tq // tc) for mp in range(2)]

    def scores(j, mp, cols, dst, key0=None):
        s = jnp.dot(maps[mp][0][j], maps[mp][1][:, cols], preferred_element_type=F32)
        if key0 is not None:
            krow = lax.broadcasted_iota(jnp.int32, s.shape, 0) + key0
            qcol = lax.broadcasted_iota(jnp.int32, s.shape, 1) + cols.start
            s = jnp.where(krow <= qcol, s, NEG_BIG)
        dst[mp][:, cols] = s
        return jnp.max(s, axis=0, keepdims=True)

    def accumulate(j, mp, cols, src, mx):
        _, _, a_sc, mr, _ = maps[mp]
        c = slope * jnp.full((1, tc), j * tk - qi * tq, jnp.int32).astype(F32)
        m = st[mr:mr + 1, cols]
        m_new = jnp.maximum(m, mx + c)
        alpha = jnp.exp2(m - m_new)
        p = jnp.exp2(src[mp][:, cols] - (m_new - c))
        st[mr:mr + 1, cols] = m_new
        a_sc[:, cols] = alpha * a_sc[:, cols] + jnp.dot(vt_sc[j], p.astype(BF16),
                                                        preferred_element_type=F32)

    bufs = ((sa1, sa2), (sb1, sb2))

    def park(maxima):
        for (mp, cols), x in zip(pieces, maxima):
            xr = maps[mp][4]
            st[xr:xr + 1, cols] = x

    def parked():
        return [st[maps[mp][4]:maps[mp][4] + 1, cols] for mp, cols in pieces]

    def step(s_blk, s_dst, p_blk, p_src, p_max, s_pieces=None, p_pieces=None):
        s_pieces = [(mp, cols, None) for mp, cols in pieces] if s_pieces is None else s_pieces
        p_pieces = pieces if p_pieces is None else p_pieces
        out = []
        for k in range(max(len(s_pieces), len(p_pieces))):
            if k < len(s_pieces):
                mp, cols, key0 = s_pieces[k]
                out.append(scores(s_blk, mp, cols, s_dst, key0))
            if k < len(p_pieces):
                mp, cols = p_pieces[k]
                accumulate(p_blk, mp, cols, p_src, p_max[k])
        return out

    def diag_pieces(d):
        out = []
        for mp, cols in pieces:
            if d * tk + tk - 1 <= cols.start:
                out.append((mp, cols, None))
            elif d * tk <= cols.stop - 1:
                out.append((mp, cols, d * tk))
        return out

    d0, d1 = diag_pieces(0), diag_pieces(1)
    d0_cols = [(mp, cols) for mp, cols, _ in d0]
    d1_cols = [(mp, cols) for mp, cols, _ in d1]
    first_diag = 2 * qi

    def finish(cur, nxt, x_prev=None, prev_blk=None):
        if x_prev is None:
            x0 = [scores(first_diag, mp, cols, nxt, key0) for mp, cols, key0 in d0]
        else:
            x0 = step(first_diag, nxt, prev_blk, cur, x_prev, s_pieces=d0)
        x1 = step(first_diag + 1, cur, first_diag, nxt, x0, s_pieces=d1, p_pieces=d0_cols)
        for (mp, cols), x in zip(d1_cols, x1):
            accumulate(first_diag + 1, mp, cols, cur, x)

    @pl.when(qi == 0)
    def _():
        finish(bufs[0], bufs[1])

    @pl.when(qi > 0)
    def _():
        park([scores(0, mp, cols, bufs[0]) for mp, cols in pieces])

        def pair(i, carry):
            j = 2 * i
            x = parked()
            x = step(j + 1, bufs[1], j, bufs[0], x)
            x = step(j + 2, bufs[0], j + 1, bufs[1], x)
            park(x)
            return carry

        lax.fori_loop(0, qi - 1, pair, 0)
        j = 2 * (qi - 1)
        x = step(j + 1, bufs[1], j, bufs[0], parked())
        finish(bufs[1], bufs[0], x_prev=x, prev_blk=j + 1)

    lam = (jnp.exp(jnp.sum(lq1_ref[...] * lk1_ref[...], axis=1, keepdims=True))
           - jnp.exp(jnp.sum(lq2_ref[...] * lk2_ref[...], axis=1, keepdims=True))
           + lam_init)
    o1 = a1[0:DV, :] / a1[DV:DV + 1, :]
    o2 = a2[0:DV, :] / a2[DV:DV + 1, :]
    o = (o1 - lam * o2).T
    y = o * lax.rsqrt(jnp.mean(jnp.square(o), axis=1, keepdims=True) + RMS_EPS)
    y = (y * w_ref[...]) * (1.0 - lam_init)
    o_ref[...] = y.astype(o_ref.dtype)


def _diff_attention(proj, slopes_l2, qaug, lq1, lk1, lq2, lk2, subln_w, *, batch, seq, lam_init, tq):
    n = proj.shape[0]
    nq = seq // tq
    tk = tq // 2
    assert tk % STRIP == 0 and tk % LANES == 0
    nk = seq // tk
    kern = functools.partial(_diff_attn_kernel, tq=tq, tk=tk, lam_init=lam_init)
    vec = pl.BlockSpec((1, HEAD_DIM), lambda b, h, i: (0, 0))
    cq, ck, cv = _COL_DQ // LANES, _COL_DK // LANES, _COL_DV // LANES
    return pl.pallas_call(
        kern,
        out_shape=jax.ShapeDtypeStruct((n, DIFF_HEADS * 2 * HEAD_DIM), BF16),
        grid=(batch, DIFF_HEADS, nq),
        in_specs=[pl.BlockSpec(memory_space=pltpu.SMEM),
                  vec, vec, vec, vec,
                  pl.BlockSpec((1, 2, LANES, tq), lambda b, h, i: (h, 0, 0, 0)),
                  pl.BlockSpec((tq, LANES), lambda b, h, i: (b * nq + i, cq + h)),
                  pl.BlockSpec((seq, LANES), lambda b, h, i: (b, ck + h)),
                  pl.BlockSpec((seq, LANES), lambda b, h, i: (b, cv + h)),
                  pl.BlockSpec((1, 2 * HEAD_DIM), lambda b, h, i: (0, 0))],
        out_specs=pl.BlockSpec((tq, LANES), lambda b, h, i: (b * nq + i, h)),
        scratch_shapes=[pltpu.VMEM((nk, tk, LANES), BF16), pltpu.VMEM((nk, tk, LANES), BF16),
                        pltpu.VMEM((nk, DV + ONES_ROWS, tk), BF16),
                        pltpu.VMEM((DV + ONES_ROWS, tq), F32), pltpu.VMEM((DV + ONES_ROWS, tq), F32),
                        pltpu.VMEM((8, tq), F32)] + [pltpu.VMEM((tk, tq), F32)] * 4,
        compiler_params=_params(("arbitrary", "arbitrary", "arbitrary")),
        name="diff_attn",
    )(slopes_l2, lq1, lk1, lq2, lk2, qaug, proj, proj, proj, subln_w)


def _diff_query_aug(slopes_l2, tq):
    s0 = slopes_l2.astype(BF16).astype(F32)
    s1 = (slopes_l2 - s0).astype(BF16).astype(F32)
    s2 = (slopes_l2 - s0 - s1).astype(BF16).astype(F32)
    parts = jnp.stack([s0, s1, s2, s0, s1, s2], axis=1)
    n_heads = slopes_l2.shape[0]
    cols = jnp.zeros((n_heads, 2, LANES), F32)
    cols = cols.at[:, 0, HEAD_DIM:HEAD_DIM + N_AUG].set(parts)
    cols = cols.at[:, 1, 0:N_AUG].set(parts)
    return jnp.broadcast_to(cols[:, :, :, None], (n_heads, 2, LANES, tq))


def _swa_kernel(slopes_ref, sinks_ref, q_ref, kp_ref, kc_ref, vp_ref, vc_ref, o_ref,
                bias_sc, sink_sc, ot_sc, *, tq):
    qi = pl.program_id(1)
    wide = SWA_GROUP * WINDOW

    @pl.when((pl.program_id(0) == 0) & (qi == 0))
    def _():
        key = lax.broadcasted_iota(jnp.int32, (2 * WINDOW, wide), 0)
        col = lax.broadcasted_iota(jnp.int32, (2 * WINDOW, wide), 1)
        dist = (col & (WINDOW - 1)) + WINDOW - key
        valid = (dist >= 0) & (dist < WINDOW)
        head = lax.broadcasted_iota(jnp.int32, (1, wide), 1) >> (WINDOW.bit_length() - 1)
        for hk in range(SWA_KV_HEADS):
            slope = jnp.zeros((1, wide), F32)
            sink = jnp.zeros((1, wide), F32)
            for g in range(SWA_GROUP):
                slope = jnp.where(head == g, slopes_ref[hk * SWA_GROUP + g], slope)
                sink = jnp.where(head == g, sinks_ref[hk * SWA_GROUP + g], sink)
            bias_sc[hk] = jnp.where(valid, -slope * dist.astype(F32), NEG_BIG)
            sink_sc[hk] = sink

    kcat = jnp.concatenate([kp_ref[...], kc_ref[...]], axis=0)
    vcat = jnp.concatenate([vp_ref[...], vc_ref[...]], axis=0)
    zeros = jnp.zeros((HEAD_DIM, WINDOW), F32)
    for j in range(tq // WINDOW):
        kj = kcat[j * WINDOW:(j + 2) * WINDOW]
        vt = vcat[j * WINDOW:(j + 2) * WINDOW].astype(F32).T
        qt = q_ref[j * WINDOW:(j + 1) * WINDOW, :].astype(F32).T
        for hk in range(SWA_KV_HEADS):
            cols = []
            for g in range(SWA_GROUP):
                hq = hk * SWA_GROUP + g
                qh = qt[hq * HEAD_DIM:(hq + 1) * HEAD_DIM]
                cols.append(jnp.concatenate([qh, zeros] if hk == 0 else [zeros, qh], axis=0))
            q8 = jnp.concatenate(cols, axis=1).astype(BF16)
            s = jnp.dot(kj, q8, preferred_element_type=F32) + bias_sc[hk]
            if j == 0:
                key = lax.broadcasted_iota(jnp.int32, s.shape, 0)
                s = jnp.where((key < WINDOW) & (qi == 0), NEG_BIG, s)
            sink = sink_sc[hk]
            m = jnp.maximum(jnp.max(s, axis=0, keepdims=True), sink)
            e = jnp.exp2(s - m)
            denom = jnp.sum(e, axis=0, keepdims=True) + jnp.exp2(sink - m)
            v_hk = vt[hk * HEAD_DIM:(hk + 1) * HEAD_DIM].astype(BF16)
            ot = jnp.dot(v_hk, e.astype(BF16), preferred_element_type=F32) / denom
            for g in range(SWA_GROUP):
                hq = hk * SWA_GROUP + g
                ot_sc[hq * HEAD_DIM:(hq + 1) * HEAD_DIM, :] = ot[:, g * WINDOW:(g + 1) * WINDOW]
        o_ref[j * WINDOW:(j + 1) * WINDOW, :] = ot_sc[...].T.astype(o_ref.dtype)


def _swa_attention(proj, slopes, sinks, *, batch, seq, tq):
    n = proj.shape[0]
    nq = seq // tq
    sub = tq // WINDOW
    nwin = seq // WINDOW
    kern = functools.partial(_swa_kernel, tq=tq)
    cq = _COL_SQ // (SWA_HEADS * HEAD_DIM)
    ck, cv = _COL_SK // LANES, _COL_SV // LANES
    prev = lambda c: (lambda b, i: (b * nwin + jnp.maximum(i * sub - 1, 0), c))
    cur = lambda c: (lambda b, i: (b * nq + i, c))
    smem = pl.BlockSpec(memory_space=pltpu.SMEM)
    return pl.pallas_call(
        kern,
        out_shape=jax.ShapeDtypeStruct((n, SWA_HEADS * HEAD_DIM), BF16),
        grid=(batch, nq),
        in_specs=[smem, smem,
                  pl.BlockSpec((tq, SWA_HEADS * HEAD_DIM), cur(cq)),
                  pl.BlockSpec((WINDOW, LANES), prev(ck)),
                  pl.BlockSpec((tq, LANES), cur(ck)),
                  pl.BlockSpec((WINDOW, LANES), prev(cv)),
                  pl.BlockSpec((tq, LANES), cur(cv))],
        out_specs=pl.BlockSpec((tq, SWA_HEADS * HEAD_DIM), lambda b, i: (b * nq + i, 0)),
        scratch_shapes=[pltpu.VMEM((SWA_KV_HEADS, 2 * WINDOW, SWA_GROUP * WINDOW), F32),
                        pltpu.VMEM((SWA_KV_HEADS, 1, SWA_GROUP * WINDOW), F32),
                        pltpu.VMEM((SWA_HEADS * HEAD_DIM, WINDOW), F32)],
        compiler_params=_params(("arbitrary", "arbitrary")),
        name="swa_attn",
    )(slopes, sinks, proj, proj, proj, proj, proj)


def _layer_norm(y, g, b):
    mu = jnp.mean(y, axis=1, keepdims=True)
    var = jnp.mean(jnp.square(y - mu), axis=1, keepdims=True)
    return (y - mu) * lax.rsqrt(var + LN_EPS) * g + b


def _merge_kernel(x_ref, od_ref, os_ref, ga_ref, gb_ref, p_ref,
                  wa_ref, wb_ref, wo_ref, bo_ref, g1_ref, b1_ref, wpg_ref, wpp_ref,
                  x1_ref, r_ref, *, dn_alpha):
    a = jnp.dot(od_ref[...], wa_ref[...], preferred_element_type=F32)
    b = jnp.dot(os_ref[...], wb_ref[...], preferred_element_type=F32)
    merged = (jax.nn.sigmoid(ga_ref[...].astype(F32)) * a
              + jax.nn.sigmoid(gb_ref[...].astype(F32)) * b)
    mix = jnp.dot(merged.astype(BF16), wo_ref[...], preferred_element_type=F32) + bo_ref[...]
    x1 = _layer_norm(dn_alpha * x_ref[...] + mix, g1_ref[...], b1_ref[...])
    x1_ref[...] = x1
    gate = jax.nn.sigmoid(jnp.dot(x1.astype(BF16), wpg_ref[...], preferred_element_type=F32))
    ple = gate * jnp.dot(p_ref[...].astype(BF16), wpp_ref[...], preferred_element_type=F32)
    r_ref[...] = dn_alpha * x1 + ple


def _merge(x, od, osw, proj, p, wa, wb, wo, bo, g1, b1, wpg, wpp, *, dn_alpha, tm):
    n, d = x.shape
    pd = p.shape[1]
    row = lambda c: (lambda i: (i, c))
    full = lambda shape: pl.BlockSpec(shape, lambda i: (0, 0))
    kern = functools.partial(_merge_kernel, dn_alpha=dn_alpha)
    return pl.pallas_call(
        kern,
        out_shape=(jax.ShapeDtypeStruct((n, d), F32), jax.ShapeDtypeStruct((n, d), F32)),
        grid=(n // tm,),
        in_specs=[pl.BlockSpec((tm, d), row(0)),
                  pl.BlockSpec((tm, d), row(0)),
                  pl.BlockSpec((tm, d), row(0)),
                  pl.BlockSpec((tm, d), row(_COL_GA // d)),
                  pl.BlockSpec((tm, d), row(_COL_GB // d)),
                  pl.BlockSpec((tm, pd), row(0)),
                  full((d, d)), full((d, d)), full((d, d)), full((1, d)),
                  full((1, d)), full((1, d)), full((d, d)), full((pd, d))],
        out_specs=(pl.BlockSpec((tm, d), row(0)), pl.BlockSpec((tm, d), row(0))),
        compiler_params=_params(("parallel",)),
        name="merge_ln1",
    )(x, od, osw, proj, proj, p, wa, wb, wo, bo, g1, b1, wpg, wpp)


def _router_kernel(x_ref, w_ref, b_ref, code_ref, gate_ref, cnt_ref, carry, *, tm):
    @pl.when(pl.program_id(0) == 0)
    def _():
        carry[...] = jnp.zeros(carry.shape, F32)

    logits = jnp.dot(x_ref[...], w_ref[...], preferred_element_type=F32,
                     precision=lax.Precision.HIGHEST) + b_ref[...]
    lane = lax.broadcasted_iota(jnp.int32, logits.shape, 1)
    lanef = lane.astype(F32)
    work = logits
    tops, idxs = [], []
    onehot = jnp.zeros(logits.shape, F32)
    for _ in range(TOP_K):
        m = jnp.max(work, axis=1, keepdims=True)
        idx = jnp.min(jnp.where(work == m, lanef, float(LANES)), axis=1, keepdims=True)
        sel = lanef == idx
        onehot = jnp.where(sel, 1.0, onehot)
        work = jnp.where(sel, -jnp.inf, work)
        tops.append(m)
        idxs.append(idx)
    es = [jnp.exp(t - tops[0]) for t in tops]
    denom = es[0] + es[1] + es[2] + es[3]
    r = lax.broadcasted_iota(jnp.int32, (tm, tm), 0)
    c = lax.broadcasted_iota(jnp.int32, (tm, tm), 1)
    tri = jnp.where(c < r, 1.0, 0.0).astype(BF16)
    before = jnp.dot(tri, onehot.astype(BF16), preferred_element_type=F32) + carry[0:1, :]
    code = jnp.zeros(logits.shape, jnp.int32)
    gate = jnp.zeros(logits.shape, F32)
    for k in range(TOP_K):
        rank = jnp.sum(jnp.where(lanef == idxs[k], before, 0.0), axis=1, keepdims=True)
        ck = (idxs[k] * float(1 << RANK_BITS) + rank).astype(jnp.int32)
        code = jnp.where(lane == k, ck, code)
        gate = jnp.where(lane == k, es[k] / denom, gate)
    code_ref[...] = code
    gate_ref[...] = gate
    carry[0:1, :] = carry[0:1, :] + jnp.sum(onehot, axis=0, keepdims=True)
    cnt_ref[...] = carry[...]


def _router(x1, w, b, *, tm):
    n, d = x1.shape
    kern = functools.partial(_router_kernel, tm=tm)
    return pl.pallas_call(
        kern,
        out_shape=(jax.ShapeDtypeStruct((n, LANES), jnp.int32),
                   jax.ShapeDtypeStruct((n, LANES), F32),
                   jax.ShapeDtypeStruct((8, LANES), F32)),
        grid=(n // tm,),
        in_specs=[pl.BlockSpec((tm, d), lambda i: (i, 0)),
                  pl.BlockSpec((d, LANES), lambda i: (0, 0)),
                  pl.BlockSpec((1, LANES), lambda i: (0, 0))],
        out_specs=(pl.BlockSpec((tm, LANES), lambda i: (i, 0)),
                   pl.BlockSpec((tm, LANES), lambda i: (i, 0)),
                   pl.BlockSpec((8, LANES), lambda i: (0, 0))),
        scratch_shapes=[pltpu.VMEM((8, LANES), F32)],
        compiler_params=_params(("arbitrary",)),
        name="router",
    )(x1, w, b)


def _dispatch_kernel(tail_ref, codes_ref, x_ref, xs_ref, codes_smem, zeros, csem, sem, zsem,
                     *, rows):
    i = pl.program_id(0)
    n_steps = pl.num_programs(0)
    n_assign = rows * TOP_K

    def codes_load(step):
        return pltpu.make_async_copy(codes_ref.at[pl.ds(step * n_assign, n_assign)],
                                     codes_smem, csem)

    @pl.when(i == 0)
    def _():
        codes_load(0).start()
        zeros[...] = jnp.zeros(zeros.shape, zeros.dtype)

        def fill(e):
            first = pl.multiple_of(tail_ref[e], MOE_BLOCK)
            return pltpu.make_async_copy(zeros, xs_ref.at[pl.ds(first, MOE_BLOCK)], zsem)

        def start(e, carry):
            @pl.when(tail_ref[e] >= 0)
            def _():
                fill(e).start()
            return carry

        def finish(e, carry):
            @pl.when(tail_ref[e] >= 0)
            def _():
                fill(e).wait()
            return carry

        def unused(b):
            first = pl.multiple_of(b * MOE_BLOCK, MOE_BLOCK)
            return pltpu.make_async_copy(zeros, xs_ref.at[pl.ds(first, MOE_BLOCK)], zsem)

        n_used = tail_ref[N_EXPERTS]
        n_blocks = xs_ref.shape[0] // MOE_BLOCK
        lax.fori_loop(0, N_EXPERTS, start, 0)
        lax.fori_loop(n_used, n_blocks, lambda b, c: (unused(b).start(), c)[1], 0)
        lax.fori_loop(0, N_EXPERTS, finish, 0)
        lax.fori_loop(n_used, n_blocks, lambda b, c: (unused(b).wait(), c)[1], 0)

    codes_load(i).wait()

    def issue(t, carry):
        src = x_ref.at[pl.ds(t, 1)]
        for k in range(TOP_K):
            slot = codes_smem[t * TOP_K + k]
            pltpu.make_async_copy(src, xs_ref.at[pl.ds(slot, 1)], sem).start(priority=k % 2)
        return carry

    lax.fori_loop(0, rows, issue, 0, unroll=2)

    @pl.when(i + 1 < n_steps)
    def _():
        codes_load(i + 1).start()

    pltpu.make_async_copy(xs_ref.at[pl.ds(0, n_assign)], xs_ref.at[pl.ds(0, n_assign)], sem).wait()


def _dispatch(tail, codes, x1, n_slots, *, rows):
    n, d = x1.shape
    kern = functools.partial(_dispatch_kernel, rows=rows)
    any_spec = pl.BlockSpec(memory_space=pl.ANY)
    return pl.pallas_call(
        kern,
        out_shape=jax.ShapeDtypeStruct((n_slots, d), x1.dtype),
        grid_spec=pltpu.PrefetchScalarGridSpec(
            num_scalar_prefetch=1,
            grid=(n // rows,),
            in_specs=[any_spec, pl.BlockSpec((rows, d), lambda i, tail: (i, 0))],
            out_specs=any_spec,
            scratch_shapes=[pltpu.SMEM((rows * TOP_K,), jnp.int32),
                            pltpu.VMEM((MOE_BLOCK, d), x1.dtype),
                            pltpu.SemaphoreType.DMA, pltpu.SemaphoreType.DMA,
                            pltpu.SemaphoreType.DMA]),
        compiler_params=_dma_params(("arbitrary",)),
        name="moe_dispatch",
    )(tail, codes, x1)


def _expert_kernel(be_ref, nu_ref, xs_ref, wgu_ref, bgu_ref, wd_ref, bd_ref, y_ref, wgu_b, wd_b):
    j = pl.program_id(0)
    dff = wd_ref.shape[2]

    @pl.when(j < nu_ref[0])
    def _():
        @pl.when((j == 0) | (be_ref[j] != be_ref[jnp.maximum(j - 1, 0)]))
        def _():
            wgu_b[...] = wgu_ref[0, 0].astype(BF16)
            wd_b[...] = wd_ref[0, 0].astype(BF16)

        gu = jnp.dot(xs_ref[...].astype(BF16), wgu_b[...], preferred_element_type=F32) + bgu_ref[0, 0]
        gate = jnp.minimum(gu[:, :dff], SWIGLU_LIMIT)
        up = jnp.clip(gu[:, dff:], -SWIGLU_LIMIT, SWIGLU_LIMIT)
        act = (up + 1.0) * (gate * jax.nn.sigmoid(SWIGLU_ALPHA * gate))
        y_ref[...] = jnp.dot(act.astype(BF16), wd_b[...], preferred_element_type=F32) + bd_ref[0, 0]

    @pl.when(j >= nu_ref[0])
    def _():
        y_ref[...] = jnp.zeros(y_ref.shape, y_ref.dtype)


def _experts(block_e, n_used, xs, wgu, bgu, wd, bd, *, layer):
    n_slots, d = xs.shape
    dff = wd.shape[2]
    nb = n_slots // MOE_BLOCK
    blk = lambda j, be, nu: jnp.minimum(j, nu[0] - 1)
    row = lambda j, be, nu: (blk(j, be, nu), 0)
    wsel = lambda j, be, nu: (layer, be[blk(j, be, nu)], 0, 0)
    return pl.pallas_call(
        _expert_kernel,
        out_shape=jax.ShapeDtypeStruct((n_slots, d), F32),
        grid_spec=pltpu.PrefetchScalarGridSpec(
            num_scalar_prefetch=2,
            grid=(nb,),
            in_specs=[pl.BlockSpec((MOE_BLOCK, d), row),
                      pl.BlockSpec((1, 1, d, 2 * dff), wsel),
                      pl.BlockSpec((1, 1, 1, 2 * dff), wsel),
                      pl.BlockSpec((1, 1, dff, d), wsel),
                      pl.BlockSpec((1, 1, 1, d), wsel)],
            out_specs=pl.BlockSpec((MOE_BLOCK, d), lambda j, be, nu: (j, 0)),
            scratch_shapes=[pltpu.VMEM((d, 2 * dff), BF16), pltpu.VMEM((dff, d), BF16)]),
        compiler_params=_params(("arbitrary",)),
        name="moe_experts",
    )(block_e, n_used, xs, wgu, bgu, wd, bd)


def _combine_kernel(codes_ref, gate_ref, r_ref, y_ref, g2_ref, b2_ref,
                    x2_ref, xb_ref, codes_smem, ybuf0, ybuf1, csem, sem0, sem1, *, rows):
    i = pl.program_id(0)
    n_steps = pl.num_programs(0)
    n_assign = rows * TOP_K

    def codes_load(step):
        return pltpu.make_async_copy(codes_ref.at[pl.ds(step * n_assign, n_assign)],
                                     codes_smem, csem)

    def gather(step, ybuf, sem):
        def issue(t, carry):
            for k in range(TOP_K):
                slot = codes_smem[t * TOP_K + k]
                pltpu.make_async_copy(y_ref.at[pl.ds(slot, 1)],
                                      ybuf.at[k, pl.ds(t, 1)], sem).start(priority=k % 2)
            return carry

        lax.fori_loop(0, rows, issue, 0, unroll=2)

        @pl.when(step + 1 < n_steps)
        def _():
            codes_load(step + 1).start()

    @pl.when(i == 0)
    def _():
        first = codes_load(0)
        first.start()
        first.wait()
        gather(0, ybuf0, sem0)

    def step(ybuf, sem, ybuf_next, sem_next):
        @pl.when(i + 1 < n_steps)
        def _():
            codes_load(i + 1).wait()
            gather(i + 1, ybuf_next, sem_next)

        for k in range(TOP_K):
            pltpu.make_async_copy(y_ref.at[pl.ds(0, rows)], ybuf.at[k], sem).wait()
        gate = gate_ref[...]
        acc = r_ref[...]
        for k in range(TOP_K):
            acc = acc + gate[:, k:k + 1] * ybuf[k]
        x2 = _layer_norm(acc, g2_ref[...], b2_ref[...])
        x2_ref[...] = x2
        xb_ref[...] = x2.astype(BF16)

    @pl.when(i % 2 == 0)
    def _():
        step(ybuf0, sem0, ybuf1, sem1)

    @pl.when(i % 2 == 1)
    def _():
        step(ybuf1, sem1, ybuf0, sem0)


def _combine(codes, gates, r, y, g2, b2, *, rows):
    n, d = r.shape
    kern = functools.partial(_combine_kernel, rows=rows)
    any_spec = pl.BlockSpec(memory_space=pl.ANY)
    return pl.pallas_call(
        kern,
        out_shape=(jax.ShapeDtypeStruct((n, d), F32), jax.ShapeDtypeStruct((n, d), BF16)),
        grid=(n // rows,),
        in_specs=[any_spec,
                  pl.BlockSpec((rows, LANES), lambda i: (i, 0)),
                  pl.BlockSpec((rows, d), lambda i: (i, 0)),
                  any_spec,
                  pl.BlockSpec((1, d), lambda i: (0, 0)),
                  pl.BlockSpec((1, d), lambda i: (0, 0))],
        out_specs=(pl.BlockSpec((rows, d), lambda i: (i, 0)),
                   pl.BlockSpec((rows, d), lambda i: (i, 0))),
        scratch_shapes=[pltpu.SMEM((rows * TOP_K,), jnp.int32),
                        pltpu.VMEM((TOP_K, rows, d), F32), pltpu.VMEM((TOP_K, rows, d), F32),
                        pltpu.SemaphoreType.DMA, pltpu.SemaphoreType.DMA,
                        pltpu.SemaphoreType.DMA],
        compiler_params=_dma_params(("arbitrary",)),
        name="moe_combine_ln2",
    )(codes, gates, r, y, g2, b2)


def _alibi_slopes(n_heads):
    h = jnp.arange(1, n_heads + 1, dtype=F32)
    return jnp.exp2(-8.0 * h / n_heads)


def _block_size(n, target):
    t = min(n, target)
    while n % t:
        t //= 2
    return t


def kernel(x, p, w_in, b_in, lambda_q1, lambda_k1, lambda_q2, lambda_k2, subln_w, sinks,
           w_br_diff, w_br_swa, w_out, b_out, ln1_g, ln1_b, w_router, b_router,
           w_gate_up, b_gate_up, w_down, b_down, w_ple_gate, w_ple_proj, ln2_g, ln2_b):
    batch, seq, d = x.shape
    depth = w_in.shape[0]
    n = batch * seq
    dn_alpha = (2 * depth) ** 0.25
    n_assign = n * TOP_K
    n_blocks = n_assign // MOE_BLOCK + N_EXPERTS + 1
    n_slots = n_blocks * MOE_BLOCK

    n_in = w_in.shape[2]
    permute = lambda a: jnp.concatenate([a[..., n_in - 2 * d:], a[..., :n_in - 2 * d]], axis=-1)
    log2e = math.log2(math.e)
    diff_slopes = _alibi_slopes(DIFF_HEADS) * log2e
    swa_slopes = _alibi_slopes(SWA_HEADS) * log2e

    tm_lin = _block_size(n, 1024)
    tn_lin = _block_size(n_in, 1280)
    tq_diff = _block_size(seq, 1024)
    tq_swa = _block_size(seq, 512)
    tm_merge = _block_size(n, 512)
    tm_router = _block_size(n, 512)
    rows_moe = _block_size(n, 256)

    xf = x.reshape(n, d)
    xb = xf
    for i in range(depth):
        lam_init = 0.8 - 0.6 * math.exp(-0.3 * i)
        col_scale = jnp.ones((n_in,), F32).at[_COL_DQ:_COL_DK].set(HEAD_DIM ** -0.5 * log2e)
        col_scale = col_scale.at[_COL_SQ:_COL_SK].set(HEAD_DIM ** -0.5 * log2e)
        w_in_b = (permute(w_in[i]) * col_scale).astype(BF16)
        b_in_p = (permute(b_in[i]) * col_scale)[None, :]
        proj = _linear(xb, w_in_b, b_in_p, tm=tm_lin, tn=tn_lin)

        od = _diff_attention(proj, diff_slopes, _diff_query_aug(diff_slopes, tq_diff),
                             lambda_q1[i][None, :], lambda_k1[i][None, :],
                             lambda_q2[i][None, :], lambda_k2[i][None, :],
                             subln_w[i][None, :], batch=batch, seq=seq,
                             lam_init=lam_init, tq=tq_diff)
        osw = _swa_attention(proj, swa_slopes, sinks[i].astype(F32) * log2e,
                             batch=batch, seq=seq, tq=tq_swa)

        x1, r = _merge(xf, od, osw, proj, p[i].reshape(n, -1),
                       w_br_diff[i].astype(BF16), w_br_swa[i].astype(BF16),
                       w_out[i].astype(BF16), b_out[i][None, :],
                       ln1_g[i][None, :], ln1_b[i][None, :],
                       w_ple_gate[i].astype(BF16), w_ple_proj[i].astype(BF16),
                       dn_alpha=dn_alpha, tm=tm_merge)

        wr = jnp.zeros((d, LANES), F32).at[:, :N_EXPERTS].set(w_router[i])
        br = jnp.full((1, LANES), NEG_BIG, F32).at[0, :N_EXPERTS].set(b_router[i])
        code, gates, cnt = _router(x1, wr, br, tm=tm_router)

        counts = cnt[0, :N_EXPERTS].astype(jnp.int32)
        padded = (counts + MOE_BLOCK - 1) // MOE_BLOCK * MOE_BLOCK
        padded_end = jnp.cumsum(padded)
        off = (padded_end - padded).astype(jnp.int32)
        block_start = jnp.arange(n_blocks, dtype=jnp.int32) * MOE_BLOCK
        block_e = jnp.minimum(
            jnp.sum((block_start[:, None] >= padded_end[None, :]).astype(jnp.int32), axis=1),
            N_EXPERTS - 1)
        codes = code[:, :TOP_K].reshape(n_assign)
        codes = off[codes >> RANK_BITS] + (codes & ((1 << RANK_BITS) - 1))
        n_used = (padded_end[-1:] // MOE_BLOCK).astype(jnp.int32)
        tail = jnp.concatenate([jnp.where(padded > 0, padded_end - MOE_BLOCK, -1).astype(jnp.int32),
                                n_used])

        xs = _dispatch(tail, codes, x1, n_slots, rows=rows_moe)
        y = _experts(block_e, n_used, xs, w_gate_up, b_gate_up[:, :, None, :],
                     w_down, b_down[:, :, None, :], layer=i)
        xf, xb = _combine(codes, gates, r, y, ln2_g[i][None, :], ln2_b[i][None, :],
                          rows=rows_moe)
    return xf.reshape(batch, seq, d)
```

```python
import functools
import math

import jax
import jax.numpy as jnp
from jax import lax
from jax.experimental import pallas as pl
from jax.experimental.pallas import tpu as pltpu

F32 = jnp.float32
BF16 = jnp.bfloat16

HEAD_DIM = 64
DIFF_HEADS = 8
SWA_HEADS = 16
SWA_KV_HEADS = 2
SWA_GROUP = SWA_HEADS // SWA_KV_HEADS
WINDOW = 128
N_EXPERTS = 32
TOP_K = 4
MOE_BLOCK = 512
SWIGLU_LIMIT = 7.0
SWIGLU_ALPHA = 1.702
LN_EPS = 1e-5
RMS_EPS = 1e-5
NEG_BIG = -1e30

LANES = 128
VMEM_LIMIT = 56 * 1024 * 1024
RANK_BITS = 16

_COL_GA, _COL_GB, _COL_DQ, _COL_DK, _COL_DV, _COL_SQ, _COL_SK, _COL_SV = (
    0, 1024, 2048, 3072, 4096, 5120, 6144, 6272)


def _params(semantics):
    return pltpu.CompilerParams(dimension_semantics=semantics,
                                vmem_limit_bytes=VMEM_LIMIT)


def _dma_params(semantics):
    return pltpu.CompilerParams(dimension_semantics=semantics,
                                vmem_limit_bytes=VMEM_LIMIT,
                                disable_bounds_checks=True)


def _linear_kernel(x_ref, w_ref, b_ref, o_ref):
    acc = jnp.dot(x_ref[...].astype(BF16), w_ref[...], preferred_element_type=F32)
    o_ref[...] = (acc + b_ref[...]).astype(o_ref.dtype)


def _linear(x, w, b, *, tm, tn):
    n, k = x.shape
    nout = w.shape[1]
    return pl.pallas_call(
        _linear_kernel,
        out_shape=jax.ShapeDtypeStruct((n, nout), BF16),
        grid=(n // tm, nout // tn),
        in_specs=[pl.BlockSpec((tm, k), lambda i, j: (i, 0)),
                  pl.BlockSpec((k, tn), lambda i, j: (0, j)),
                  pl.BlockSpec((1, tn), lambda i, j: (0, j))],
        out_specs=pl.BlockSpec((tm, tn), lambda i, j: (i, j)),
        compiler_params=_params(("parallel", "arbitrary")),
        name="in_proj",
    )(x, w, b)


N_AUG = 6
STRIP = 256
DV = 2 * HEAD_DIM
ONES_ROWS = 16


def _key_aug(tk, first_lane):
    r = lax.broadcasted_iota(jnp.int32, (tk, LANES), 0)
    lane = lax.broadcasted_iota(jnp.int32, (tk, LANES), 1) - first_lane
    lane_bits = LANES.bit_length() - 1
    hi = ((r >> lane_bits) << lane_bits).astype(F32)
    lo = (r & (LANES - 1)).astype(F32)
    return jnp.where((lane >= 0) & (lane < 3), hi,
                     jnp.where((lane >= 3) & (lane < N_AUG), lo, 0.0))


def _diff_attn_kernel(slopes_ref, lq1_ref, lk1_ref, lq2_ref, lk2_ref, qaug_ref,
                      q_ref, k_ref, v_ref, w_ref, o_ref,
                      k1a_sc, k2a_sc, vt_sc, a1, a2, st, sa1, sa2, sb1, sb2, *, tq, tk, lam_init):
    h = pl.program_id(1)
    qi = pl.program_id(2)
    slope = slopes_ref[h]
    n_chunks = k1a_sc.shape[0]

    @pl.when(qi == 0)
    def _():
        lane = lax.broadcasted_iota(jnp.int32, (tk, LANES), 1)
        aug1 = _key_aug(tk, HEAD_DIM).astype(BF16)
        aug2 = _key_aug(tk, 0).astype(BF16)

        def build(c, carry):
            rows = pl.ds(pl.multiple_of(c * tk, tk), tk)
            k = k_ref[rows, :]
            k1a_sc[c] = jnp.where(lane < HEAD_DIM, k, aug1)
            k2a_sc[c] = jnp.where(lane >= HEAD_DIM, k, aug2)
            vt_sc[c, 0:DV, :] = v_ref[rows, :].astype(F32).T.astype(BF16)
            vt_sc[c, DV:DV + ONES_ROWS, :] = jnp.where(
                lax.broadcasted_iota(jnp.int32, (ONES_ROWS, tk), 0) == 0, 1.0, 0.0).astype(BF16)
            return carry

        lax.fori_loop(0, n_chunks, build, 0)

    qt = q_ref[...].astype(F32).T
    row = lax.broadcasted_iota(jnp.int32, qt.shape, 0)
    qt1 = jnp.where(row < HEAD_DIM, qt, qaug_ref[0, 0]).astype(BF16)
    qt2 = jnp.where(row >= HEAD_DIM, qt, qaug_ref[0, 1]).astype(BF16)
    a1[...] = jnp.zeros(a1.shape, F32)
    a2[...] = jnp.zeros(a2.shape, F32)

    M1, M2, XA1, XA2 = range(4)
    for r_ in (M1, M2):
        st[r_:r_ + 1, :] = jnp.full((1, tq), NEG_BIG, F32)

    maps = ((k1a_sc, qt1, a1, M1, XA1), (k2a_sc, qt2, a2, M2, XA2))
    tc = min(tq, STRIP)
    pieces = [(mp, slice(h * tc, (h + 1) * tc)) for h in range(tq // tc) for mp in range(2)]

    def scores(j, mp, cols, dst, key0=None):
        s = jnp.dot(maps[mp][0][j], maps[mp][1][:, cols], preferred_element_type=F32)
        if key0 is not None:
            krow = lax.broadcasted_iota(jnp.int32, s.shape, 0) + key0
            qcol = lax.broadcasted_iota(jnp.int32, s.shape, 1) + cols.start
            s = jnp.where(krow <= qcol, s, NEG_BIG)
        dst[mp][:, cols] = s
        return jnp.max(s, axis=0, keepdims=True)

    def accumulate(j, mp, cols, src, mx):
        _, _, a_sc, mr, _ = maps[mp]
        c = slope * jnp.full((1, tc), j * tk - qi * tq, jnp.int32).astype(F32)
        m = st[mr:mr + 1, cols]
        m_new = jnp.maximum(m, mx + c)
        alpha = jnp.exp2(m - m_new)
        p = jnp.exp2(src[mp][:, cols] - (m_new - c))
        st[mr:mr + 1, cols] = m_new
        a_sc[:, cols] = alpha * a_sc[:, cols] + jnp.dot(vt_sc[j], p.astype(BF16),
                                                        preferred_element_type=F32)

    bufs = ((sa1, sa2), (sb1, sb2))

    def park(maxima):
        for (mp, cols), x in zip(pieces, maxima):
            xr = maps[mp][4]
            st[xr:xr + 1, cols] = x

    def parked():
        return [st[maps[mp][4]:maps[mp][4] + 1, cols] for mp, cols in pieces]

    def step(s_blk, s_dst, p_blk, p_src, p_max, s_pieces=None, p_pieces=None):
        s_pieces = [(mp, cols, None) for mp, cols in pieces] if s_pieces is None else s_pieces
        p_pieces = pieces if p_pieces is None else p_pieces
        out = []
        for k in range(max(len(s_pieces), len(p_pieces))):
            if k < len(s_pieces):
                mp, cols, key0 = s_pieces[k]
                out.append(scores(s_blk, mp, cols, s_dst, key0))
            if k < len(p_pieces):
                mp, cols = p_pieces[k]
                accumulate(p_blk, mp, cols, p_src, p_max[k])
        return out

    def diag_pieces(d):
        out = []
        for mp, cols in pieces:
            if d * tk + tk - 1 <= cols.start:
                out.append((mp, cols, None))
            elif d * tk <= cols.stop - 1:
                out.append((mp, cols, d * tk))
        return out

    d0, d1 = diag_pieces(0), diag_pieces(1)
    d0_cols = [(mp, cols) for mp, cols, _ in d0]
    d1_cols = [(mp, cols) for mp, cols, _ in d1]
    first_diag = 2 * qi

    def finish(cur, nxt, x_prev=None, prev_blk=None):
        if x_prev is None:
            x0 = [scores(first_diag, mp, cols, nxt, key0) for mp, cols, key0 in d0]
        else:
            x0 = step(first_diag, nxt, prev_blk, cur, x_prev, s_pieces=d0)
        x1 = step(first_diag + 1, cur, first_diag, nxt, x0, s_pieces=d1, p_pieces=d0_cols)
        for (mp, cols), x in zip(d1_cols, x1):
            accumulate(first_diag + 1, mp, cols, cur, x)

    @pl.when(qi == 0)
    def _():
        finish(bufs[0], bufs[1])

    @pl.when(qi > 0)
    def _():
        park([scores(0, mp, cols, bufs[0]) for mp, cols in pieces])

        def pair(i, carry):
            j = 2 * i
            x = parked()
            x = step(j + 1, bufs[1], j, bufs[0], x)
            x = step(j + 2, bufs[0], j + 1, bufs[1], x)
            park(x)
            return carry

        lax.fori_loop(0, qi - 1, pair, 0)
        j = 2 * (qi - 1)
        x = step(j + 1, bufs[1], j, bufs[0], parked())
        finish(bufs[1], bufs[0], x_prev=x, prev_blk=j + 1)

    lam = (jnp.exp(jnp.sum(lq1_ref[...] * lk1_ref[...], axis=1, keepdims=True))
           - jnp.exp(jnp.sum(lq2_ref[...] * lk2_ref[...], axis=1, keepdims=True))
           + lam_init)
    o1 = a1[0:DV, :] / a1[DV:DV + 1, :]
    o2 = a2[0:DV, :] / a2[DV:DV + 1, :]
    o = (o1 - lam * o2).T
    y = o * lax.rsqrt(jnp.mean(jnp.square(o), axis=1, keepdims=True) + RMS_EPS)
    y = (y * w_ref[...]) * (1.0 - lam_init)
    o_ref[...] = y.astype(o_ref.dtype)


def _diff_attention(proj, slopes_l2, qaug, lq1, lk1, lq2, lk2, subln_w, *, batch, seq, lam_init, tq):
    n = proj.shape[0]
    nq = seq // tq
    tk = tq // 2
    assert tk % STRIP == 0 and tk % LANES == 0
    nk = seq // tk
    kern = functools.partial(_diff_attn_kernel, tq=tq, tk=tk, lam_init=lam_init)
    vec = pl.BlockSpec((1, HEAD_DIM), lambda b, h, i: (0, 0))
    cq, ck, cv = _COL_DQ // LANES, _COL_DK // LANES, _COL_DV // LANES
    return pl.pallas_call(
        kern,
        out_shape=jax.ShapeDtypeStruct((n, DIFF_HEADS * 2 * HEAD_DIM), BF16),
        grid=(batch, DIFF_HEADS, nq),
        in_specs=[pl.BlockSpec(memory_space=pltpu.SMEM),
                  vec, vec, vec, vec,
                  pl.BlockSpec((1, 2, LANES, tq), lambda b, h, i: (h, 0, 0, 0)),
                  pl.BlockSpec((tq, LANES), lambda b, h, i: (b * nq + i, cq + h)),
                  pl.BlockSpec((seq, LANES), lambda b, h, i: (b, ck + h)),
                  pl.BlockSpec((seq, LANES), lambda b, h, i: (b, cv + h)),
                  pl.BlockSpec((1, 2 * HEAD_DIM), lambda b, h, i: (0, 0))],
        out_specs=pl.BlockSpec((tq, LANES), lambda b, h, i: (b * nq + i, h)),
        scratch_shapes=[pltpu.VMEM((nk, tk, LANES), BF16), pltpu.VMEM((nk, tk, LANES), BF16),
                        pltpu.VMEM((nk, DV + ONES_ROWS, tk), BF16),
                        pltpu.VMEM((DV + ONES_ROWS, tq), F32), pltpu.VMEM((DV + ONES_ROWS, tq), F32),
                        pltpu.VMEM((8, tq), F32)] + [pltpu.VMEM((tk, tq), F32)] * 4,
        compiler_params=_params(("arbitrary", "arbitrary", "arbitrary")),
        name="diff_attn",
    )(slopes_l2, lq1, lk1, lq2, lk2, qaug, proj, proj, proj, subln_w)


def _diff_query_aug(slopes_l2, tq):
    s0 = slopes_l2.astype(BF16).astype(F32)
    s1 = (slopes_l2 - s0).astype(BF16).astype(F32)
    s2 = (slopes_l2 - s0 - s1).astype(BF16).astype(F32)
    parts = jnp.stack([s0, s1, s2, s0, s1, s2], axis=1)
    n_heads = slopes_l2.shape[0]
    cols = jnp.zeros((n_heads, 2, LANES), F32)
    cols = cols.at[:, 0, HEAD_DIM:HEAD_DIM + N_AUG].set(parts)
    cols = cols.at[:, 1, 0:N_AUG].set(parts)
    return jnp.broadcast_to(cols[:, :, :, None], (n_heads, 2, LANES, tq))


def _swa_kernel(slopes_ref, sinks_ref, q_ref, kp_ref, kc_ref, vp_ref, vc_ref, o_ref,
                bias_sc, sink_sc, ot_sc, *, tq):
    qi = pl.program_id(1)
    wide = SWA_GROUP * WINDOW

    @pl.when((pl.program_id(0) == 0) & (qi == 0))
    def _():
        key = lax.broadcasted_iota(jnp.int32, (2 * WINDOW, wide), 0)
        col = lax.broadcasted_iota(jnp.int32, (2 * WINDOW, wide), 1)
        dist = (col & (WINDOW - 1)) + WINDOW - key
        valid = (dist >= 0) & (dist < WINDOW)
        head = lax.broadcasted_iota(jnp.int32, (1, wide), 1) >> (WINDOW.bit_length() - 1)
        for hk in range(SWA_KV_HEADS):
            slope = jnp.zeros((1, wide), F32)
            sink = jnp.zeros((1, wide), F32)
            for g in range(SWA_GROUP):
                slope = jnp.where(head == g, slopes_ref[hk * SWA_GROUP + g], slope)
                sink = jnp.where(head == g, sinks_ref[hk * SWA_GROUP + g], sink)
            bias_sc[hk] = jnp.where(valid, -slope * dist.astype(F32), NEG_BIG)
            sink_sc[hk] = sink

    kcat = jnp.concatenate([kp_ref[...], kc_ref[...]], axis=0)
    vcat = jnp.concatenate([vp_ref[...], vc_ref[...]], axis=0)
    zeros = jnp.zeros((HEAD_DIM, WINDOW), F32)
    ones_rows = jnp.where(lax.broadcasted_iota(jnp.int32, (ONES_ROWS, 2 * WINDOW), 0) == 0,
                          1.0, 0.0).astype(BF16)
    for j in range(tq // WINDOW):
        kj = kcat[j * WINDOW:(j + 2) * WINDOW]
        vt = vcat[j * WINDOW:(j + 2) * WINDOW].astype(F32).T
        qt = q_ref[j * WINDOW:(j + 1) * WINDOW, :].astype(F32).T
        for hk in range(SWA_KV_HEADS):
            cols = []
            for g in range(SWA_GROUP):
                hq = hk * SWA_GROUP + g
                qh = qt[hq * HEAD_DIM:(hq + 1) * HEAD_DIM]
                cols.append(jnp.concatenate([qh, zeros] if hk == 0 else [zeros, qh], axis=0))
            q8 = jnp.concatenate(cols, axis=1).astype(BF16)
            s = jnp.dot(kj, q8, preferred_element_type=F32) + bias_sc[hk]
            if j == 0:
                key = lax.broadcasted_iota(jnp.int32, s.shape, 0)
                s = jnp.where((key < WINDOW) & (qi == 0), NEG_BIG, s)
            sink = sink_sc[hk]
            m = jnp.maximum(jnp.max(s, axis=0, keepdims=True), sink)
            e = jnp.exp2(s - m)
            v_hk = jnp.concatenate([vt[hk * HEAD_DIM:(hk + 1) * HEAD_DIM].astype(BF16), ones_rows],
                                   axis=0)
            pv = jnp.dot(v_hk, e.astype(BF16), preferred_element_type=F32)
            denom = pv[HEAD_DIM:HEAD_DIM + 1] + jnp.exp2(sink - m)
            ot = pv[:HEAD_DIM] / denom
            for g in range(SWA_GROUP):
                hq = hk * SWA_GROUP + g
                ot_sc[hq * HEAD_DIM:(hq + 1) * HEAD_DIM, :] = ot[:, g * WINDOW:(g + 1) * WINDOW]
        o_ref[j * WINDOW:(j + 1) * WINDOW, :] = ot_sc[...].T.astype(o_ref.dtype)


def _swa_attention(proj, slopes, sinks, *, batch, seq, tq):
    n = proj.shape[0]
    nq = seq // tq
    sub = tq // WINDOW
    nwin = seq // WINDOW
    kern = functools.partial(_swa_kernel, tq=tq)
    cq = _COL_SQ // (SWA_HEADS * HEAD_DIM)
    ck, cv = _COL_SK // LANES, _COL_SV // LANES
    prev = lambda c: (lambda b, i: (b * nwin + jnp.maximum(i * sub - 1, 0), c))
    cur = lambda c: (lambda b, i: (b * nq + i, c))
    smem = pl.BlockSpec(memory_space=pltpu.SMEM)
    return pl.pallas_call(
        kern,
        out_shape=jax.ShapeDtypeStruct((n, SWA_HEADS * HEAD_DIM), BF16),
        grid=(batch, nq),
        in_specs=[smem, smem,
                  pl.BlockSpec((tq, SWA_HEADS * HEAD_DIM), cur(cq)),
                  pl.BlockSpec((WINDOW, LANES), prev(ck)),
                  pl.BlockSpec((tq, LANES), cur(ck)),
                  pl.BlockSpec((WINDOW, LANES), prev(cv)),
                  pl.BlockSpec((tq, LANES), cur(cv))],
        out_specs=pl.BlockSpec((tq, SWA_HEADS * HEAD_DIM), lambda b, i: (b * nq + i, 0)),
        scratch_shapes=[pltpu.VMEM((SWA_KV_HEADS, 2 * WINDOW, SWA_GROUP * WINDOW), F32),
                        pltpu.VMEM((SWA_KV_HEADS, 1, SWA_GROUP * WINDOW), F32),
                        pltpu.VMEM((SWA_HEADS * HEAD_DIM, WINDOW), F32)],
        compiler_params=_params(("arbitrary", "arbitrary")),
        name="swa_attn",
    )(slopes, sinks, proj, proj, proj, proj, proj)


def _layer_norm(y, g, b):
    mu = jnp.mean(y, axis=1, keepdims=True)
    var = jnp.mean(jnp.square(y - mu), axis=1, keepdims=True)
    return (y - mu) * lax.rsqrt(var + LN_EPS) * g + b


def _merge_kernel(x_ref, od_ref, os_ref, ga_ref, gb_ref, p_ref,
                  wa_ref, wb_ref, wo_ref, bo_ref, g1_ref, b1_ref, wpg_ref, wpp_ref,
                  x1_ref, r_ref, *, dn_alpha):
    a = jnp.dot(od_ref[...], wa_ref[...], preferred_element_type=F32)
    b = jnp.dot(os_ref[...], wb_ref[...], preferred_element_type=F32)
    merged = (jax.nn.sigmoid(ga_ref[...].astype(F32)) * a
              + jax.nn.sigmoid(gb_ref[...].astype(F32)) * b)
    mix = jnp.dot(merged.astype(BF16), wo_ref[...], preferred_element_type=F32) + bo_ref[...]
    x1 = _layer_norm(dn_alpha * x_ref[...] + mix, g1_ref[...], b1_ref[...])
    x1_ref[...] = x1
    gate = jax.nn.sigmoid(jnp.dot(x1.astype(BF16), wpg_ref[...], preferred_element_type=F32))
    ple = gate * jnp.dot(p_ref[...].astype(BF16), wpp_ref[...], preferred_element_type=F32)
    r_ref[...] = dn_alpha * x1 + ple


def _merge(x, od, osw, proj, p, wa, wb, wo, bo, g1, b1, wpg, wpp, *, dn_alpha, tm):
    n, d = x.shape
    pd = p.shape[1]
    row = lambda c: (lambda i: (i, c))
    full = lambda shape: pl.BlockSpec(shape, lambda i: (0, 0))
    kern = functools.partial(_merge_kernel, dn_alpha=dn_alpha)
    return pl.pallas_call(
        kern,
        out_shape=(jax.ShapeDtypeStruct((n, d), F32), jax.ShapeDtypeStruct((n, d), F32)),
        grid=(n // tm,),
        in_specs=[pl.BlockSpec((tm, d), row(0)),
                  pl.BlockSpec((tm, d), row(0)),
                  pl.BlockSpec((tm, d), row(0)),
                  pl.BlockSpec((tm, d), row(_COL_GA // d)),
                  pl.BlockSpec((tm, d), row(_COL_GB // d)),
                  pl.BlockSpec((tm, pd), row(0)),
                  full((d, d)), full((d, d)), full((d, d)), full((1, d)),
                  full((1, d)), full((1, d)), full((d, d)), full((pd, d))],
        out_specs=(pl.BlockSpec((tm, d), row(0)), pl.BlockSpec((tm, d), row(0))),
        compiler_params=_params(("parallel",)),
        name="merge_ln1",
    )(x, od, osw, proj, proj, p, wa, wb, wo, bo, g1, b1, wpg, wpp)


def _router_kernel(x_ref, w_ref, b_ref, code_ref, gate_ref, cnt_ref, carry, *, tm):
    @pl.when(pl.program_id(0) == 0)
    def _():
        carry[...] = jnp.zeros(carry.shape, F32)

    logits = jnp.dot(x_ref[...], w_ref[...], preferred_element_type=F32,
                     precision=lax.Precision.HIGHEST) + b_ref[...]
    lane = lax.broadcasted_iota(jnp.int32, logits.shape, 1)
    lanef = lane.astype(F32)
    work = logits
    tops, idxs = [], []
    onehot = jnp.zeros(logits.shape, F32)
    for _ in range(TOP_K):
        m = jnp.max(work, axis=1, keepdims=True)
        idx = jnp.min(jnp.where(work == m, lanef, float(LANES)), axis=1, keepdims=True)
        sel = lanef == idx
        onehot = jnp.where(sel, 1.0, onehot)
        work = jnp.where(sel, -jnp.inf, work)
        tops.append(m)
        idxs.append(idx)
    es = [jnp.exp(t - tops[0]) for t in tops]
    denom = es[0] + es[1] + es[2] + es[3]
    r = lax.broadcasted_iota(jnp.int32, (tm, tm), 0)
    c = lax.broadcasted_iota(jnp.int32, (tm, tm), 1)
    tri = jnp.where(c < r, 1.0, 0.0).astype(BF16)
    before = jnp.dot(tri, onehot.astype(BF16), preferred_element_type=F32) + carry[0:1, :]
    code = jnp.zeros(logits.shape, jnp.int32)
    gate = jnp.zeros(logits.shape, F32)
    for k in range(TOP_K):
        rank = jnp.sum(jnp.where(lanef == idxs[k], before, 0.0), axis=1, keepdims=True)
        ck = (idxs[k] * float(1 << RANK_BITS) + rank).astype(jnp.int32)
        code = jnp.where(lane == k, ck, code)
        gate = jnp.where(lane == k, es[k] / denom, gate)
    code_ref[...] = code
    gate_ref[...] = gate
    carry[0:1, :] = carry[0:1, :] + jnp.sum(onehot, axis=0, keepdims=True)
    cnt_ref[...] = carry[...]


def _router(x1, w, b, *, tm):
    n, d = x1.shape
    kern = functools.partial(_router_kernel, tm=tm)
    return pl.pallas_call(
        kern,
        out_shape=(jax.ShapeDtypeStruct((n, LANES), jnp.int32),
                   jax.ShapeDtypeStruct((n, LANES), F32),
                   jax.ShapeDtypeStruct((8, LANES), F32)),
        grid=(n // tm,),
        in_specs=[pl.BlockSpec((tm, d), lambda i: (i, 0)),
                  pl.BlockSpec((d, LANES), lambda i: (0, 0)),
                  pl.BlockSpec((1, LANES), lambda i: (0, 0))],
        out_specs=(pl.BlockSpec((tm, LANES), lambda i: (i, 0)),
                   pl.BlockSpec((tm, LANES), lambda i: (i, 0)),
                   pl.BlockSpec((8, LANES), lambda i: (0, 0))),
        scratch_shapes=[pltpu.VMEM((8, LANES), F32)],
        compiler_params=_params(("arbitrary",)),
        name="router",
    )(x1, w, b)


def _dispatch_kernel(tail_ref, codes_ref, x_ref, xs_ref, codes_smem, zeros, csem, sem, zsem,
                     *, rows):
    i = pl.program_id(0)
    n_steps = pl.num_programs(0)
    n_assign = rows * TOP_K

    def codes_load(step):
        return pltpu.make_async_copy(codes_ref.at[pl.ds(step * n_assign, n_assign)],
                                     codes_smem, csem)

    @pl.when(i == 0)
    def _():
        codes_load(0).start()
        zeros[...] = jnp.zeros(zeros.shape, zeros.dtype)

        def fill(e):
            first = pl.multiple_of(tail_ref[e], MOE_BLOCK)
            return pltpu.make_async_copy(zeros, xs_ref.at[pl.ds(first, MOE_BLOCK)], zsem)

        def start(e, carry):
            @pl.when(tail_ref[e] >= 0)
            def _():
                fill(e).start()
            return carry

        def finish(e, carry):
            @pl.when(tail_ref[e] >= 0)
            def _():
                fill(e).wait()
            return carry

        def unused(b):
            first = pl.multiple_of(b * MOE_BLOCK, MOE_BLOCK)
            return pltpu.make_async_copy(zeros, xs_ref.at[pl.ds(first, MOE_BLOCK)], zsem)

        n_used = tail_ref[N_EXPERTS]
        n_blocks = xs_ref.shape[0] // MOE_BLOCK
        lax.fori_loop(0, N_EXPERTS, start, 0)
        lax.fori_loop(n_used, n_blocks, lambda b, c: (unused(b).start(), c)[1], 0)
        lax.fori_loop(0, N_EXPERTS, finish, 0)
        lax.fori_loop(n_used, n_blocks, lambda b, c: (unused(b).wait(), c)[1], 0)

    codes_load(i).wait()

    def issue(t, carry):
        src = x_ref.at[pl.ds(t, 1)]
        for k in range(TOP_K):
            slot = codes_smem[t * TOP_K + k]
            pltpu.make_async_copy(src, xs_ref.at[pl.ds(slot, 1)], sem).start(priority=k % 2)
        return carry

    lax.fori_loop(0, rows, issue, 0, unroll=2)

    @pl.when(i + 1 < n_steps)
    def _():
        codes_load(i + 1).start()

    pltpu.make_async_copy(xs_ref.at[pl.ds(0, n_assign)], xs_ref.at[pl.ds(0, n_assign)], sem).wait()


def _dispatch(tail, codes, x1, n_slots, *, rows):
    n, d = x1.shape
    kern = functools.partial(_dispatch_kernel, rows=rows)
    any_spec = pl.BlockSpec(memory_space=pl.ANY)
    return pl.pallas_call(
        kern,
        out_shape=jax.ShapeDtypeStruct((n_slots, d), x1.dtype),
        grid_spec=pltpu.PrefetchScalarGridSpec(
            num_scalar_prefetch=1,
            grid=(n // rows,),
            in_specs=[any_spec, pl.BlockSpec((rows, d), lambda i, tail: (i, 0))],
            out_specs=any_spec,
            scratch_shapes=[pltpu.SMEM((rows * TOP_K,), jnp.int32),
                            pltpu.VMEM((MOE_BLOCK, d), x1.dtype),
                            pltpu.SemaphoreType.DMA, pltpu.SemaphoreType.DMA,
                            pltpu.SemaphoreType.DMA]),
        compiler_params=_dma_params(("arbitrary",)),
        name="moe_dispatch",
    )(tail, codes, x1)


def _expert_kernel(be_ref, nu_ref, xs_ref, wgu_ref, bgu_ref, wd_ref, bd_ref, y_ref, wgu_b, wd_b):
    j = pl.program_id(0)
    dff = wd_ref.shape[2]

    @pl.when(j < nu_ref[0])
    def _():
        @pl.when((j == 0) | (be_ref[j] != be_ref[jnp.maximum(j - 1, 0)]))
        def _():
            wgu_b[...] = wgu_ref[0, 0].astype(BF16)
            wd_b[...] = wd_ref[0, 0].astype(BF16)

        gu = jnp.dot(xs_ref[...].astype(BF16), wgu_b[...], preferred_element_type=F32) + bgu_ref[0, 0]
        gate = jnp.minimum(gu[:, :dff], SWIGLU_LIMIT)
        up = jnp.clip(gu[:, dff:], -SWIGLU_LIMIT, SWIGLU_LIMIT)
        act = (up + 1.0) * (gate * jax.nn.sigmoid(SWIGLU_ALPHA * gate))
        y_ref[...] = jnp.dot(act.astype(BF16), wd_b[...], preferred_element_type=F32) + bd_ref[0, 0]

    @pl.when(j >= nu_ref[0])
    def _():
        y_ref[...] = jnp.zeros(y_ref.shape, y_ref.dtype)


def _experts(block_e, n_used, xs, wgu, bgu, wd, bd, *, layer):
    n_slots, d = xs.shape
    dff = wd.shape[2]
    nb = n_slots // MOE_BLOCK
    blk = lambda j, be, nu: jnp.minimum(j, nu[0] - 1)
    row = lambda j, be, nu: (blk(j, be, nu), 0)
    wsel = lambda j, be, nu: (layer, be[blk(j, be, nu)], 0, 0)
    return pl.pallas_call(
        _expert_kernel,
        out_shape=jax.ShapeDtypeStruct((n_slots, d), F32),
        grid_spec=pltpu.PrefetchScalarGridSpec(
            num_scalar_prefetch=2,
            grid=(nb,),
            in_specs=[pl.BlockSpec((MOE_BLOCK, d), row),
                      pl.BlockSpec((1, 1, d, 2 * dff), wsel),
                      pl.BlockSpec((1, 1, 1, 2 * dff), wsel),
                      pl.BlockSpec((1, 1, dff, d), wsel),
                      pl.BlockSpec((1, 1, 1, d), wsel)],
            out_specs=pl.BlockSpec((MOE_BLOCK, d), lambda j, be, nu: (j, 0)),
            scratch_shapes=[pltpu.VMEM((d, 2 * dff), BF16), pltpu.VMEM((dff, d), BF16)]),
        compiler_params=_params(("arbitrary",)),
        name="moe_experts",
    )(block_e, n_used, xs, wgu, bgu, wd, bd)


def _combine_kernel(codes_ref, gate_ref, r_ref, y_ref, g2_ref, b2_ref,
                    x2_ref, xb_ref, codes_smem, ybuf0, ybuf1, csem, sem0, sem1, *, rows):
    i = pl.program_id(0)
    n_steps = pl.num_programs(0)
    n_assign = rows * TOP_K

    def codes_load(step):
        return pltpu.make_async_copy(codes_ref.at[pl.ds(step * n_assign, n_assign)],
                                     codes_smem, csem)

    def gather(step, ybuf, sem):
        def issue(t, carry):
            for k in range(TOP_K):
                slot = codes_smem[t * TOP_K + k]
                pltpu.make_async_copy(y_ref.at[pl.ds(slot, 1)],
                                      ybuf.at[k, pl.ds(t, 1)], sem).start(priority=k % 2)
            return carry

        lax.fori_loop(0, rows, issue, 0, unroll=2)

        @pl.when(step + 1 < n_steps)
        def _():
            codes_load(step + 1).start()

    @pl.when(i == 0)
    def _():
        first = codes_load(0)
        first.start()
        first.wait()
        gather(0, ybuf0, sem0)

    def step(ybuf, sem, ybuf_next, sem_next):
        @pl.when(i + 1 < n_steps)
        def _():
            codes_load(i + 1).wait()
            gather(i + 1, ybuf_next, sem_next)

        for k in range(TOP_K):
            pltpu.make_async_copy(y_ref.at[pl.ds(0, rows)], ybuf.at[k], sem).wait()
        gate = gate_ref[...]
        acc = r_ref[...]
        for k in range(TOP_K):
            acc = acc + gate[:, k:k + 1] * ybuf[k]
        x2 = _layer_norm(acc, g2_ref[...], b2_ref[...])
        x2_ref[...] = x2
        xb_ref[...] = x2.astype(BF16)

    @pl.when(i % 2 == 0)
    def _():
        step(ybuf0, sem0, ybuf1, sem1)

    @pl.when(i % 2 == 1)
    def _():
        step(ybuf1, sem1, ybuf0, sem0)


def _combine(codes, gates, r, y, g2, b2, *, rows):
    n, d = r.shape
    kern = functools.partial(_combine_kernel, rows=rows)
    any_spec = pl.BlockSpec(memory_space=pl.ANY)
    return pl.pallas_call(
        kern,
        out_shape=(jax.ShapeDtypeStruct((n, d), F32), jax.ShapeDtypeStruct((n, d), BF16)),
        grid=(n // rows,),
        in_specs=[any_spec,
                  pl.BlockSpec((rows, LANES), lambda i: (i, 0)),
                  pl.BlockSpec((rows, d), lambda i: (i, 0)),
                  any_spec,
                  pl.BlockSpec((1, d), lambda i: (0, 0)),
                  pl.BlockSpec((1, d), lambda i: (0, 0))],
        out_specs=(pl.BlockSpec((rows, d), lambda i: (i, 0)),
                   pl.BlockSpec((rows, d), lambda i: (i, 0))),
        scratch_shapes=[pltpu.SMEM((rows * TOP_K,), jnp.int32),
                        pltpu.VMEM((TOP_K, rows, d), F32), pltpu.VMEM((TOP_K, rows, d), F32),
                        pltpu.SemaphoreType.DMA, pltpu.SemaphoreType.DMA,
                        pltpu.SemaphoreType.DMA],
        compiler_params=_dma_params(("arbitrary",)),
        name="moe_combine_ln2",
    )(codes, gates, r, y, g2, b2)


def _alibi_slopes(n_heads):
    h = jnp.arange(1, n_heads + 1, dtype=F32)
    return jnp.exp2(-8.0 * h / n_heads)


def _block_size(n, target):
    t = min(n, target)
    while n % t:
        t //= 2
    return t


def kernel(x, p, w_in, b_in, lambda_q1, lambda_k1, lambda_q2, lambda_k2, subln_w, sinks,
           w_br_diff, w_br_swa, w_out, b_out, ln1_g, ln1_b, w_router, b_router,
           w_gate_up, b_gate_up, w_down, b_down, w_ple_gate, w_ple_proj, ln2_g, ln2_b):
    batch, seq, d = x.shape
    depth = w_in.shape[0]
    n = batch * seq
    dn_alpha = (2 * depth) ** 0.25
    n_assign = n * TOP_K
    n_blocks = n_assign // MOE_BLOCK + N_EXPERTS + 1
    n_slots = n_blocks * MOE_BLOCK

    n_in = w_in.shape[2]
    permute = lambda a: jnp.concatenate([a[..., n_in - 2 * d:], a[..., :n_in - 2 * d]], axis=-1)
    log2e = math.log2(math.e)
    diff_slopes = _alibi_slopes(DIFF_HEADS) * log2e
    swa_slopes = _alibi_slopes(SWA_HEADS) * log2e

    tm_lin = _block_size(n, 1024)
    tn_lin = _block_size(n_in, 1280)
    tq_diff = _block_size(seq, 1024)
    tq_swa = _block_size(seq, 512)
    tm_merge = _block_size(n, 512)
    tm_router = _block_size(n, 512)
    rows_moe = _block_size(n, 256)

    xf = x.reshape(n, d)
    xb = xf
    for i in range(depth):
        lam_init = 0.8 - 0.6 * math.exp(-0.3 * i)
        col_scale = jnp.ones((n_in,), F32).at[_COL_DQ:_COL_DK].set(HEAD_DIM ** -0.5 * log2e)
        col_scale = col_scale.at[_COL_SQ:_COL_SK].set(HEAD_DIM ** -0.5 * log2e)
        w_in_b = (permute(w_in[i]) * col_scale).astype(BF16)
        b_in_p = (permute(b_in[i]) * col_scale)[None, :]
        proj = _linear(xb, w_in_b, b_in_p, tm=tm_lin, tn=tn_lin)

        od = _diff_attention(proj, diff_slopes, _diff_query_aug(diff_slopes, tq_diff),
                             lambda_q1[i][None, :], lambda_k1[i][None, :],
                             lambda_q2[i][None, :], lambda_k2[i][None, :],
                             subln_w[i][None, :], batch=batch, seq=seq,
                             lam_init=lam_init, tq=tq_diff)
        osw = _swa_attention(proj, swa_slopes, sinks[i].astype(F32) * log2e,
                             batch=batch, seq=seq, tq=tq_swa)

        x1, r = _merge(xf, od, osw, proj, p[i].reshape(n, -1),
                       w_br_diff[i].astype(BF16), w_br_swa[i].astype(BF16),
                       w_out[i].astype(BF16), b_out[i][None, :],
                       ln1_g[i][None, :], ln1_b[i][None, :],
                       w_ple_gate[i].astype(BF16), w_ple_proj[i].astype(BF16),
                       dn_alpha=dn_alpha, tm=tm_merge)

        wr = jnp.zeros((d, LANES), F32).at[:, :N_EXPERTS].set(w_router[i])
        br = jnp.full((1, LANES), NEG_BIG, F32).at[0, :N_EXPERTS].set(b_router[i])
        code, gates, cnt = _router(x1, wr, br, tm=tm_router)

        counts = cnt[0, :N_EXPERTS].astype(jnp.int32)
        padded = (counts + MOE_BLOCK - 1) // MOE_BLOCK * MOE_BLOCK
        padded_end = jnp.cumsum(padded)
        off = (padded_end - padded).astype(jnp.int32)
        block_start = jnp.arange(n_blocks, dtype=jnp.int32) * MOE_BLOCK
        block_e = jnp.minimum(
            jnp.sum((block_start[:, None] >= padded_end[None, :]).astype(jnp.int32), axis=1),
            N_EXPERTS - 1)
        codes = code[:, :TOP_K].reshape(n_assign)
        codes = off[codes >> RANK_BITS] + (codes & ((1 << RANK_BITS) - 1))
        n_used = (padded_end[-1:] // MOE_BLOCK).astype(jnp.int32)
        tail = jnp.concatenate([jnp.where(padded > 0, padded_end - MOE_BLOCK, -1).astype(jnp.int32),
                                n_used])

        xs = _dispatch(tail, codes, x1, n_slots, rows=rows_moe)
        y = _experts(block_e, n_used, xs, w_gate_up, b_gate_up[:, :, None, :],
                     w_down, b_down[:, :, None, :], layer=i)
        xf, xb = _combine(codes, gates, r, y, ln2_g[i][None, :], ln2_b[i][None, :],
                          rows=rows_moe)
    return xf.reshape(batch, seq, d)
```
